```python
import math
import jax, jax.numpy as jnp
from jax import lax
import numpy as np

D_MODEL = 1024
BATCH = 8
SEQ = 4096
DEPTH = 4

CHUNK = 64

MIX_WIDTH = D_MODEL
SSM_WIDTH = MIX_WIDTH // 2
SSM_GROUP_CH = 16
SSM_GROUPS = SSM_WIDTH // SSM_GROUP_CH
SSM_STATE = 64
DT_MIN = 1e-3
DT_MAX = 1e-1
ATTN_WIDTH = MIX_WIDTH - SSM_WIDTH
HEAD_DIM = 64
N_HEADS = ATTN_WIDTH // HEAD_DIM
Q_BLOCK = 128
IN_WIDTH = SSM_WIDTH + 3 * ATTN_WIDTH + N_HEADS

PEER_HEADS = 8
PEER_N_KEYS = 128
PEER_EXPERTS = PEER_N_KEYS * PEER_N_KEYS
PEER_TOPK = 16
PEER_QUERY_DIM = 128
PEER_HALF = PEER_QUERY_DIM // 2
PEER_BLOCK = 128

RMS_EPS = 1e-6

kernel_name = "hybrid_s5_fox_peer_trunk"


def rms_norm(x, g):
    xf = x.astype(jnp.float32)
    y = xf * lax.rsqrt(jnp.mean(xf * xf, axis=-1, keepdims=True) + RMS_EPS)
    return (y * g.astype(jnp.float32)).astype(x.dtype)


def _ssm_combine(e1, e2):
    a1r, a1i, b1r, b1i = e1
    a2r, a2i, b2r, b2i = e2
    return (a2r * a1r - a2i * a1i,
            a2r * a1i + a2i * a1r,
            a2r * b1r - a2i * b1i + b2r,
            a2r * b1i + a2i * b1r + b2i)


def s5_mixer(u, lam_re, lam_im, log_dt, b_re, b_im, c_re, c_im, d_skip, w_glu, b_glu):
    bsz, seq, _ = u.shape
    f32 = jnp.float32
    uf = u.astype(f32).reshape(bsz, seq, SSM_GROUPS, SSM_GROUP_CH)
    dt = jnp.exp(log_dt.astype(f32))[:, None]
    lr = lam_re.astype(f32)
    li = lam_im.astype(f32)
    mag = jnp.exp(lr * dt)
    a_re = mag * jnp.cos(li * dt)
    a_im = mag * jnp.sin(li * dt)
    den = lr * lr + li * li
    nr = a_re - 1.0
    z_re = (nr * lr + a_im * li) / den
    z_im = (a_im * lr - nr * li) / den
    br = b_re.astype(f32)
    bi = b_im.astype(f32)
    bb_re = z_re[..., None] * br - z_im[..., None] * bi
    bb_im = z_re[..., None] * bi + z_im[..., None] * br
    x_re = jnp.einsum('bsgh,gph->bsgp', uf, bb_re)
    x_im = jnp.einsum('bsgh,gph->bsgp', uf, bb_im)
    ar = jnp.broadcast_to(a_re[None, None], (1, seq, SSM_GROUPS, SSM_STATE))
    ai = jnp.broadcast_to(a_im[None, None], (1, seq, SSM_GROUPS, SSM_STATE))
    _, _, s_re, s_im = lax.associative_scan(_ssm_combine, (ar, ai, x_re, x_im), axis=1)
    y = (jnp.einsum('bsgp,ghp->bsgh', s_re, c_re.astype(f32))
         - jnp.einsum('bsgp,ghp->bsgh', s_im, c_im.astype(f32))
         + d_skip.astype(f32) * uf)
    g = jax.nn.gelu(y.reshape(bsz, seq, SSM_WIDTH))
    out = g * jax.nn.sigmoid(g @ w_glu.astype(f32) + b_glu.astype(f32))
    return out.astype(u.dtype)


def fox_attention(q, k, v, f_logit):
    bsz, seq, _ = q.shape
    qh = q.reshape(bsz, seq, N_HEADS, HEAD_DIM).transpose(0, 2, 1, 3)
    kh = k.reshape(bsz, seq, N_HEADS, HEAD_DIM).transpose(0, 2, 1, 3)
    vh = v.reshape(bsz, seq, N_HEADS, HEAD_DIM).transpose(0, 2, 1, 3)
    log_f = jax.nn.log_sigmoid(f_logit.astype(jnp.float32))
    cum = jnp.cumsum(log_f, axis=1).transpose(0, 2, 1)
    scale = HEAD_DIM ** -0.5
    outs = []
    for blk in range(seq // Q_BLOCK):
        q0 = blk * Q_BLOCK
        q1 = q0 + Q_BLOCK
        logits = jnp.einsum('bhqd,bhkd->bhqk', qh[:, :, q0:q1], kh[:, :, :q1]).astype(jnp.float32) * scale
        logits = logits + cum[:, :, q0:q1, None] - cum[:, :, None, :q1]
        causal = jnp.arange(q0, q1)[:, None] >= jnp.arange(q1)[None, :]
        logits = jnp.where(causal, logits, -jnp.inf)
        probs = jax.nn.softmax(logits, axis=-1).astype(vh.dtype)
        outs.append(jnp.einsum('bhqk,bhkd->bhqd', probs, vh[:, :, :q1]))
    out = jnp.concatenate(outs, axis=2)
    return out.transpose(0, 2, 1, 3).reshape(bsz, seq, ATTN_WIDTH)


def peer_ffn(h, w_q, sub_keys, u_tab, v_tab):
    bsz, seq, d = h.shape
    q = (h @ w_q).reshape(bsz, seq, PEER_HEADS, 2, PEER_HALF)
    s1 = jnp.einsum('bshd,hnd->bshn', q[..., 0, :], sub_keys[:, 0]).astype(jnp.float32)
    s2 = jnp.einsum('bshd,hnd->bshn', q[..., 1, :], sub_keys[:, 1]).astype(jnp.float32)
    t1, i1 = lax.top_k(s1, PEER_TOPK)
    t2, i2 = lax.top_k(s2, PEER_TOPK)
    cand = (t1[..., :, None] + t2[..., None, :]).reshape(bsz, seq, PEER_HEADS, PEER_TOPK * PEER_TOPK)
    cand_idx = (i1[..., :, None] * PEER_N_KEYS + i2[..., None, :]).reshape(bsz, seq, PEER_HEADS, PEER_TOPK * PEER_TOPK)
    best, pos = lax.top_k(cand, PEER_TOPK)
    idx = jnp.take_along_axis(cand_idx, pos, axis=-1)
    gate = jax.nn.softmax(best, axis=-1)
    n_tok = bsz * seq
    n_blk = n_tok // PEER_BLOCK
    hf = h.reshape(n_blk, PEER_BLOCK, d)
    idxf = idx.reshape(n_blk, PEER_BLOCK, PEER_HEADS * PEER_TOPK)
    gf = gate.reshape(n_blk, PEER_BLOCK, PEER_HEADS * PEER_TOPK)

    def block(args):
        hb, ib, gb = args
        ub = jnp.take(u_tab, ib, axis=0)
        vb = jnp.take(v_tab, ib, axis=0)
        act = jax.nn.gelu(jnp.einsum('tkd,td->tk', ub, hb).astype(jnp.float32))
        coef = (act * gb).astype(hb.dtype)
        return jnp.einsum('tk,tkd->td', coef, vb)

    out = lax.map(block, (hf, idxf, gf))
    return out.reshape(bsz, seq, d)


def setup_inputs(seed: int = 0) -> dict:
    key = jax.random.key(seed)
    ks = jax.random.split(key, 24)
    f32 = jnp.float32
    L, D, G, P, Hc = DEPTH, D_MODEL, SSM_GROUPS, SSM_STATE, SSM_GROUP_CH
    nrm = lambda k, shape, s: jax.random.normal(k, shape, f32) * s
    x = jax.random.normal(ks[0], (BATCH, SEQ, D), f32)
    norm1_g = 1.0 + nrm(ks[1], (L, D), 0.01)
    w_in = nrm(ks[2], (L, D, IN_WIDTH), D ** -0.5)
    n_idx = jnp.arange(P, dtype=f32)
    ssm_lambda_re = -0.5 + nrm(ks[3], (L, G, P), 0.01)
    ssm_lambda_im = math.pi * n_idx + nrm(ks[4], (L, G, P), 0.01)
    ssm_log_dt = jax.random.uniform(ks[5], (L, G), f32, math.log(DT_MIN), math.log(DT_MAX))
    b_scale = (2.0 * Hc) ** -0.5
    ssm_b_re = nrm(ks[6], (L, G, P, Hc), b_scale)
    ssm_b_im = nrm(ks[7], (L, G, P, Hc), b_scale)
    c_scale = (2.0 * P) ** -0.5
    ssm_c_re = nrm(ks[8], (L, G, Hc, P), c_scale)
    ssm_c_im = nrm(ks[9], (L, G, Hc, P), c_scale)
    ssm_d = nrm(ks[10], (L, G, Hc), 1.0)
    ssm_w_glu = nrm(ks[11], (L, SSM_WIDTH, SSM_WIDTH), SSM_WIDTH ** -0.5)
    ssm_b_glu = nrm(ks[12], (L, SSM_WIDTH), 0.01)
    fox_b_f = jnp.linspace(1.0, 5.0, N_HEADS, dtype=f32)[None, :] + nrm(ks[13], (L, N_HEADS), 0.1)
    g_ssm_out = 1.0 + nrm(ks[14], (L, SSM_WIDTH), 0.01)
    g_attn_out = 1.0 + nrm(ks[15], (L, ATTN_WIDTH), 0.01)
    w_o = nrm(ks[16], (L, MIX_WIDTH, D), MIX_WIDTH ** -0.5)
    norm2_g = 1.0 + nrm(ks[17], (L, D), 0.01)
    peer_w_q = nrm(ks[18], (L, D, PEER_HEADS * PEER_QUERY_DIM), D ** -0.5)
    peer_keys = nrm(ks[19], (L, PEER_HEADS, 2, PEER_N_KEYS, PEER_HALF), PEER_HALF ** -0.5)
    peer_u = nrm(ks[20], (L, PEER_EXPERTS, D), D ** -0.5)
    peer_v = nrm(ks[21], (L, PEER_EXPERTS, D), 0.25)
    norm_f = 1.0 + nrm(ks[22], (D,), 0.01)
    return {"x": x, "norm1_g": norm1_g, "w_in": w_in,
            "ssm_lambda_re": ssm_lambda_re, "ssm_lambda_im": ssm_lambda_im, "ssm_log_dt": ssm_log_dt,
            "ssm_b_re": ssm_b_re, "ssm_b_im": ssm_b_im, "ssm_c_re": ssm_c_re, "ssm_c_im": ssm_c_im,
            "ssm_d": ssm_d, "ssm_w_glu": ssm_w_glu, "ssm_b_glu": ssm_b_glu,
            "fox_b_f": fox_b_f, "g_ssm_out": g_ssm_out, "g_attn_out": g_attn_out, "w_o": w_o,
            "norm2_g": norm2_g, "peer_w_q": peer_w_q, "peer_keys": peer_keys,
            "peer_u": peer_u, "peer_v": peer_v, "norm_f": norm_f}


def reference(x, norm1_g, w_in, ssm_lambda_re, ssm_lambda_im, ssm_log_dt, ssm_b_re, ssm_b_im,
              ssm_c_re, ssm_c_im, ssm_d, ssm_w_glu, ssm_b_glu, fox_b_f, g_ssm_out, g_attn_out, w_o,
              norm2_g, peer_w_q, peer_keys, peer_u, peer_v, norm_f):
    h = x
    o_q = SSM_WIDTH
    o_k = o_q + ATTN_WIDTH
    o_v = o_k + ATTN_WIDTH
    o_f = o_v + ATTN_WIDTH
    for l in range(DEPTH):
        xn = rms_norm(h, norm1_g[l])
        proj = xn @ w_in[l]
        u_ssm = proj[..., :o_q]
        q = proj[..., o_q:o_k]
        k = proj[..., o_k:o_v]
        v = proj[..., o_v:o_f]
        f_logit = proj[..., o_f:] + fox_b_f[l]
        y_ssm = s5_mixer(u_ssm, ssm_lambda_re[l], ssm_lambda_im[l], ssm_log_dt[l], ssm_b_re[l], ssm_b_im[l],
                         ssm_c_re[l], ssm_c_im[l], ssm_d[l], ssm_w_glu[l], ssm_b_glu[l])
        y_att = fox_attention(q, k, v, f_logit)
        mixed = jnp.concatenate([rms_norm(y_ssm, g_ssm_out[l]), rms_norm(y_att, g_attn_out[l])], axis=-1)
        h = h + mixed @ w_o[l]
        h = h + peer_ffn(rms_norm(h, norm2_g[l]), peer_w_q[l], peer_keys[l], peer_u[l], peer_v[l])
    return rms_norm(h, norm_f)
```

```python
import functools
import math

import jax
import jax.numpy as jnp
from jax import lax
from jax.experimental import pallas as pl
from jax.experimental.pallas import tpu as pltpu
from jax.experimental.pallas import tpu_sc as plsc

F32 = jnp.float32
BF16 = jnp.bfloat16
I32 = jnp.int32

RMS_EPS = 1e-6
SSM_GROUP_CH = 16
SSM_STATE = 64
SSM_CHUNK = 64
HEAD_DIM = 64
N_HEADS = 8
AUG = 128
PEER_HEADS = 8
PEER_KEYS = 128
PEER_TOPK = 16
PEER_HALF = 64
VMEM_LIMIT = 56 * 1024 * 1024


def _rms(x, g):
    return x * lax.rsqrt(jnp.mean(x * x, axis=-1, keepdims=True) + RMS_EPS) * g


def _gelu(x):
    c = math.sqrt(2.0 / math.pi)
    return 0.5 * x * (1.0 + jnp.tanh(c * (x + 0.044715 * (x * x * x))))


def _sigmoid(x):
    return 1.0 / (1.0 + jnp.exp(-x))


def _in_proj_kernel(h_ref, g_ref, wm_ref, wf_ref, bf_ref, u_ref, qa_ref, ka_ref, v_ref, cum_ref):
    j = pl.program_id(1)

    @pl.when(j == 0)
    def _():
        cum_ref[...] = jnp.zeros_like(cum_ref)

    x = h_ref[0]
    tm = x.shape[0]
    xn = _rms(x, g_ref[...])
    proj = jnp.dot(xn.astype(BF16), wm_ref[...], preferred_element_type=F32)
    f = jnp.dot(xn, wf_ref[...], precision=lax.Precision.HIGHEST, preferred_element_type=F32) + bf_ref[...]
    logf = jnp.minimum(f, 0.0) - jnp.log(1.0 + jnp.exp(-jnp.abs(f)))
    row = lax.broadcasted_iota(I32, (tm, tm), 0)
    col = lax.broadcasted_iota(I32, (tm, tm), 1)
    tri = (row >= col).astype(F32)
    cum = jnp.dot(tri, logf, precision=lax.Precision.HIGHEST, preferred_element_type=F32) + cum_ref[0:1, :]
    cum_ref[0:1, :] = cum[tm - 1:tm, :]

    w = HEAD_DIM * N_HEADS
    u_ref[0] = proj[:, :w].astype(BF16)
    lane = lax.broadcasted_iota(I32, (tm, AUG), 1)
    scale = HEAD_DIM ** -0.5
    for hh in range(N_HEADS):
        pair = hh // 2
        q2 = proj[:, w + 128 * pair: w + 128 * pair + 128]
        k2 = proj[:, 2 * w + 128 * pair: 2 * w + 128 * pair + 128]
        v2 = proj[:, 3 * w + 128 * pair: 3 * w + 128 * pair + 128]
        if hh % 2 == 1:
            q2 = pltpu.roll(q2, 64, axis=1)
            k2 = pltpu.roll(k2, 64, axis=1)
            vh = v2[:, 64:]
        else:
            vh = v2[:, :64]
        c = jnp.broadcast_to(cum[:, hh:hh + 1], (tm, AUG))
        c1 = c.astype(BF16).astype(F32)
        r1 = c - c1
        c2 = r1.astype(BF16).astype(F32)
        c3 = r1 - c2
        one = jnp.ones((tm, AUG), F32)
        zero = jnp.zeros((tm, AUG), F32)
        qa = jnp.where(lane < 64, q2 * scale,
             jnp.where(lane == 64, c1, jnp.where(lane == 65, c2, jnp.where(lane == 66, c3,
             jnp.where(lane < 70, one, zero)))))
        ka = jnp.where(lane < 64, k2,
             jnp.where(lane < 67, one, jnp.where(lane == 67, -c1, jnp.where(lane == 68, -c2,
             jnp.where(lane == 69, -c3, zero)))))
        qa_ref[0, hh] = qa.astype(BF16)
        ka_ref[0, hh] = ka.astype(BF16)
        v_ref[0, hh] = vh.astype(BF16)


def _in_proj(h, g, wm, wf, bf, tm):
    b, s, d = h.shape
    w = HEAD_DIM * N_HEADS
    return pl.pallas_call(
        _in_proj_kernel,
        grid=(b, s // tm),
        in_specs=[
            pl.BlockSpec((1, tm, d), lambda i, j: (i, j, 0)),
            pl.BlockSpec((1, d), lambda i, j: (0, 0)),
            pl.BlockSpec((d, 4 * w), lambda i, j: (0, 0)),
            pl.BlockSpec((d, 128), lambda i, j: (0, 0)),
            pl.BlockSpec((1, 128), lambda i, j: (0, 0)),
        ],
        out_specs=[
            pl.BlockSpec((1, tm, w), lambda i, j: (i, j, 0)),
            pl.BlockSpec((1, N_HEADS, tm, AUG), lambda i, j: (i, 0, j, 0)),
            pl.BlockSpec((1, N_HEADS, tm, AUG), lambda i, j: (i, 0, j, 0)),
            pl.BlockSpec((1, N_HEADS, tm, HEAD_DIM), lambda i, j: (i, 0, j, 0)),
        ],
        out_shape=[
            jax.ShapeDtypeStruct((b, s, w), BF16),
            jax.ShapeDtypeStruct((b, N_HEADS, s, AUG), BF16),
            jax.ShapeDtypeStruct((b, N_HEADS, s, AUG), BF16),
            jax.ShapeDtypeStruct((b, N_HEADS, s, HEAD_DIM), BF16),
        ],
        scratch_shapes=[pltpu.VMEM((8, 128), F32)],
        compiler_params=pltpu.CompilerParams(
            dimension_semantics=("parallel", "arbitrary"), vmem_limit_bytes=VMEM_LIMIT),
        name="in_proj",
    )(h, g, wm, wf, bf)


def _ssm_kernel(u_ref, m_ref, w_ref, r_ref, a_ref, d_ref, y_ref, e_scr, *, n_chunks, batch):
    u = u_ref[0]
    y = jnp.dot(u, m_ref[0], preferred_element_type=F32)
    e_scr[...] = jnp.dot(u, w_ref[0], preferred_element_type=F32)
    ar = a_ref[0, 0:1, :]
    ai = a_ref[0, 1:2, :]

    def body(c, s):
        off = pl.multiple_of(c * batch, batch)
        e_c = e_scr[pl.ds(off, batch), :]
        e_scr[pl.ds(off, batch), :] = s
        return ar * s + ai * pltpu.roll(s, SSM_STATE, axis=1) + e_c

    lax.fori_loop(0, n_chunks, body, jnp.zeros((batch, 2 * SSM_STATE), F32))
    y = y + jnp.dot(e_scr[...].astype(BF16), r_ref[0], preferred_element_type=F32)
    y_ref[0] = y + u.astype(F32) * d_ref[0]


def _ssm(ug, m, w, r, a, d, n_chunks, batch):
    g, rows, width = ug.shape
    kern = functools.partial(_ssm_kernel, n_chunks=n_chunks, batch=batch)
    return pl.pallas_call(
        kern,
        grid=(g,),
        in_specs=[
            pl.BlockSpec((1, rows, width), lambda i: (i, 0, 0)),
            pl.BlockSpec((1, width, width), lambda i: (i, 0, 0)),
            pl.BlockSpec((1, width, 2 * SSM_STATE), lambda i: (i, 0, 0)),
            pl.BlockSpec((1, 2 * SSM_STATE, width), lambda i: (i, 0, 0)),
            pl.BlockSpec((1, 8, 2 * SSM_STATE), lambda i: (i, 0, 0)),
            pl.BlockSpec((1, 1, width), lambda i: (i, 0, 0)),
        ],
        out_specs=pl.BlockSpec((1, rows, width), lambda i: (i, 0, 0)),
        out_shape=jax.ShapeDtypeStruct((g, rows, width), F32),
        scratch_shapes=[pltpu.VMEM((rows, 2 * SSM_STATE), F32)],
        compiler_params=pltpu.CompilerParams(
            dimension_semantics=("parallel",), vmem_limit_bytes=VMEM_LIMIT),
        name="ssm",
    )(ug, m, w, r, a, d)


def _ssm_tables(lam_re, lam_im, log_dt, b_re, b_im, c_re, c_im, d_skip):
    hp = lax.Precision.HIGHEST
    lc = SSM_CHUNK
    g, p = lam_re.shape
    hc = SSM_GROUP_CH
    dt = jnp.exp(log_dt)[:, None]
    a_re = jnp.exp(lam_re * dt) * jnp.cos(lam_im * dt)
    a_im = jnp.exp(lam_re * dt) * jnp.sin(lam_im * dt)
    den = lam_re * lam_re + lam_im * lam_im
    nr = a_re - 1.0
    z_re = (nr * lam_re + a_im * lam_im) / den
    z_im = (a_im * lam_re - nr * lam_im) / den
    bb_re = z_re[..., None] * b_re - z_im[..., None] * b_im
    bb_im = z_re[..., None] * b_im + z_im[..., None] * b_re
    tau = jnp.arange(lc + 1, dtype=F32)[:, None, None]
    mag = jnp.exp(tau * (lam_re * dt)[None])
    ang = tau * (lam_im * dt)[None]
    p_re = mag * jnp.cos(ang)
    p_im = mag * jnp.sin(ang)
    ab_re = p_re[:lc, :, :, None] * bb_re[None] - p_im[:lc, :, :, None] * bb_im[None]
    ab_im = p_re[:lc, :, :, None] * bb_im[None] + p_im[:lc, :, :, None] * bb_re[None]
    kk = (jnp.einsum('ghp,tgpk->tghk', c_re, ab_re, precision=hp)
          - jnp.einsum('ghp,tgpk->tghk', c_im, ab_im, precision=hp))
    t_idx = jnp.arange(lc)
    lag = t_idx[None, :] - t_idx[:, None]
    toe = jnp.where((lag >= 0)[:, :, None, None, None], kk[jnp.clip(lag, 0, lc - 1)], 0.0)
    m = toe.transpose(2, 0, 4, 1, 3).reshape(g, lc * hc, lc * hc)
    w_re = ab_re[::-1].transpose(1, 0, 3, 2).reshape(g, lc * hc, p)
    w_im = ab_im[::-1].transpose(1, 0, 3, 2).reshape(g, lc * hc, p)
    w = jnp.concatenate([w_re, w_im], axis=-1)
    q_re = p_re[1:, :, None, :] * c_re[None] - p_im[1:, :, None, :] * c_im[None]
    q_im = p_re[1:, :, None, :] * c_im[None] + p_im[1:, :, None, :] * c_re[None]
    r = jnp.concatenate([q_re.transpose(1, 3, 0, 2).reshape(g, p, lc * hc),
                         -q_im.transpose(1, 3, 0, 2).reshape(g, p, lc * hc)], axis=1)
    al_re, al_im = p_re[lc], p_im[lc]
    a = jnp.zeros((g, 8, 2 * p), F32)
    a = a.at[:, 0, :].set(jnp.concatenate([al_re, al_re], axis=-1))
    a = a.at[:, 1, :].set(jnp.concatenate([-al_im, al_im], axis=-1))
    d = jnp.tile(d_skip[:, None, :], (1, lc, 1)).reshape(g, 1, lc * hc)
    return m.astype(BF16), w.astype(BF16), r.astype(BF16), a, d


def _attn_kernel(q_ref, k_ref, v_ref, o_ref, m_ref, l_ref, acc_ref, *, blk):
    i = pl.program_id(1)
    q = q_ref[0]
    m_ref[...] = jnp.full_like(m_ref, -jnp.inf)
    l_ref[...] = jnp.zeros_like(l_ref)
    acc_ref[...] = jnp.zeros_like(acc_ref)

    def step(j, masked):
        off = pl.multiple_of(j * blk, blk)
        k = k_ref[0, pl.ds(off, blk), :]
        v = v_ref[0, pl.ds(off, blk), :]
        s = lax.dot_general(q, k, (((1,), (1,)), ((), ())), preferred_element_type=F32)
        if masked:
            row = lax.broadcasted_iota(I32, s.shape, 0)
            col = lax.broadcasted_iota(I32, s.shape, 1)
            s = jnp.where(row >= col, s, -jnp.inf)
        m_prev = m_ref[...]
        m_new = jnp.maximum(m_prev, jnp.max(s, axis=1, keepdims=True))
        p = jnp.exp(s - m_new)
        alpha = jnp.exp(m_prev - m_new)
        l_ref[...] = alpha * l_ref[...] + jnp.sum(p, axis=1, keepdims=True)
        acc_ref[...] = alpha * acc_ref[...] + jnp.dot(p.astype(BF16), v, preferred_element_type=F32)
        m_ref[...] = m_new

    def body(j, c):
        step(j, False)
        return c

    lax.fori_loop(0, i, body, 0)
    step(i, True)
    o_ref[0] = acc_ref[...] / l_ref[...]


def _attention(qa, ka, v, blk):
    bh, s, _ = qa.shape
    kern = functools.partial(_attn_kernel, blk=blk)
    return pl.pallas_call(
        kern,
        grid=(bh, s // blk),
        in_specs=[
            pl.BlockSpec((1, blk, AUG), lambda b, i: (b, i, 0)),
            pl.BlockSpec((1, s, AUG), lambda b, i: (b, 0, 0)),
            pl.BlockSpec((1, s, HEAD_DIM), lambda b, i: (b, 0, 0)),
        ],
        out_specs=pl.BlockSpec((1, blk, HEAD_DIM), lambda b, i: (b, i, 0)),
        out_shape=jax.ShapeDtypeStruct((bh, s, HEAD_DIM), F32),
        scratch_shapes=[pltpu.VMEM((blk, 1), F32), pltpu.VMEM((blk, 1), F32), pltpu.VMEM((blk, HEAD_DIM), F32)],
        compiler_params=pltpu.CompilerParams(
            dimension_semantics=("parallel", "arbitrary"), vmem_limit_bytes=VMEM_LIMIT),
        name="fox_attn",
    )(qa, ka, v)


def _out_proj_kernel(y_ref, att_ref, h_ref, wg_ref, bg_ref, gs_ref, ga_ref, wos_ref, woa_ref, o_ref):
    g = _gelu(y_ref[0])
    z = jnp.dot(g.astype(BF16), wg_ref[...], preferred_element_type=F32) + bg_ref[...]
    o = g * _sigmoid(z)
    a = _rms(o, gs_ref[...])
    acc = h_ref[0] + jnp.dot(a.astype(BF16), wos_ref[...], preferred_element_type=F32)
    ssq = jnp.zeros((o.shape[0], 1), F32)
    for hh in range(N_HEADS):
        t = att_ref[0, hh]
        ssq = ssq + jnp.sum(t * t, axis=1, keepdims=True)
    inv = lax.rsqrt(ssq / (N_HEADS * HEAD_DIM) + RMS_EPS)
    for hh in range(N_HEADS):
        bh = att_ref[0, hh] * inv * ga_ref[hh]
        acc = acc + jnp.dot(bh.astype(BF16), woa_ref[hh], preferred_element_type=F32)
    o_ref[0] = acc


def _out_proj(y_ssm, y_att, h, wg, bg, gs, ga, wos, woa, tm):
    b, s, d = h.shape
    w = y_ssm.shape[-1]
    return pl.pallas_call(
        _out_proj_kernel,
        grid=(b, s // tm),
        in_specs=[
            pl.BlockSpec((1, tm, w), lambda i, j: (i, j, 0)),
            pl.BlockSpec((1, N_HEADS, tm, HEAD_DIM), lambda i, j: (i, 0, j, 0)),
            pl.BlockSpec((1, tm, d), lambda i, j: (i, j, 0)),
            pl.BlockSpec((w, w), lambda i, j: (0, 0)),
            pl.BlockSpec((1, w), lambda i, j: (0, 0)),
            pl.BlockSpec((1, w), lambda i, j: (0, 0)),
            pl.BlockSpec((N_HEADS, 1, HEAD_DIM), lambda i, j: (0, 0, 0)),
            pl.BlockSpec((w, d), lambda i, j: (0, 0)),
            pl.BlockSpec((N_HEADS, HEAD_DIM, d), lambda i, j: (0, 0, 0)),
        ],
        out_specs=pl.BlockSpec((1, tm, d), lambda i, j: (i, j, 0)),
        out_shape=jax.ShapeDtypeStruct((b, s, d), F32),
        compiler_params=pltpu.CompilerParams(
            dimension_semantics=("parallel", "parallel"), vmem_limit_bytes=VMEM_LIMIT),
        name="out_proj",
    )(y_ssm, y_att, h, wg, bg, gs, ga, wos, woa)


def _take_top(vals, payload, k):
    n_rows = vals.shape[0]
    rows = lax.broadcasted_iota(I32, vals.shape, 0)
    tops, picks = [], []
    for _ in range(k):
        m = jnp.max(vals, axis=0, keepdims=True)
        arg = jnp.min(jnp.where(vals == m, rows, n_rows), axis=0, keepdims=True)
        hit = rows == arg
        tops.append(m)
        picks.append(arg if payload is None else jnp.max(jnp.where(hit, payload, -1), axis=0, keepdims=True))
        vals = jnp.where(hit, -jnp.inf, vals)
    return jnp.concatenate(tops, axis=0), jnp.concatenate(picks, axis=0)


def _route_kernel(h_ref, g_ref, wq_ref, keys_ref, hn_ref, idx_ref, gate_ref):
    x = h_ref[...]
    hn = _rms(x, g_ref[...])
    hn_ref[...] = hn
    q = jnp.dot(hn.astype(BF16), wq_ref[...], preferred_element_type=F32)
    k = PEER_TOPK
    idx_rows, gate_rows = [], []
    for hh in range(PEER_HEADS):
        qh = q[:, 128 * hh: 128 * (hh + 1)].astype(BF16)
        sc = lax.dot_general(keys_ref[hh], qh, (((1,), (1,)), ((), ())), preferred_element_type=F32)
        t1, i1 = _take_top(sc[:PEER_KEYS], None, k)
        t2, i2 = _take_top(sc[PEER_KEYS:], None, k)
        cand = jnp.concatenate([t1[a:a + 1] + t2 for a in range(k)], axis=0)
        cidx = jnp.concatenate([i1[a:a + 1] * PEER_KEYS + i2 for a in range(k)], axis=0)
        best, idx = _take_top(cand, cidx, k)
        e = jnp.exp(best - best[0:1])
        gate = e / jnp.sum(e, axis=0, keepdims=True)
        idx_rows.append(idx)
        gate_rows.append(gate)
    idx_ref[...] = jnp.concatenate(idx_rows, axis=0).T
    gate_ref[...] = jnp.concatenate(gate_rows, axis=0).T


def _route(h2, g, wq, keys_cat, tm):
    t, d = h2.shape
    nk = PEER_HEADS * PEER_TOPK
    return pl.pallas_call(
        _route_kernel,
        grid=(t // tm,),
        in_specs=[
            pl.BlockSpec((tm, d), lambda i: (i, 0)),
            pl.BlockSpec((1, d), lambda i: (0, 0)),
            pl.BlockSpec((d, PEER_HEADS * 128), lambda i: (0, 0)),
            pl.BlockSpec((PEER_HEADS, 2 * PEER_KEYS, 128), lambda i: (0, 0, 0)),
        ],
        out_specs=[
            pl.BlockSpec((tm, d), lambda i: (i, 0)),
            pl.BlockSpec((tm, nk), lambda i: (i, 0)),
            pl.BlockSpec((tm, nk), lambda i: (i, 0)),
        ],
        out_shape=[
            jax.ShapeDtypeStruct((t, d), F32),
            jax.ShapeDtypeStruct((t, nk), I32),
            jax.ShapeDtypeStruct((t, nk), F32),
        ],
        compiler_params=pltpu.CompilerParams(
            dimension_semantics=("parallel",), vmem_limit_bytes=VMEM_LIMIT),
        name="peer_route",
    )(h2, g, wq, keys_cat)


def _coef_kernel(a_ref, g_ref, o_ref):
    o_ref[...] = _gelu(a_ref[...]) * g_ref[...]


def _coef(act, gate, tm):
    t, n = act.shape
    return pl.pallas_call(
        _coef_kernel,
        grid=(t // tm,),
        in_specs=[pl.BlockSpec((tm, n), lambda i: (i, 0)), pl.BlockSpec((tm, n), lambda i: (i, 0))],
        out_specs=pl.BlockSpec((tm, n), lambda i: (i, 0)),
        out_shape=jax.ShapeDtypeStruct((t, n), F32),
        compiler_params=pltpu.CompilerParams(dimension_semantics=("parallel",)),
        name="peer_coef",
    )(act, gate)


def _final_norm_kernel(x_ref, g_ref, o_ref):
    o_ref[...] = _rms(x_ref[...], g_ref[...])


def _final_norm(x2, g, tm):
    t, d = x2.shape
    return pl.pallas_call(
        _final_norm_kernel,
        grid=(t // tm,),
        in_specs=[pl.BlockSpec((tm, d), lambda i: (i, 0)), pl.BlockSpec((1, d), lambda i: (0, 0))],
        out_specs=pl.BlockSpec((tm, d), lambda i: (i, 0)),
        out_shape=jax.ShapeDtypeStruct((t, d), F32),
        compiler_params=pltpu.CompilerParams(dimension_semantics=("parallel",)),
        name="final_norm",
    )(x2, g)


SC_WORKERS = 32
SC_CORES = 2
SC_LANES = 16
SC_TOK = 8
SC_ROWS = 32
SC_COLS = 256


def _sc_params():
    cp = pltpu.CompilerParams()
    if "needs_layout_passes" in pltpu.CompilerParams.__dataclass_fields__:
        cp = pltpu.CompilerParams(needs_layout_passes=False)
    return cp


def _sc_worker_id():
    return lax.axis_index("s") * SC_CORES + lax.axis_index("c")


def _sc_pipeline(tab_hbm, idx_v, rows, sems, n_chunks, compute):
    def gather(k, slot):
        return pltpu.make_async_copy(tab_hbm.at[idx_v.at[k]], rows[slot], sems[slot])

    gather(0, 0).start()

    def pair(i, carry):
        k0 = 2 * i
        gather(k0 + 1, 1).start()
        gather(k0, 0).wait()
        compute(rows[0], k0)

        @pl.when(k0 + 2 < n_chunks)
        def _():
            gather(k0 + 2, 0).start()

        gather(k0 + 1, 1).wait()
        compute(rows[1], k0 + 1)
        return carry

    lax.fori_loop(0, n_chunks // 2, pair, 0)


def _peer_scores_body(hn_hbm, idx_hbm, tab_hbm, out_hbm, h_v, idx_v, rows0, rows1, acc_v, act_v, sem0, sem1,
                      *, tok_per_worker):
    d = hn_hbm.shape[1]
    per_tok = PEER_HEADS * PEER_TOPK // SC_ROWS
    n_chunks = SC_TOK * per_tok
    lane = lax.iota(I32, SC_LANES)
    wid = _sc_worker_id()

    def compute(rows, k):
        t = k // per_tok
        c = k % per_tok
        for cc in range(d // SC_COLS):
            hv = [h_v[t, pl.ds(cc * SC_COLS + SC_LANES * j, SC_LANES)] for j in range(SC_COLS // SC_LANES)]

            def rbody(r, carry):
                p = rows[r, pl.ds(cc * SC_COLS, SC_LANES)] * hv[0]
                for j in range(1, SC_COLS // SC_LANES):
                    p = p + rows[r, pl.ds(cc * SC_COLS + SC_LANES * j, SC_LANES)] * hv[j]
                off = pl.multiple_of(r * SC_LANES, SC_LANES)
                if cc == 0:
                    acc_v[pl.ds(off, SC_LANES)] = p
                else:
                    acc_v[pl.ds(off, SC_LANES)] = acc_v[pl.ds(off, SC_LANES)] + p
                return carry

            lax.fori_loop(0, SC_ROWS, rbody, 0)
        for rg in range(SC_ROWS // SC_LANES):
            base = rg * SC_LANES * SC_LANES
            tot = plsc.load_gather(acc_v, [lane * SC_LANES + base])
            for j in range(1, SC_LANES):
                tot = tot + plsc.load_gather(acc_v, [lane * SC_LANES + (base + j)])
            off = pl.multiple_of(c * SC_ROWS + rg * SC_LANES, SC_LANES)
            act_v[t, pl.ds(off, SC_LANES)] = tot

    def batch(bi, carry):
        t0 = pl.multiple_of(wid * tok_per_worker + bi * SC_TOK, SC_TOK)
        pltpu.sync_copy(hn_hbm.at[pl.ds(t0, SC_TOK)], h_v)
        pltpu.sync_copy(idx_hbm.at[pl.ds(t0 * per_tok, n_chunks)], idx_v)
        _sc_pipeline(tab_hbm, idx_v, (rows0, rows1), (sem0, sem1), n_chunks, compute)
        pltpu.sync_copy(act_v, out_hbm.at[pl.ds(t0, SC_TOK)])
        return carry

    lax.fori_loop(0, tok_per_worker // SC_TOK, batch, 0)


def _peer_scores(hn, idx4, u_tab):
    t, d = hn.shape
    nk = PEER_HEADS * PEER_TOPK
    mesh = plsc.VectorSubcoreMesh(core_axis_name="c", subcore_axis_name="s")
    body = functools.partial(_peer_scores_body, tok_per_worker=t // SC_WORKERS)
    return pl.kernel(
        body,
        out_type=jax.ShapeDtypeStruct((t, nk), F32),
        mesh=mesh,
        scratch_types=[
            pltpu.VMEM((SC_TOK, d), F32),
            pltpu.VMEM((SC_TOK * nk // SC_ROWS, SC_ROWS), I32),
            pltpu.VMEM((SC_ROWS, d), F32),
            pltpu.VMEM((SC_ROWS, d), F32),
            pltpu.VMEM((SC_ROWS * SC_LANES,), F32),
            pltpu.VMEM((SC_TOK, nk), F32),
            pltpu.SemaphoreType.DMA,
            pltpu.SemaphoreType.DMA,
        ],
        compiler_params=_sc_params(),
        name="peer_scores_sc",
    )(hn, idx4, u_tab)


def _peer_combine_body(h_hbm, idx_hbm, coef_hbm, tab_hbm, out_hbm, o_v, idx_v, coef_v, rows0, rows1, sem0, sem1,
                       *, tok_per_worker):
    d = h_hbm.shape[1]
    nk = PEER_HEADS * PEER_TOPK
    per_tok = nk // SC_ROWS
    n_chunks = SC_TOK * per_tok
    n_reg = SC_COLS // SC_LANES
    wid = _sc_worker_id()

    def compute(rows, k):
        t = k // per_tok
        for cc in range(d // SC_COLS):
            acc0 = tuple(o_v[t, pl.ds(cc * SC_COLS + SC_LANES * j, SC_LANES)] for j in range(n_reg))

            def rbody(r, acc):
                w = plsc.load_gather(coef_v, [jnp.full((SC_LANES,), k * SC_ROWS + r, I32)])
                return tuple(acc[j] + w * rows[r, pl.ds(cc * SC_COLS + SC_LANES * j, SC_LANES)]
                             for j in range(n_reg))

            acc = lax.fori_loop(0, SC_ROWS, rbody, acc0)
            for j in range(n_reg):
                o_v[t, pl.ds(cc * SC_COLS + SC_LANES * j, SC_LANES)] = acc[j]

    def batch(bi, carry):
        t0 = pl.multiple_of(wid * tok_per_worker + bi * SC_TOK, SC_TOK)
        pltpu.sync_copy(h_hbm.at[pl.ds(t0, SC_TOK)], o_v)
        pltpu.sync_copy(idx_hbm.at[pl.ds(t0 * per_tok, n_chunks)], idx_v)
        pltpu.sync_copy(coef_hbm.at[pl.ds(t0 * nk, SC_TOK * nk)], coef_v)
        _sc_pipeline(tab_hbm, idx_v, (rows0, rows1), (sem0, sem1), n_chunks, compute)
        pltpu.sync_copy(o_v, out_hbm.at[pl.ds(t0, SC_TOK)])
        return carry

    lax.fori_loop(0, tok_per_worker // SC_TOK, batch, 0)


def _peer_combine(h2, idx4, coef_flat, v_tab):
    t, d = h2.shape
    nk = PEER_HEADS * PEER_TOPK
    mesh = plsc.VectorSubcoreMesh(core_axis_name="c", subcore_axis_name="s")
    body = functools.partial(_peer_combine_body, tok_per_worker=t // SC_WORKERS)
    return pl.kernel(
        body,
        out_type=jax.ShapeDtypeStruct((t, d), F32),
        mesh=mesh,
        scratch_types=[
            pltpu.VMEM((SC_TOK, d), F32),
            pltpu.VMEM((SC_TOK * nk // SC_ROWS, SC_ROWS), I32),
            pltpu.VMEM((SC_TOK * nk,), F32),
            pltpu.VMEM((SC_ROWS, d), F32),
            pltpu.VMEM((SC_ROWS, d), F32),
            pltpu.SemaphoreType.DMA,
            pltpu.SemaphoreType.DMA,
        ],
        compiler_params=_sc_params(),
        name="peer_combine_sc",
    )(h2, idx4, coef_flat, v_tab)


def _mixers(h, l, norm1_g, w_in, ssm, fox_b_f, g_ssm_out, g_attn_out, w_o, ssm_w_glu, ssm_b_glu, tm, blk):
    b, s, d = h.shape
    w = N_HEADS * HEAD_DIM
    wl = w_in[l]
    wm = wl[:, :4 * w].astype(BF16)
    wf = jnp.pad(wl[:, 4 * w:], ((0, 0), (0, 128 - N_HEADS)))
    bf = jnp.pad(fox_b_f[l], (0, 128 - N_HEADS)).reshape(1, 128)
    u, qa, ka, v = _in_proj(h, norm1_g[l].reshape(1, d), wm, wf, bf, tm)

    lc = SSM_CHUNK
    nc = s // lc
    g = w // SSM_GROUP_CH
    ug = u.reshape(b, nc, lc, g, SSM_GROUP_CH).transpose(3, 1, 0, 2, 4).reshape(g, nc * b, lc * SSM_GROUP_CH)
    yg = _ssm(ug, *ssm, n_chunks=nc, batch=b)
    y_ssm = yg.reshape(g, nc, b, lc, SSM_GROUP_CH).transpose(2, 1, 3, 0, 4).reshape(b, s, w)

    y_att = _attention(qa.reshape(b * N_HEADS, s, AUG), ka.reshape(b * N_HEADS, s, AUG),
                       v.reshape(b * N_HEADS, s, HEAD_DIM), blk).reshape(b, N_HEADS, s, HEAD_DIM)

    wo = w_o[l].astype(BF16)
    return _out_proj(y_ssm, y_att, h, ssm_w_glu[l].astype(BF16), ssm_b_glu[l].reshape(1, w),
                     g_ssm_out[l].reshape(1, w), g_attn_out[l].reshape(N_HEADS, 1, HEAD_DIM),
                     wo[:w], wo[w:].reshape(N_HEADS, HEAD_DIM, d), tm)


def _keys_cat(keys_l):
    z = jnp.zeros_like(keys_l[:, 0])
    top = jnp.concatenate([keys_l[:, 0], z], axis=-1)
    bot = jnp.concatenate([z, keys_l[:, 1]], axis=-1)
    return jnp.concatenate([top, bot], axis=1).astype(BF16)


def kernel(x, norm1_g, w_in, ssm_lambda_re, ssm_lambda_im, ssm_log_dt, ssm_b_re, ssm_b_im, ssm_c_re, ssm_c_im, ssm_d, ssm_w_glu, ssm_b_glu, fox_b_f, g_ssm_out, g_attn_out, w_o, norm2_g, peer_w_q, peer_keys, peer_u, peer_v, norm_f):
    b, s, d = x.shape
    depth = w_in.shape[0]
    t = b * s
    nk = PEER_HEADS * PEER_TOPK
    tm = min(512, s)
    blk = min(256, s)
    h = x
    for l in range(depth):
        ssm = _ssm_tables(ssm_lambda_re[l], ssm_lambda_im[l], ssm_log_dt[l], ssm_b_re[l], ssm_b_im[l],
                          ssm_c_re[l], ssm_c_im[l], ssm_d[l])
        h = _mixers(h, l, norm1_g, w_in, ssm, fox_b_f, g_ssm_out, g_attn_out, w_o, ssm_w_glu, ssm_b_glu, tm, blk)
        h2 = h.reshape(t, d)
        hn, idx, gate = _route(h2, norm2_g[l].reshape(1, d), peer_w_q[l].astype(BF16), _keys_cat(peer_keys[l]),
                               min(256, t))
        idx4 = idx.reshape(t * nk // SC_ROWS, SC_ROWS)
        act = _peer_scores(hn, idx4, peer_u[l])
        coef = _coef(act, gate, min(1024, t))
        h = _peer_combine(h2, idx4, coef.reshape(t * nk), peer_v[l]).reshape(b, s, d)
    return _final_norm(h.reshape(t, d), norm_f.reshape(1, d), min(512, t)).reshape(b, s, d)
```

```python
import functools
import math

import jax
import jax.numpy as jnp
from jax import lax
from jax.experimental import pallas as pl
from jax.experimental.pallas import tpu as pltpu
from jax.experimental.pallas import tpu_sc as plsc

F32 = jnp.float32
BF16 = jnp.bfloat16
I32 = jnp.int32

RMS_EPS = 1e-6
SSM_GROUP_CH = 16
SSM_STATE = 64
SSM_CHUNK = 64
HEAD_DIM = 64
N_HEADS = 8
AUG = 128
PEER_HEADS = 8
PEER_KEYS = 128
PEER_TOPK = 16
PEER_HALF = 64
VMEM_LIMIT = 56 * 1024 * 1024


def _rms(x, g):
    return x * lax.rsqrt(jnp.mean(x * x, axis=-1, keepdims=True) + RMS_EPS) * g


def _gelu(x):
    c = math.sqrt(2.0 / math.pi)
    return 0.5 * x * (1.0 + jnp.tanh(c * (x + 0.044715 * (x * x * x))))


def _sigmoid(x):
    return 1.0 / (1.0 + jnp.exp(-x))


def _in_proj_kernel(h_ref, g_ref, wm_ref, wf_ref, bf_ref, u_ref, qa_ref, ka_ref, v_ref, cum_ref):
    j = pl.program_id(1)

    @pl.when(j == 0)
    def _():
        cum_ref[...] = jnp.zeros_like(cum_ref)

    x = h_ref[0]
    tm = x.shape[0]
    xn = _rms(x, g_ref[...])
    proj = jnp.dot(xn.astype(BF16), wm_ref[...], preferred_element_type=F32)
    f = jnp.dot(xn, wf_ref[...], precision=lax.Precision.HIGHEST, preferred_element_type=F32) + bf_ref[...]
    logf = jnp.minimum(f, 0.0) - jnp.log(1.0 + jnp.exp(-jnp.abs(f)))
    row = lax.broadcasted_iota(I32, (tm, tm), 0)
    col = lax.broadcasted_iota(I32, (tm, tm), 1)
    tri = (row >= col).astype(F32)
    cum = jnp.dot(tri, logf, precision=lax.Precision.HIGHEST, preferred_element_type=F32) + cum_ref[0:1, :]
    cum_ref[0:1, :] = cum[tm - 1:tm, :]

    w = HEAD_DIM * N_HEADS
    u_ref[0] = proj[:, :w].astype(BF16)
    lane = lax.broadcasted_iota(I32, (tm, AUG), 1)
    scale = HEAD_DIM ** -0.5
    for hh in range(N_HEADS):
        pair = hh // 2
        q2 = proj[:, w + 128 * pair: w + 128 * pair + 128]
        k2 = proj[:, 2 * w + 128 * pair: 2 * w + 128 * pair + 128]
        v2 = proj[:, 3 * w + 128 * pair: 3 * w + 128 * pair + 128]
        if hh % 2 == 1:
            q2 = pltpu.roll(q2, 64, axis=1)
            k2 = pltpu.roll(k2, 64, axis=1)
            vh = v2[:, 64:]
        else:
            vh = v2[:, :64]
        c = jnp.broadcast_to(cum[:, hh:hh + 1], (tm, AUG))
        c1 = c.astype(BF16).astype(F32)
        r1 = c - c1
        c2 = r1.astype(BF16).astype(F32)
        c3 = r1 - c2
        one = jnp.ones((tm, AUG), F32)
        zero = jnp.zeros((tm, AUG), F32)
        qa = jnp.where(lane < 64, q2 * scale,
             jnp.where(lane == 64, c1, jnp.where(lane == 65, c2, jnp.where(lane == 66, c3,
             jnp.where(lane < 70, one, zero)))))
        ka = jnp.where(lane < 64, k2,
             jnp.where(lane < 67, one, jnp.where(lane == 67, -c1, jnp.where(lane == 68, -c2,
             jnp.where(lane == 69, -c3, zero)))))
        qa_ref[0, hh] = qa.astype(BF16)
        ka_ref[0, hh] = ka.astype(BF16)
        v_ref[0, hh] = vh.astype(BF16)


def _in_proj(h, g, wm, wf, bf, tm):
    b, s, d = h.shape
    w = HEAD_DIM * N_HEADS
    return pl.pallas_call(
        _in_proj_kernel,
        grid=(b, s // tm),
        in_specs=[
            pl.BlockSpec((1, tm, d), lambda i, j: (i, j, 0)),
            pl.BlockSpec((1, d), lambda i, j: (0, 0)),
            pl.BlockSpec((d, 4 * w), lambda i, j: (0, 0)),
            pl.BlockSpec((d, 128), lambda i, j: (0, 0)),
            pl.BlockSpec((1, 128), lambda i, j: (0, 0)),
        ],
        out_specs=[
            pl.BlockSpec((1, tm, w), lambda i, j: (i, j, 0)),
            pl.BlockSpec((1, N_HEADS, tm, AUG), lambda i, j: (i, 0, j, 0)),
            pl.BlockSpec((1, N_HEADS, tm, AUG), lambda i, j: (i, 0, j, 0)),
            pl.BlockSpec((1, N_HEADS, tm, HEAD_DIM), lambda i, j: (i, 0, j, 0)),
        ],
        out_shape=[
            jax.ShapeDtypeStruct((b, s, w), BF16),
            jax.ShapeDtypeStruct((b, N_HEADS, s, AUG), BF16),
            jax.ShapeDtypeStruct((b, N_HEADS, s, AUG), BF16),
            jax.ShapeDtypeStruct((b, N_HEADS, s, HEAD_DIM), BF16),
        ],
        scratch_shapes=[pltpu.VMEM((8, 128), F32)],
        compiler_params=pltpu.CompilerParams(
            dimension_semantics=("parallel", "arbitrary"), vmem_limit_bytes=VMEM_LIMIT),
        name="in_proj",
    )(h, g, wm, wf, bf)


def _ssm_kernel(u_ref, m_ref, w_ref, r_ref, a_ref, d_ref, y_ref, e_scr, *, n_chunks, batch):
    u = u_ref[0]
    y = jnp.dot(u, m_ref[0], preferred_element_type=F32)
    e_scr[...] = jnp.dot(u, w_ref[0], preferred_element_type=F32)
    ar = a_ref[0, 0:1, :]
    ai = a_ref[0, 1:2, :]

    def body(c, s):
        off = pl.multiple_of(c * batch, batch)
        e_c = e_scr[pl.ds(off, batch), :]
        e_scr[pl.ds(off, batch), :] = s
        return ar * s + ai * pltpu.roll(s, SSM_STATE, axis=1) + e_c

    lax.fori_loop(0, n_chunks, body, jnp.zeros((batch, 2 * SSM_STATE), F32))
    y = y + jnp.dot(e_scr[...].astype(BF16), r_ref[0], preferred_element_type=F32)
    y_ref[0] = y + u.astype(F32) * d_ref[0]


def _ssm(ug, m, w, r, a, d, n_chunks, batch):
    g, rows, width = ug.shape
    kern = functools.partial(_ssm_kernel, n_chunks=n_chunks, batch=batch)
    return pl.pallas_call(
        kern,
        grid=(g,),
        in_specs=[
            pl.BlockSpec((1, rows, width), lambda i: (i, 0, 0)),
            pl.BlockSpec((1, width, width), lambda i: (i, 0, 0)),
            pl.BlockSpec((1, width, 2 * SSM_STATE), lambda i: (i, 0, 0)),
            pl.BlockSpec((1, 2 * SSM_STATE, width), lambda i: (i, 0, 0)),
            pl.BlockSpec((1, 8, 2 * SSM_STATE), lambda i: (i, 0, 0)),
            pl.BlockSpec((1, 1, width), lambda i: (i, 0, 0)),
        ],
        out_specs=pl.BlockSpec((1, rows, width), lambda i: (i, 0, 0)),
        out_shape=jax.ShapeDtypeStruct((g, rows, width), F32),
        scratch_shapes=[pltpu.VMEM((rows, 2 * SSM_STATE), F32)],
        compiler_params=pltpu.CompilerParams(
            dimension_semantics=("parallel",), vmem_limit_bytes=VMEM_LIMIT),
        name="ssm",
    )(ug, m, w, r, a, d)


def _ssm_tables(lam_re, lam_im, log_dt, b_re, b_im, c_re, c_im, d_skip):
    hp = lax.Precision.HIGHEST
    lc = SSM_CHUNK
    g, p = lam_re.shape
    hc = SSM_GROUP_CH
    dt = jnp.exp(log_dt)[:, None]
    a_re = jnp.exp(lam_re * dt) * jnp.cos(lam_im * dt)
    a_im = jnp.exp(lam_re * dt) * jnp.sin(lam_im * dt)
    den = lam_re * lam_re + lam_im * lam_im
    nr = a_re - 1.0
    z_re = (nr * lam_re + a_im * lam_im) / den
    z_im = (a_im * lam_re - nr * lam_im) / den
    bb_re = z_re[..., None] * b_re - z_im[..., None] * b_im
    bb_im = z_re[..., None] * b_im + z_im[..., None] * b_re
    tau = jnp.arange(lc + 1, dtype=F32)[:, None, None]
    mag = jnp.exp(tau * (lam_re * dt)[None])
    ang = tau * (lam_im * dt)[None]
    p_re = mag * jnp.cos(ang)
    p_im = mag * jnp.sin(ang)
    ab_re = p_re[:lc, :, :, None] * bb_re[None] - p_im[:lc, :, :, None] * bb_im[None]
    ab_im = p_re[:lc, :, :, None] * bb_im[None] + p_im[:lc, :, :, None] * bb_re[None]
    kk = (jnp.einsum('ghp,tgpk->tghk', c_re, ab_re, precision=hp)
          - jnp.einsum('ghp,tgpk->tghk', c_im, ab_im, precision=hp))
    t_idx = jnp.arange(lc)
    lag = t_idx[None, :] - t_idx[:, None]
    toe = jnp.where((lag >= 0)[:, :, None, None, None], kk[jnp.clip(lag, 0, lc - 1)], 0.0)
    m = toe.transpose(2, 0, 4, 1, 3).reshape(g, lc * hc, lc * hc)
    w_re = ab_re[::-1].transpose(1, 0, 3, 2).reshape(g, lc * hc, p)
    w_im = ab_im[::-1].transpose(1, 0, 3, 2).reshape(g, lc * hc, p)
    w = jnp.concatenate([w_re, w_im], axis=-1)
    q_re = p_re[1:, :, None, :] * c_re[None] - p_im[1:, :, None, :] * c_im[None]
    q_im = p_re[1:, :, None, :] * c_im[None] + p_im[1:, :, None, :] * c_re[None]
    r = jnp.concatenate([q_re.transpose(1, 3, 0, 2).reshape(g, p, lc * hc),
                         -q_im.transpose(1, 3, 0, 2).reshape(g, p, lc * hc)], axis=1)
    al_re, al_im = p_re[lc], p_im[lc]
    a = jnp.zeros((g, 8, 2 * p), F32)
    a = a.at[:, 0, :].set(jnp.concatenate([al_re, al_re], axis=-1))
    a = a.at[:, 1, :].set(jnp.concatenate([-al_im, al_im], axis=-1))
    d = jnp.tile(d_skip[:, None, :], (1, lc, 1)).reshape(g, 1, lc * hc)
    return m.astype(BF16), w.astype(BF16), r.astype(BF16), a, d


def _attn_kernel(q_ref, k_ref, v_ref, o_ref, m_ref, l_ref, acc_ref, *, blk):
    i = pl.program_id(1)
    q = q_ref[0]
    m_ref[...] = jnp.full_like(m_ref, -jnp.inf)
    l_ref[...] = jnp.zeros_like(l_ref)
    acc_ref[...] = jnp.zeros_like(acc_ref)

    def step(j, masked):
        off = pl.multiple_of(j * blk, blk)
        k = k_ref[0, pl.ds(off, blk), :]
        v = v_ref[0, pl.ds(off, blk), :]
        s = lax.dot_general(q, k, (((1,), (1,)), ((), ())), preferred_element_type=F32)
        if masked:
            row = lax.broadcasted_iota(I32, s.shape, 0)
            col = lax.broadcasted_iota(I32, s.shape, 1)
            s = jnp.where(row >= col, s, -jnp.inf)
        m_prev = m_ref[...]
        m_new = jnp.maximum(m_prev, jnp.max(s, axis=1, keepdims=True))
        p = jnp.exp(s - m_new)
        alpha = jnp.exp(m_prev - m_new)
        l_ref[...] = alpha * l_ref[...] + jnp.sum(p, axis=1, keepdims=True)
        acc_ref[...] = alpha * acc_ref[...] + jnp.dot(p.astype(BF16), v, preferred_element_type=F32)
        m_ref[...] = m_new

    def body(j, c):
        step(j, False)
        return c

    lax.fori_loop(0, i, body, 0)
    step(i, True)
    o_ref[0] = acc_ref[...] / l_ref[...]


def _attention(qa, ka, v, blk):
    bh, s, _ = qa.shape
    kern = functools.partial(_attn_kernel, blk=blk)
    return pl.pallas_call(
        kern,
        grid=(bh, s // blk),
        in_specs=[
            pl.BlockSpec((1, blk, AUG), lambda b, i: (b, i, 0)),
            pl.BlockSpec((1, s, AUG), lambda b, i: (b, 0, 0)),
            pl.BlockSpec((1, s, HEAD_DIM), lambda b, i: (b, 0, 0)),
        ],
        out_specs=pl.BlockSpec((1, blk, HEAD_DIM), lambda b, i: (b, i, 0)),
        out_shape=jax.ShapeDtypeStruct((bh, s, HEAD_DIM), F32),
        scratch_shapes=[pltpu.VMEM((blk, 1), F32), pltpu.VMEM((blk, 1), F32), pltpu.VMEM((blk, HEAD_DIM), F32)],
        compiler_params=pltpu.CompilerParams(
            dimension_semantics=("parallel", "arbitrary"), vmem_limit_bytes=VMEM_LIMIT),
        name="fox_attn",
    )(qa, ka, v)


def _out_proj_kernel(y_ref, att_ref, h_ref, wg_ref, bg_ref, gs_ref, ga_ref, wos_ref, woa_ref, o_ref):
    g = _gelu(y_ref[0])
    z = jnp.dot(g.astype(BF16), wg_ref[...], preferred_element_type=F32) + bg_ref[...]
    o = g * _sigmoid(z)
    a = _rms(o, gs_ref[...])
    acc = h_ref[0] + jnp.dot(a.astype(BF16), wos_ref[...], preferred_element_type=F32)
    ssq = jnp.zeros((o.shape[0], 1), F32)
    for hh in range(N_HEADS):
        t = att_ref[0, hh]
        ssq = ssq + jnp.sum(t * t, axis=1, keepdims=True)
    inv = lax.rsqrt(ssq / (N_HEADS * HEAD_DIM) + RMS_EPS)
    for hh in range(N_HEADS):
        bh = att_ref[0, hh] * inv * ga_ref[hh]
        acc = acc + jnp.dot(bh.astype(BF16), woa_ref[hh], preferred_element_type=F32)
    o_ref[0] = acc


def _out_proj(y_ssm, y_att, h, wg, bg, gs, ga, wos, woa, tm):
    b, s, d = h.shape
    w = y_ssm.shape[-1]
    return pl.pallas_call(
        _out_proj_kernel,
        grid=(b, s // tm),
        in_specs=[
            pl.BlockSpec((1, tm, w), lambda i, j: (i, j, 0)),
            pl.BlockSpec((1, N_HEADS, tm, HEAD_DIM), lambda i, j: (i, 0, j, 0)),
            pl.BlockSpec((1, tm, d), lambda i, j: (i, j, 0)),
            pl.BlockSpec((w, w), lambda i, j: (0, 0)),
            pl.BlockSpec((1, w), lambda i, j: (0, 0)),
            pl.BlockSpec((1, w), lambda i, j: (0, 0)),
            pl.BlockSpec((N_HEADS, 1, HEAD_DIM), lambda i, j: (0, 0, 0)),
            pl.BlockSpec((w, d), lambda i, j: (0, 0)),
            pl.BlockSpec((N_HEADS, HEAD_DIM, d), lambda i, j: (0, 0, 0)),
        ],
        out_specs=pl.BlockSpec((1, tm, d), lambda i, j: (i, j, 0)),
        out_shape=jax.ShapeDtypeStruct((b, s, d), F32),
        compiler_params=pltpu.CompilerParams(
            dimension_semantics=("parallel", "parallel"), vmem_limit_bytes=VMEM_LIMIT),
        name="out_proj",
    )(y_ssm, y_att, h, wg, bg, gs, ga, wos, woa)


def _take_top(vals, payload, k):
    n_rows = vals.shape[0]
    rows = lax.broadcasted_iota(I32, vals.shape, 0)
    tops, picks = [], []
    for _ in range(k):
        m = jnp.max(vals, axis=0, keepdims=True)
        arg = jnp.min(jnp.where(vals == m, rows, n_rows), axis=0, keepdims=True)
        hit = rows == arg
        tops.append(m)
        picks.append(arg if payload is None else jnp.max(jnp.where(hit, payload, -1), axis=0, keepdims=True))
        vals = jnp.where(hit, -jnp.inf, vals)
    return jnp.concatenate(tops, axis=0), jnp.concatenate(picks, axis=0)


def _route_kernel(h_ref, g_ref, wq_ref, keys_ref, hn_ref, idx_ref, gate_ref):
    x = h_ref[...]
    hn = _rms(x, g_ref[...])
    hn_ref[...] = hn
    q = jnp.dot(hn.astype(BF16), wq_ref[...], preferred_element_type=F32)
    k = PEER_TOPK
    idx_rows, gate_rows = [], []
    for hh in range(PEER_HEADS):
        qh = q[:, 128 * hh: 128 * (hh + 1)].astype(BF16)
        sc = lax.dot_general(keys_ref[hh], qh, (((1,), (1,)), ((), ())), preferred_element_type=F32)
        t1, i1 = _take_top(sc[:PEER_KEYS], None, k)
        t2, i2 = _take_top(sc[PEER_KEYS:], None, k)
        cand = jnp.concatenate([t1[a:a + 1] + t2 for a in range(k)], axis=0)
        cidx = jnp.concatenate([i1[a:a + 1] * PEER_KEYS + i2 for a in range(k)], axis=0)
        best, idx = _take_top(cand, cidx, k)
        e = jnp.exp(best - best[0:1])
        gate = e / jnp.sum(e, axis=0, keepdims=True)
        idx_rows.append(idx)
        gate_rows.append(gate)
    idx_ref[...] = jnp.concatenate(idx_rows, axis=0).T
    gate_ref[...] = jnp.concatenate(gate_rows, axis=0).T


def _route(h2, g, wq, keys_cat, tm):
    t, d = h2.shape
    nk = PEER_HEADS * PEER_TOPK
    return pl.pallas_call(
        _route_kernel,
        grid=(t // tm,),
        in_specs=[
            pl.BlockSpec((tm, d), lambda i: (i, 0)),
            pl.BlockSpec((1, d), lambda i: (0, 0)),
            pl.BlockSpec((d, PEER_HEADS * 128), lambda i: (0, 0)),
            pl.BlockSpec((PEER_HEADS, 2 * PEER_KEYS, 128), lambda i: (0, 0, 0)),
        ],
        out_specs=[
            pl.BlockSpec((tm, d), lambda i: (i, 0)),
            pl.BlockSpec((tm, nk), lambda i: (i, 0)),
            pl.BlockSpec((tm, nk), lambda i: (i, 0)),
        ],
        out_shape=[
            jax.ShapeDtypeStruct((t, d), F32),
            jax.ShapeDtypeStruct((t, nk), I32),
            jax.ShapeDtypeStruct((t, nk), F32),
        ],
        compiler_params=pltpu.CompilerParams(
            dimension_semantics=("parallel",), vmem_limit_bytes=VMEM_LIMIT),
        name="peer_route",
    )(h2, g, wq, keys_cat)


def _coef_kernel(a_ref, g_ref, o_ref):
    o_ref[...] = _gelu(a_ref[...]) * g_ref[...]


def _coef(act, gate, tm):
    t, n = act.shape
    return pl.pallas_call(
        _coef_kernel,
        grid=(t // tm,),
        in_specs=[pl.BlockSpec((tm, n), lambda i: (i, 0)), pl.BlockSpec((tm, n), lambda i: (i, 0))],
        out_specs=pl.BlockSpec((tm, n), lambda i: (i, 0)),
        out_shape=jax.ShapeDtypeStruct((t, n), F32),
        compiler_params=pltpu.CompilerParams(dimension_semantics=("parallel",)),
        name="peer_coef",
    )(act, gate)


def _final_norm_kernel(x_ref, g_ref, o_ref):
    o_ref[...] = _rms(x_ref[...], g_ref[...])


def _final_norm(x2, g, tm):
    t, d = x2.shape
    return pl.pallas_call(
        _final_norm_kernel,
        grid=(t // tm,),
        in_specs=[pl.BlockSpec((tm, d), lambda i: (i, 0)), pl.BlockSpec((1, d), lambda i: (0, 0))],
        out_specs=pl.BlockSpec((tm, d), lambda i: (i, 0)),
        out_shape=jax.ShapeDtypeStruct((t, d), F32),
        compiler_params=pltpu.CompilerParams(dimension_semantics=("parallel",)),
        name="final_norm",
    )(x2, g)


SC_WORKERS = 32
SC_CORES = 2
SC_LANES = 16
SC_TOK = 8
SC_ROWS = 16
SC_BUFS = 4
SC_COLS = 256


def _sc_params():
    cp = pltpu.CompilerParams()
    if "needs_layout_passes" in pltpu.CompilerParams.__dataclass_fields__:
        cp = pltpu.CompilerParams(needs_layout_passes=False)
    return cp


def _sc_worker_id():
    return lax.axis_index("s") * SC_CORES + lax.axis_index("c")


def _sc_pipeline(tab_hbm, idx_v, rows, sems, n_chunks, compute):
    def gather(k, slot):
        return pltpu.make_async_copy(tab_hbm.at[idx_v.at[k]], rows[slot], sems[slot])

    for s in range(SC_BUFS - 1):
        gather(s, s).start()

    def group(i, carry):
        for s in range(SC_BUFS):
            k = i * SC_BUFS + s
            ahead = k + SC_BUFS - 1

            @pl.when(ahead < n_chunks)
            def _():
                gather(ahead, (s + SC_BUFS - 1) % SC_BUFS).start()

            gather(k, s).wait()
            compute(rows[s], k)
        return carry

    lax.fori_loop(0, n_chunks // SC_BUFS, group, 0)


def _tree_sum(xs):
    xs = list(xs)
    while len(xs) > 1:
        xs = [xs[i] + xs[i + 1] for i in range(0, len(xs) - 1, 2)] + ([xs[-1]] if len(xs) % 2 else [])
    return xs[0]


def _peer_scores_body(hn_hbm, idx_hbm, tab_hbm, out_hbm, h_v, idx_v, acc_v, act_v, *bufs, tok_per_worker):
    rows, sems = bufs[:SC_BUFS], bufs[SC_BUFS:]
    d = hn_hbm.shape[1]
    per_tok = PEER_HEADS * PEER_TOPK // SC_ROWS
    n_chunks = SC_TOK * per_tok
    n_reg = SC_COLS // SC_LANES
    lane = lax.iota(I32, SC_LANES)
    wid = _sc_worker_id()

    def compute(rw, k):
        t = k // per_tok
        c = k % per_tok
        for cc in range(d // SC_COLS):
            hv = [h_v[t, pl.ds(cc * SC_COLS + SC_LANES * j, SC_LANES)] for j in range(n_reg)]

            @plsc.parallel_loop(0, SC_ROWS, 1)
            def _(r):
                p = _tree_sum([rw[r, pl.ds(cc * SC_COLS + SC_LANES * j, SC_LANES)] * hv[j] for j in range(n_reg)])
                off = pl.multiple_of(r * SC_LANES, SC_LANES)
                if cc == 0:
                    acc_v[pl.ds(off, SC_LANES)] = p
                else:
                    acc_v[pl.ds(off, SC_LANES)] = acc_v[pl.ds(off, SC_LANES)] + p

        tot = _tree_sum([plsc.load_gather(acc_v, [lane * SC_LANES + j]) for j in range(SC_LANES)])
        off = pl.multiple_of(c * SC_ROWS, SC_LANES)
        act_v[t, pl.ds(off, SC_LANES)] = tot

    def batch(bi, carry):
        t0 = pl.multiple_of(wid * tok_per_worker + bi * SC_TOK, SC_TOK)
        pltpu.sync_copy(hn_hbm.at[pl.ds(t0, SC_TOK)], h_v)
        pltpu.sync_copy(idx_hbm.at[pl.ds(t0 * per_tok, n_chunks)], idx_v)
        _sc_pipeline(tab_hbm, idx_v, rows, sems, n_chunks, compute)
        pltpu.sync_copy(act_v, out_hbm.at[pl.ds(t0, SC_TOK)])
        return carry

    lax.fori_loop(0, tok_per_worker // SC_TOK, batch, 0)


def _peer_scores(hn, idx4, u_tab):
    t, d = hn.shape
    nk = PEER_HEADS * PEER_TOPK
    mesh = plsc.VectorSubcoreMesh(core_axis_name="c", subcore_axis_name="s")
    body = functools.partial(_peer_scores_body, tok_per_worker=t // SC_WORKERS)
    return pl.kernel(
        body,
        out_type=jax.ShapeDtypeStruct((t, nk), F32),
        mesh=mesh,
        scratch_types=[
            pltpu.VMEM((SC_TOK, d), F32),
            pltpu.VMEM((SC_TOK * nk // SC_ROWS, SC_ROWS), I32),
            pltpu.VMEM((SC_ROWS * SC_LANES,), F32),
            pltpu.VMEM((SC_TOK, nk), F32),
        ] + [pltpu.VMEM((SC_ROWS, d), F32)] * SC_BUFS + [pltpu.SemaphoreType.DMA] * SC_BUFS,
        compiler_params=_sc_params(),
        name="peer_scores_sc",
    )(hn, idx4, u_tab)


def _peer_combine_body(h_hbm, idx_hbm, coef_hbm, tab_hbm, out_hbm, o_v, idx_v, coef_v, *bufs, tok_per_worker):
    rows, sems = bufs[:SC_BUFS], bufs[SC_BUFS:]
    d = h_hbm.shape[1]
    nk = PEER_HEADS * PEER_TOPK
    per_tok = nk // SC_ROWS
    n_chunks = SC_TOK * per_tok
    n_reg = SC_COLS // SC_LANES
    wid = _sc_worker_id()

    def compute(rw, k):
        t = k // per_tok
        for cc in range(d // SC_COLS):
            acc0 = tuple(o_v[t, pl.ds(cc * SC_COLS + SC_LANES * j, SC_LANES)] for j in range(n_reg))

            def rbody(r, acc):
                w = plsc.load_gather(coef_v, [jnp.full((SC_LANES,), k * SC_ROWS + r, I32)])
                return tuple(acc[j] + w * rw[r, pl.ds(cc * SC_COLS + SC_LANES * j, SC_LANES)]
                             for j in range(n_reg))

            acc = lax.fori_loop(0, SC_ROWS, rbody, acc0)
            for j in range(n_reg):
                o_v[t, pl.ds(cc * SC_COLS + SC_LANES * j, SC_LANES)] = acc[j]

    def batch(bi, carry):
        t0 = pl.multiple_of(wid * tok_per_worker + bi * SC_TOK, SC_TOK)
        pltpu.sync_copy(h_hbm.at[pl.ds(t0, SC_TOK)], o_v)
        pltpu.sync_copy(idx_hbm.at[pl.ds(t0 * per_tok, n_chunks)], idx_v)
        pltpu.sync_copy(coef_hbm.at[pl.ds(t0 * nk, SC_TOK * nk)], coef_v)
        _sc_pipeline(tab_hbm, idx_v, rows, sems, n_chunks, compute)
        pltpu.sync_copy(o_v, out_hbm.at[pl.ds(t0, SC_TOK)])
        return carry

    lax.fori_loop(0, tok_per_worker // SC_TOK, batch, 0)


def _peer_combine(h2, idx4, coef_flat, v_tab):
    t, d = h2.shape
    nk = PEER_HEADS * PEER_TOPK
    mesh = plsc.VectorSubcoreMesh(core_axis_name="c", subcore_axis_name="s")
    body = functools.partial(_peer_combine_body, tok_per_worker=t // SC_WORKERS)
    return pl.kernel(
        body,
        out_type=jax.ShapeDtypeStruct((t, d), F32),
        mesh=mesh,
        scratch_types=[
            pltpu.VMEM((SC_TOK, d), F32),
            pltpu.VMEM((SC_TOK * nk // SC_ROWS, SC_ROWS), I32),
            pltpu.VMEM((SC_TOK * nk,), F32),
        ] + [pltpu.VMEM((SC_ROWS, d), F32)] * SC_BUFS + [pltpu.SemaphoreType.DMA] * SC_BUFS,
        compiler_params=_sc_params(),
        name="peer_combine_sc",
    )(h2, idx4, coef_flat, v_tab)


def _mixers(h, l, norm1_g, w_in, ssm, fox_b_f, g_ssm_out, g_attn_out, w_o, ssm_w_glu, ssm_b_glu, tm, blk):
    b, s, d = h.shape
    w = N_HEADS * HEAD_DIM
    wl = w_in[l]
    wm = wl[:, :4 * w].astype(BF16)
    wf = jnp.pad(wl[:, 4 * w:], ((0, 0), (0, 128 - N_HEADS)))
    bf = jnp.pad(fox_b_f[l], (0, 128 - N_HEADS)).reshape(1, 128)
    u, qa, ka, v = _in_proj(h, norm1_g[l].reshape(1, d), wm, wf, bf, tm)

    lc = SSM_CHUNK
    nc = s // lc
    g = w // SSM_GROUP_CH
    ug = u.reshape(b, nc, lc, g, SSM_GROUP_CH).transpose(3, 1, 0, 2, 4).reshape(g, nc * b, lc * SSM_GROUP_CH)
    yg = _ssm(ug, *ssm, n_chunks=nc, batch=b)
    y_ssm = yg.reshape(g, nc, b, lc, SSM_GROUP_CH).transpose(2, 1, 3, 0, 4).reshape(b, s, w)

    y_att = _attention(qa.reshape(b * N_HEADS, s, AUG), ka.reshape(b * N_HEADS, s, AUG),
                       v.reshape(b * N_HEADS, s, HEAD_DIM), blk).reshape(b, N_HEADS, s, HEAD_DIM)

    wo = w_o[l].astype(BF16)
    return _out_proj(y_ssm, y_att, h, ssm_w_glu[l].astype(BF16), ssm_b_glu[l].reshape(1, w),
                     g_ssm_out[l].reshape(1, w), g_attn_out[l].reshape(N_HEADS, 1, HEAD_DIM),
                     wo[:w], wo[w:].reshape(N_HEADS, HEAD_DIM, d), tm)


def _keys_cat(keys_l):
    z = jnp.zeros_like(keys_l[:, 0])
    top = jnp.concatenate([keys_l[:, 0], z], axis=-1)
    bot = jnp.concatenate([z, keys_l[:, 1]], axis=-1)
    return jnp.concatenate([top, bot], axis=1).astype(BF16)


def kernel(x, norm1_g, w_in, ssm_lambda_re, ssm_lambda_im, ssm_log_dt, ssm_b_re, ssm_b_im, ssm_c_re, ssm_c_im, ssm_d, ssm_w_glu, ssm_b_glu, fox_b_f, g_ssm_out, g_attn_out, w_o, norm2_g, peer_w_q, peer_keys, peer_u, peer_v, norm_f):
    b, s, d = x.shape
    depth = w_in.shape[0]
    t = b * s
    nk = PEER_HEADS * PEER_TOPK
    tm = min(512, s)
    blk = min(256, s)
    h = x
    for l in range(depth):
        ssm = _ssm_tables(ssm_lambda_re[l], ssm_lambda_im[l], ssm_log_dt[l], ssm_b_re[l], ssm_b_im[l],
                          ssm_c_re[l], ssm_c_im[l], ssm_d[l])
        h = _mixers(h, l, norm1_g, w_in, ssm, fox_b_f, g_ssm_out, g_attn_out, w_o, ssm_w_glu, ssm_b_glu, tm, blk)
        h2 = h.reshape(t, d)
        hn, idx, gate = _route(h2, norm2_g[l].reshape(1, d), peer_w_q[l].astype(BF16), _keys_cat(peer_keys[l]),
                               min(256, t))
        idx4 = idx.reshape(t * nk // SC_ROWS, SC_ROWS)
        act = _peer_scores(hn, idx4, peer_u[l])
        coef = _coef(act, gate, min(1024, t))
        h = _peer_combine(h2, idx4, coef.reshape(t * nk), peer_v[l]).reshape(b, s, d)
    return _final_norm(h.reshape(t, d), norm_f.reshape(1, d), min(512, t)).reshape(b, s, d)
```

```python
import functools
import math

import jax
import jax.numpy as jnp
from jax import lax
from jax.experimental import pallas as pl
from jax.experimental.pallas import tpu as pltpu
from jax.experimental.pallas import tpu_sc as plsc

F32 = jnp.float32
BF16 = jnp.bfloat16
I32 = jnp.int32

RMS_EPS = 1e-6
SSM_GROUP_CH = 16
SSM_STATE = 64
SSM_CHUNK = 64
HEAD_DIM = 64
N_HEADS = 8
AUG = 128
PEER_HEADS = 8
PEER_KEYS = 128
PEER_TOPK = 16
PEER_HALF = 64
VMEM_LIMIT = 56 * 1024 * 1024


def _rms(x, g):
    return x * lax.rsqrt(jnp.mean(x * x, axis=-1, keepdims=True) + RMS_EPS) * g


def _gelu(x):
    c = math.sqrt(2.0 / math.pi)
    return 0.5 * x * (1.0 + jnp.tanh(c * (x + 0.044715 * (x * x * x))))


def _sigmoid(x):
    return 1.0 / (1.0 + jnp.exp(-x))


def _in_proj_kernel(h_ref, g_ref, wm_ref, wf_ref, bf_ref, u_ref, qa_ref, ka_ref, v_ref, cum_ref):
    j = pl.program_id(1)

    @pl.when(j == 0)
    def _():
        cum_ref[...] = jnp.zeros_like(cum_ref)

    x = h_ref[0]
    tm = x.shape[0]
    xn = _rms(x, g_ref[...])
    proj = jnp.dot(xn.astype(BF16), wm_ref[...], preferred_element_type=F32)
    f = jnp.dot(xn, wf_ref[...], precision=lax.Precision.HIGHEST, preferred_element_type=F32) + bf_ref[...]
    logf = jnp.minimum(f, 0.0) - jnp.log(1.0 + jnp.exp(-jnp.abs(f)))
    row = lax.broadcasted_iota(I32, (tm, tm), 0)
    col = lax.broadcasted_iota(I32, (tm, tm), 1)
    tri = (row >= col).astype(F32)
    cum = jnp.dot(tri, logf, precision=lax.Precision.HIGHEST, preferred_element_type=F32) + cum_ref[0:1, :]
    cum_ref[0:1, :] = cum[tm - 1:tm, :]

    w = HEAD_DIM * N_HEADS
    u_ref[0] = proj[:, :w].astype(BF16)
    lane = lax.broadcasted_iota(I32, (tm, AUG), 1)
    scale = HEAD_DIM ** -0.5
    for hh in range(N_HEADS):
        pair = hh // 2
        q2 = proj[:, w + 128 * pair: w + 128 * pair + 128]
        k2 = proj[:, 2 * w + 128 * pair: 2 * w + 128 * pair + 128]
        v2 = proj[:, 3 * w + 128 * pair: 3 * w + 128 * pair + 128]
        if hh % 2 == 1:
            q2 = pltpu.roll(q2, 64, axis=1)
            k2 = pltpu.roll(k2, 64, axis=1)
            vh = v2[:, 64:]
        else:
            vh = v2[:, :64]
        c = jnp.broadcast_to(cum[:, hh:hh + 1], (tm, AUG))
        c1 = c.astype(BF16).astype(F32)
        r1 = c - c1
        c2 = r1.astype(BF16).astype(F32)
        c3 = r1 - c2
        one = jnp.ones((tm, AUG), F32)
        zero = jnp.zeros((tm, AUG), F32)
        qa = jnp.where(lane < 64, q2 * scale,
             jnp.where(lane == 64, c1, jnp.where(lane == 65, c2, jnp.where(lane == 66, c3,
             jnp.where(lane < 70, one, zero)))))
        ka = jnp.where(lane < 64, k2,
             jnp.where(lane < 67, one, jnp.where(lane == 67, -c1, jnp.where(lane == 68, -c2,
             jnp.where(lane == 69, -c3, zero)))))
        qa_ref[0, hh] = qa.astype(BF16)
        ka_ref[0, hh] = ka.astype(BF16)
        v_ref[0, hh] = vh.astype(BF16)


def _in_proj(h, g, wm, wf, bf, tm):
    b, s, d = h.shape
    w = HEAD_DIM * N_HEADS
    return pl.pallas_call(
        _in_proj_kernel,
        grid=(b, s // tm),
        in_specs=[
            pl.BlockSpec((1, tm, d), lambda i, j: (i, j, 0)),
            pl.BlockSpec((1, d), lambda i, j: (0, 0)),
            pl.BlockSpec((d, 4 * w), lambda i, j: (0, 0)),
            pl.BlockSpec((d, 128), lambda i, j: (0, 0)),
            pl.BlockSpec((1, 128), lambda i, j: (0, 0)),
        ],
        out_specs=[
            pl.BlockSpec((1, tm, w), lambda i, j: (i, j, 0)),
            pl.BlockSpec((1, N_HEADS, tm, AUG), lambda i, j: (i, 0, j, 0)),
            pl.BlockSpec((1, N_HEADS, tm, AUG), lambda i, j: (i, 0, j, 0)),
            pl.BlockSpec((1, N_HEADS, tm, HEAD_DIM), lambda i, j: (i, 0, j, 0)),
        ],
        out_shape=[
            jax.ShapeDtypeStruct((b, s, w), BF16),
            jax.ShapeDtypeStruct((b, N_HEADS, s, AUG), BF16),
            jax.ShapeDtypeStruct((b, N_HEADS, s, AUG), BF16),
            jax.ShapeDtypeStruct((b, N_HEADS, s, HEAD_DIM), BF16),
        ],
        scratch_shapes=[pltpu.VMEM((8, 128), F32)],
        compiler_params=pltpu.CompilerParams(
            dimension_semantics=("parallel", "arbitrary"), vmem_limit_bytes=VMEM_LIMIT),
        name="in_proj",
    )(h, g, wm, wf, bf)


def _ssm_kernel(u_ref, m_ref, w_ref, r_ref, a_ref, d_ref, y_ref, e_scr, *, n_chunks, batch):
    u = u_ref[0]
    y = jnp.dot(u, m_ref[0], preferred_element_type=F32)
    e_scr[...] = jnp.dot(u, w_ref[0], preferred_element_type=F32)
    ar = a_ref[0, 0:1, :]
    ai = a_ref[0, 1:2, :]

    def body(c, s):
        off = pl.multiple_of(c * batch, batch)
        e_c = e_scr[pl.ds(off, batch), :]
        e_scr[pl.ds(off, batch), :] = s
        return ar * s + ai * pltpu.roll(s, SSM_STATE, axis=1) + e_c

    lax.fori_loop(0, n_chunks, body, jnp.zeros((batch, 2 * SSM_STATE), F32))
    y = y + jnp.dot(e_scr[...].astype(BF16), r_ref[0], preferred_element_type=F32)
    y_ref[0] = y + u.astype(F32) * d_ref[0]


def _ssm(ug, m, w, r, a, d, n_chunks, batch):
    g, rows, width = ug.shape
    kern = functools.partial(_ssm_kernel, n_chunks=n_chunks, batch=batch)
    return pl.pallas_call(
        kern,
        grid=(g,),
        in_specs=[
            pl.BlockSpec((1, rows, width), lambda i: (i, 0, 0)),
            pl.BlockSpec((1, width, width), lambda i: (i, 0, 0)),
            pl.BlockSpec((1, width, 2 * SSM_STATE), lambda i: (i, 0, 0)),
            pl.BlockSpec((1, 2 * SSM_STATE, width), lambda i: (i, 0, 0)),
            pl.BlockSpec((1, 8, 2 * SSM_STATE), lambda i: (i, 0, 0)),
            pl.BlockSpec((1, 1, width), lambda i: (i, 0, 0)),
        ],
        out_specs=pl.BlockSpec((1, rows, width), lambda i: (i, 0, 0)),
        out_shape=jax.ShapeDtypeStruct((g, rows, width), F32),
        scratch_shapes=[pltpu.VMEM((rows, 2 * SSM_STATE), F32)],
        compiler_params=pltpu.CompilerParams(
            dimension_semantics=("parallel",), vmem_limit_bytes=VMEM_LIMIT),
        name="ssm",
    )(ug, m, w, r, a, d)


def _ssm_tables(lam_re, lam_im, log_dt, b_re, b_im, c_re, c_im, d_skip):
    hp = lax.Precision.HIGHEST
    lc = SSM_CHUNK
    g, p = lam_re.shape
    hc = SSM_GROUP_CH
    dt = jnp.exp(log_dt)[:, None]
    a_re = jnp.exp(lam_re * dt) * jnp.cos(lam_im * dt)
    a_im = jnp.exp(lam_re * dt) * jnp.sin(lam_im * dt)
    den = lam_re * lam_re + lam_im * lam_im
    nr = a_re - 1.0
    z_re = (nr * lam_re + a_im * lam_im) / den
    z_im = (a_im * lam_re - nr * lam_im) / den
    bb_re = z_re[..., None] * b_re - z_im[..., None] * b_im
    bb_im = z_re[..., None] * b_im + z_im[..., None] * b_re
    tau = jnp.arange(lc + 1, dtype=F32)[:, None, None]
    mag = jnp.exp(tau * (lam_re * dt)[None])
    ang = tau * (lam_im * dt)[None]
    p_re = mag * jnp.cos(ang)
    p_im = mag * jnp.sin(ang)
    ab_re = p_re[:lc, :, :, None] * bb_re[None] - p_im[:lc, :, :, None] * bb_im[None]
    ab_im = p_re[:lc, :, :, None] * bb_im[None] + p_im[:lc, :, :, None] * bb_re[None]
    kk = (jnp.einsum('ghp,tgpk->tghk', c_re, ab_re, precision=hp)
          - jnp.einsum('ghp,tgpk->tghk', c_im, ab_im, precision=hp))
    t_idx = jnp.arange(lc)
    lag = t_idx[None, :] - t_idx[:, None]
    toe = jnp.where((lag >= 0)[:, :, None, None, None], kk[jnp.clip(lag, 0, lc - 1)], 0.0)
    m = toe.transpose(2, 0, 4, 1, 3).reshape(g, lc * hc, lc * hc)
    w_re = ab_re[::-1].transpose(1, 0, 3, 2).reshape(g, lc * hc, p)
    w_im = ab_im[::-1].transpose(1, 0, 3, 2).reshape(g, lc * hc, p)
    w = jnp.concatenate([w_re, w_im], axis=-1)
    q_re = p_re[1:, :, None, :] * c_re[None] - p_im[1:, :, None, :] * c_im[None]
    q_im = p_re[1:, :, None, :] * c_im[None] + p_im[1:, :, None, :] * c_re[None]
    r = jnp.concatenate([q_re.transpose(1, 3, 0, 2).reshape(g, p, lc * hc),
                         -q_im.transpose(1, 3, 0, 2).reshape(g, p, lc * hc)], axis=1)
    al_re, al_im = p_re[lc], p_im[lc]
    a = jnp.zeros((g, 8, 2 * p), F32)
    a = a.at[:, 0, :].set(jnp.concatenate([al_re, al_re], axis=-1))
    a = a.at[:, 1, :].set(jnp.concatenate([-al_im, al_im], axis=-1))
    d = jnp.tile(d_skip[:, None, :], (1, lc, 1)).reshape(g, 1, lc * hc)
    return m.astype(BF16), w.astype(BF16), r.astype(BF16), a, d


def _attn_kernel(q_ref, k_ref, v_ref, o_ref, m_ref, l_ref, acc_ref, *, blk):
    i = pl.program_id(1)
    q = q_ref[0]
    m_ref[...] = jnp.full_like(m_ref, -jnp.inf)
    l_ref[...] = jnp.zeros_like(l_ref)
    acc_ref[...] = jnp.zeros_like(acc_ref)

    def step(j, masked):
        off = pl.multiple_of(j * blk, blk)
        k = k_ref[0, pl.ds(off, blk), :]
        v = v_ref[0, pl.ds(off, blk), :]
        s = lax.dot_general(q, k, (((1,), (1,)), ((), ())), preferred_element_type=F32)
        if masked:
            row = lax.broadcasted_iota(I32, s.shape, 0)
            col = lax.broadcasted_iota(I32, s.shape, 1)
            s = jnp.where(row >= col, s, -jnp.inf)
        m_prev = m_ref[...]
        m_new = jnp.maximum(m_prev, jnp.max(s, axis=1, keepdims=True))
        p = jnp.exp(s - m_new)
        alpha = jnp.exp(m_prev - m_new)
        l_ref[...] = alpha * l_ref[...] + jnp.sum(p, axis=1, keepdims=True)
        acc_ref[...] = alpha * acc_ref[...] + jnp.dot(p.astype(BF16), v, preferred_element_type=F32)
        m_ref[...] = m_new

    def body(j, c):
        step(j, False)
        return c

    lax.fori_loop(0, i, body, 0)
    step(i, True)
    o_ref[0] = acc_ref[...] / l_ref[...]


def _attention(qa, ka, v, blk):
    bh, s, _ = qa.shape
    kern = functools.partial(_attn_kernel, blk=blk)
    return pl.pallas_call(
        kern,
        grid=(bh, s // blk),
        in_specs=[
            pl.BlockSpec((1, blk, AUG), lambda b, i: (b, i, 0)),
            pl.BlockSpec((1, s, AUG), lambda b, i: (b, 0, 0)),
            pl.BlockSpec((1, s, HEAD_DIM), lambda b, i: (b, 0, 0)),
        ],
        out_specs=pl.BlockSpec((1, blk, HEAD_DIM), lambda b, i: (b, i, 0)),
        out_shape=jax.ShapeDtypeStruct((bh, s, HEAD_DIM), F32),
        scratch_shapes=[pltpu.VMEM((blk, 1), F32), pltpu.VMEM((blk, 1), F32), pltpu.VMEM((blk, HEAD_DIM), F32)],
        compiler_params=pltpu.CompilerParams(
            dimension_semantics=("parallel", "arbitrary"), vmem_limit_bytes=VMEM_LIMIT),
        name="fox_attn",
    )(qa, ka, v)


def _out_proj_kernel(y_ref, att_ref, h_ref, wg_ref, bg_ref, gs_ref, ga_ref, wos_ref, woa_ref, o_ref):
    g = _gelu(y_ref[0])
    z = jnp.dot(g.astype(BF16), wg_ref[...], preferred_element_type=F32) + bg_ref[...]
    o = g * _sigmoid(z)
    a = _rms(o, gs_ref[...])
    acc = h_ref[0] + jnp.dot(a.astype(BF16), wos_ref[...], preferred_element_type=F32)
    ssq = jnp.zeros((o.shape[0], 1), F32)
    for hh in range(N_HEADS):
        t = att_ref[0, hh]
        ssq = ssq + jnp.sum(t * t, axis=1, keepdims=True)
    inv = lax.rsqrt(ssq / (N_HEADS * HEAD_DIM) + RMS_EPS)
    for hh in range(N_HEADS):
        bh = att_ref[0, hh] * inv * ga_ref[hh]
        acc = acc + jnp.dot(bh.astype(BF16), woa_ref[hh], preferred_element_type=F32)
    o_ref[0] = acc


def _out_proj(y_ssm, y_att, h, wg, bg, gs, ga, wos, woa, tm):
    b, s, d = h.shape
    w = y_ssm.shape[-1]
    return pl.pallas_call(
        _out_proj_kernel,
        grid=(b, s // tm),
        in_specs=[
            pl.BlockSpec((1, tm, w), lambda i, j: (i, j, 0)),
            pl.BlockSpec((1, N_HEADS, tm, HEAD_DIM), lambda i, j: (i, 0, j, 0)),
            pl.BlockSpec((1, tm, d), lambda i, j: (i, j, 0)),
            pl.BlockSpec((w, w), lambda i, j: (0, 0)),
            pl.BlockSpec((1, w), lambda i, j: (0, 0)),
            pl.BlockSpec((1, w), lambda i, j: (0, 0)),
            pl.BlockSpec((N_HEADS, 1, HEAD_DIM), lambda i, j: (0, 0, 0)),
            pl.BlockSpec((w, d), lambda i, j: (0, 0)),
            pl.BlockSpec((N_HEADS, HEAD_DIM, d), lambda i, j: (0, 0, 0)),
        ],
        out_specs=pl.BlockSpec((1, tm, d), lambda i, j: (i, j, 0)),
        out_shape=jax.ShapeDtypeStruct((b, s, d), F32),
        compiler_params=pltpu.CompilerParams(
            dimension_semantics=("parallel", "parallel"), vmem_limit_bytes=VMEM_LIMIT),
        name="out_proj",
    )(y_ssm, y_att, h, wg, bg, gs, ga, wos, woa)


def _take_top(vals, payload, k):
    n_rows = vals.shape[0]
    rows = lax.broadcasted_iota(I32, vals.shape, 0)
    tops, picks = [], []
    for _ in range(k):
        m = jnp.max(vals, axis=0, keepdims=True)
        arg = jnp.min(jnp.where(vals == m, rows, n_rows), axis=0, keepdims=True)
        hit = rows == arg
        tops.append(m)
        picks.append(arg if payload is None else jnp.max(jnp.where(hit, payload, -1), axis=0, keepdims=True))
        vals = jnp.where(hit, -jnp.inf, vals)
    return jnp.concatenate(tops, axis=0), jnp.concatenate(picks, axis=0)


def _route_kernel(h_ref, g_ref, wq_ref, keys_ref, hn_ref, idx_ref, gate_ref):
    x = h_ref[...]
    hn = _rms(x, g_ref[...])
    hn_ref[...] = hn
    q = jnp.dot(hn.astype(BF16), wq_ref[...], preferred_element_type=F32)
    k = PEER_TOPK
    idx_rows, gate_rows = [], []
    for hh in range(PEER_HEADS):
        qh = q[:, 128 * hh: 128 * (hh + 1)].astype(BF16)
        sc = lax.dot_general(keys_ref[hh], qh, (((1,), (1,)), ((), ())), preferred_element_type=F32)
        t1, i1 = _take_top(sc[:PEER_KEYS], None, k)
        t2, i2 = _take_top(sc[PEER_KEYS:], None, k)
        cand = jnp.concatenate([t1[a:a + 1] + t2 for a in range(k)], axis=0)
        cidx = jnp.concatenate([i1[a:a + 1] * PEER_KEYS + i2 for a in range(k)], axis=0)
        best, idx = _take_top(cand, cidx, k)
        e = jnp.exp(best - best[0:1])
        gate = e / jnp.sum(e, axis=0, keepdims=True)
        idx_rows.append(idx)
        gate_rows.append(gate)
    idx_ref[...] = jnp.concatenate(idx_rows, axis=0).T
    gate_ref[...] = jnp.concatenate(gate_rows, axis=0).T


def _route(h2, g, wq, keys_cat, tm):
    t, d = h2.shape
    nk = PEER_HEADS * PEER_TOPK
    return pl.pallas_call(
        _route_kernel,
        grid=(t // tm,),
        in_specs=[
            pl.BlockSpec((tm, d), lambda i: (i, 0)),
            pl.BlockSpec((1, d), lambda i: (0, 0)),
            pl.BlockSpec((d, PEER_HEADS * 128), lambda i: (0, 0)),
            pl.BlockSpec((PEER_HEADS, 2 * PEER_KEYS, 128), lambda i: (0, 0, 0)),
        ],
        out_specs=[
            pl.BlockSpec((tm, d), lambda i: (i, 0)),
            pl.BlockSpec((tm, nk), lambda i: (i, 0)),
            pl.BlockSpec((tm, nk), lambda i: (i, 0)),
        ],
        out_shape=[
            jax.ShapeDtypeStruct((t, d), F32),
            jax.ShapeDtypeStruct((t, nk), I32),
            jax.ShapeDtypeStruct((t, nk), F32),
        ],
        compiler_params=pltpu.CompilerParams(
            dimension_semantics=("parallel",), vmem_limit_bytes=VMEM_LIMIT),
        name="peer_route",
    )(h2, g, wq, keys_cat)


def _coef_kernel(a_ref, g_ref, o_ref):
    o_ref[...] = _gelu(a_ref[...]) * g_ref[...]


def _coef(act, gate, tm):
    t, n = act.shape
    return pl.pallas_call(
        _coef_kernel,
        grid=(t // tm,),
        in_specs=[pl.BlockSpec((tm, n), lambda i: (i, 0)), pl.BlockSpec((tm, n), lambda i: (i, 0))],
        out_specs=pl.BlockSpec((tm, n), lambda i: (i, 0)),
        out_shape=jax.ShapeDtypeStruct((t, n), F32),
        compiler_params=pltpu.CompilerParams(dimension_semantics=("parallel",)),
        name="peer_coef",
    )(act, gate)


def _final_norm_kernel(x_ref, g_ref, o_ref):
    o_ref[...] = _rms(x_ref[...], g_ref[...])


def _final_norm(x2, g, tm):
    t, d = x2.shape
    return pl.pallas_call(
        _final_norm_kernel,
        grid=(t // tm,),
        in_specs=[pl.BlockSpec((tm, d), lambda i: (i, 0)), pl.BlockSpec((1, d), lambda i: (0, 0))],
        out_specs=pl.BlockSpec((tm, d), lambda i: (i, 0)),
        out_shape=jax.ShapeDtypeStruct((t, d), F32),
        compiler_params=pltpu.CompilerParams(dimension_semantics=("parallel",)),
        name="final_norm",
    )(x2, g)


SC_WORKERS = 32
SC_CORES = 2
SC_LANES = 16
SC_TOK = 8
SC_ROWS = 16
SC_BUFS = 4
SC_COLS = 256


def _sc_params():
    cp = pltpu.CompilerParams()
    if "needs_layout_passes" in pltpu.CompilerParams.__dataclass_fields__:
        cp = pltpu.CompilerParams(needs_layout_passes=False)
    return cp


def _sc_worker_id():
    return lax.axis_index("s") * SC_CORES + lax.axis_index("c")


def _sc_pipeline(tab_hbm, idx_v, rows, sems, n_chunks, compute):
    def gather(k, slot):
        return pltpu.make_async_copy(tab_hbm.at[idx_v.at[k]], rows[slot], sems[slot])

    for s in range(SC_BUFS - 1):
        gather(s, s).start()

    def group(i, carry):
        for s in range(SC_BUFS):
            k = i * SC_BUFS + s
            ahead = k + SC_BUFS - 1

            @pl.when(ahead < n_chunks)
            def _():
                gather(ahead, (s + SC_BUFS - 1) % SC_BUFS).start()

            gather(k, s).wait()
            compute(rows[s], k)
        return carry

    lax.fori_loop(0, n_chunks // SC_BUFS, group, 0)


def _tree_sum(xs):
    xs = list(xs)
    while len(xs) > 1:
        xs = [xs[i] + xs[i + 1] for i in range(0, len(xs) - 1, 2)] + ([xs[-1]] if len(xs) % 2 else [])
    return xs[0]


def _peer_scores_body(hn_hbm, idx_hbm, tab_hbm, out_hbm, h_v, idx_v, acc_v, act_v, *bufs, tok_per_worker):
    rows, sems = bufs[:SC_BUFS], bufs[SC_BUFS:]
    d = hn_hbm.shape[1]
    per_tok = PEER_HEADS * PEER_TOPK // SC_ROWS
    n_chunks = SC_TOK * per_tok
    n_reg = SC_COLS // SC_LANES
    lane = lax.iota(I32, SC_LANES)
    wid = _sc_worker_id()

    def compute(rw, k):
        t = k // per_tok
        c = k % per_tok
        for cc in range(d // SC_COLS):
            hv = [h_v[t, pl.ds(cc * SC_COLS + SC_LANES * j, SC_LANES)] for j in range(n_reg)]

            @plsc.parallel_loop(0, SC_ROWS, 1)
            def _(r):
                p = _tree_sum([rw[r, pl.ds(cc * SC_COLS + SC_LANES * j, SC_LANES)] * hv[j] for j in range(n_reg)])
                off = pl.multiple_of(r * SC_LANES, SC_LANES)
                if cc == 0:
                    acc_v[pl.ds(off, SC_LANES)] = p
                else:
                    acc_v[pl.ds(off, SC_LANES)] = acc_v[pl.ds(off, SC_LANES)] + p

        tot = _tree_sum([plsc.load_gather(acc_v, [lane * SC_LANES + j]) for j in range(SC_LANES)])
        off = pl.multiple_of(c * SC_ROWS, SC_LANES)
        act_v[t, pl.ds(off, SC_LANES)] = tot

    def batch(bi, carry):
        t0 = pl.multiple_of(wid * tok_per_worker + bi * SC_TOK, SC_TOK)
        pltpu.sync_copy(hn_hbm.at[pl.ds(t0, SC_TOK)], h_v)
        pltpu.sync_copy(idx_hbm.at[pl.ds(t0 * per_tok, n_chunks)], idx_v)
        _sc_pipeline(tab_hbm, idx_v, rows, sems, n_chunks, compute)
        pltpu.sync_copy(act_v, out_hbm.at[pl.ds(t0, SC_TOK)])
        return carry

    lax.fori_loop(0, tok_per_worker // SC_TOK, batch, 0)


def _peer_scores(hn, idx4, u_tab):
    t, d = hn.shape
    nk = PEER_HEADS * PEER_TOPK
    mesh = plsc.VectorSubcoreMesh(core_axis_name="c", subcore_axis_name="s")
    body = functools.partial(_peer_scores_body, tok_per_worker=t // SC_WORKERS)
    return pl.kernel(
        body,
        out_type=jax.ShapeDtypeStruct((t, nk), F32),
        mesh=mesh,
        scratch_types=[
            pltpu.VMEM((SC_TOK, d), F32),
            pltpu.VMEM((SC_TOK * nk // SC_ROWS, SC_ROWS), I32),
            pltpu.VMEM((SC_ROWS * SC_LANES,), F32),
            pltpu.VMEM((SC_TOK, nk), F32),
        ] + [pltpu.VMEM((SC_ROWS, d), F32)] * SC_BUFS + [pltpu.SemaphoreType.DMA] * SC_BUFS,
        compiler_params=_sc_params(),
        name="peer_scores_sc",
    )(hn, idx4, u_tab)


def _peer_combine_body(h_hbm, idx_hbm, coef_hbm, tab_hbm, out_hbm, o_v, idx_v, coef_v, *bufs, tok_per_worker):
    rows, sems = bufs[:SC_BUFS], bufs[SC_BUFS:]
    d = h_hbm.shape[1]
    nk = PEER_HEADS * PEER_TOPK
    per_tok = nk // SC_ROWS
    n_chunks = SC_TOK * per_tok
    n_reg = SC_COLS // SC_LANES
    wid = _sc_worker_id()

    def compute(rw, k):
        t = k // per_tok
        for cc in range(d // SC_COLS):
            acc0 = tuple(o_v[t, pl.ds(cc * SC_COLS + SC_LANES * j, SC_LANES)] for j in range(n_reg))

            def rbody(r, acc):
                w = plsc.load_gather(coef_v, [jnp.full((SC_LANES,), k * SC_ROWS + r, I32)])
                return tuple(acc[j] + w * rw[r, pl.ds(cc * SC_COLS + SC_LANES * j, SC_LANES)]
                             for j in range(n_reg))

            acc = lax.fori_loop(0, SC_ROWS, rbody, acc0)
            for j in range(n_reg):
                o_v[t, pl.ds(cc * SC_COLS + SC_LANES * j, SC_LANES)] = acc[j]

    def batch(bi, carry):
        t0 = pl.multiple_of(wid * tok_per_worker + bi * SC_TOK, SC_TOK)
        pltpu.sync_copy(h_hbm.at[pl.ds(t0, SC_TOK)], o_v)
        pltpu.sync_copy(idx_hbm.at[pl.ds(t0 * per_tok, n_chunks)], idx_v)
        pltpu.sync_copy(coef_hbm.at[pl.ds(t0 * nk, SC_TOK * nk)], coef_v)
        _sc_pipeline(tab_hbm, idx_v, rows, sems, n_chunks, compute)
        pltpu.sync_copy(o_v, out_hbm.at[pl.ds(t0, SC_TOK)])
        return carry

    lax.fori_loop(0, tok_per_worker // SC_TOK, batch, 0)


def _peer_combine(h2, idx4, coef_flat, v_tab):
    t, d = h2.shape
    nk = PEER_HEADS * PEER_TOPK
    mesh = plsc.VectorSubcoreMesh(core_axis_name="c", subcore_axis_name="s")
    body = functools.partial(_peer_combine_body, tok_per_worker=t // SC_WORKERS)
    return pl.kernel(
        body,
        out_type=jax.ShapeDtypeStruct((t, d), F32),
        mesh=mesh,
        scratch_types=[
            pltpu.VMEM((SC_TOK, d), F32),
            pltpu.VMEM((SC_TOK * nk // SC_ROWS, SC_ROWS), I32),
            pltpu.VMEM((SC_TOK * nk,), F32),
        ] + [pltpu.VMEM((SC_ROWS, d), F32)] * SC_BUFS + [pltpu.SemaphoreType.DMA] * SC_BUFS,
        compiler_params=_sc_params(),
        name="peer_combine_sc",
    )(h2, idx4, coef_flat, v_tab)


def _mixers(h, l, norm1_g, w_in, ssm, fox_b_f, g_ssm_out, g_attn_out, w_o, ssm_w_glu, ssm_b_glu, tm, blk):
    b, s, d = h.shape
    w = N_HEADS * HEAD_DIM
    wl = w_in[l]
    wm = wl[:, :4 * w].astype(BF16)
    wf = jnp.pad(wl[:, 4 * w:], ((0, 0), (0, 128 - N_HEADS)))
    bf = jnp.pad(fox_b_f[l], (0, 128 - N_HEADS)).reshape(1, 128)
    u, qa, ka, v = _in_proj(h, norm1_g[l].reshape(1, d), wm, wf, bf, tm)

    lc = SSM_CHUNK
    nc = s // lc
    g = w // SSM_GROUP_CH
    ug = u.reshape(b, nc, lc, g, SSM_GROUP_CH).transpose(3, 1, 0, 2, 4).reshape(g, nc * b, lc * SSM_GROUP_CH)
    yg = _ssm(ug, *ssm, n_chunks=nc, batch=b)
    y_ssm = yg.reshape(g, nc, b, lc, SSM_GROUP_CH).transpose(2, 1, 3, 0, 4).reshape(b, s, w)

    y_att = _attention(qa.reshape(b * N_HEADS, s, AUG), ka.reshape(b * N_HEADS, s, AUG),
                       v.reshape(b * N_HEADS, s, HEAD_DIM), blk).reshape(b, N_HEADS, s, HEAD_DIM)

    wo = w_o[l].astype(BF16)
    return _out_proj(y_ssm, y_att, h, ssm_w_glu[l].astype(BF16), ssm_b_glu[l].reshape(1, w),
                     g_ssm_out[l].reshape(1, w), g_attn_out[l].reshape(N_HEADS, 1, HEAD_DIM),
                     wo[:w], wo[w:].reshape(N_HEADS, HEAD_DIM, d), tm)


def _keys_cat(keys_l):
    z = jnp.zeros_like(keys_l[:, 0])
    top = jnp.concatenate([keys_l[:, 0], z], axis=-1)
    bot = jnp.concatenate([z, keys_l[:, 1]], axis=-1)
    return jnp.concatenate([top, bot], axis=1).astype(BF16)


def kernel(x, norm1_g, w_in, ssm_lambda_re, ssm_lambda_im, ssm_log_dt, ssm_b_re, ssm_b_im, ssm_c_re, ssm_c_im, ssm_d, ssm_w_glu, ssm_b_glu, fox_b_f, g_ssm_out, g_attn_out, w_o, norm2_g, peer_w_q, peer_keys, peer_u, peer_v, norm_f):
    b, s, d = x.shape
    depth = w_in.shape[0]
    nk = PEER_HEADS * PEER_TOPK
    tm = min(512, s)
    blk = min(256, s)
    n_split = 2 if b % 2 == 0 else 1
    bh = b // n_split
    t = bh * s
    parts = [x[i * bh:(i + 1) * bh] for i in range(n_split)]
    for l in range(depth):
        ssm = _ssm_tables(ssm_lambda_re[l], ssm_lambda_im[l], ssm_log_dt[l], ssm_b_re[l], ssm_b_im[l],
                          ssm_c_re[l], ssm_c_im[l], ssm_d[l])
        wq = peer_w_q[l].astype(BF16)
        kc = _keys_cat(peer_keys[l])
        nxt = []
        for h in parts:
            h = _mixers(h, l, norm1_g, w_in, ssm, fox_b_f, g_ssm_out, g_attn_out, w_o, ssm_w_glu, ssm_b_glu, tm, blk)
            h2 = h.reshape(t, d)
            hn, idx, gate = _route(h2, norm2_g[l].reshape(1, d), wq, kc, min(256, t))
            idx4 = idx.reshape(t * nk // SC_ROWS, SC_ROWS)
            act = _peer_scores(hn, idx4, peer_u[l])
            coef = _coef(act, gate, min(1024, t))
            nxt.append(_peer_combine(h2, idx4, coef.reshape(t * nk), peer_v[l]).reshape(bh, s, d))
        parts = nxt
    outs = [_final_norm(h.reshape(t, d), norm_f.reshape(1, d), min(512, t)).reshape(bh, s, d) for h in parts]
    return jnp.concatenate(outs, axis=0) if n_split > 1 else outs[0]
```

```python
import functools
import math

import jax
import jax.numpy as jnp
from jax import lax
from jax.experimental import pallas as pl
from jax.experimental.pallas import tpu as pltpu
from jax.experimental.pallas import tpu_sc as plsc

F32 = jnp.float32
BF16 = jnp.bfloat16
I32 = jnp.int32

RMS_EPS = 1e-6
SSM_GROUP_CH = 16
SSM_STATE = 64
SSM_CHUNK = 64
HEAD_DIM = 64
N_HEADS = 8
AUG = 128
PEER_HEADS = 8
PEER_KEYS = 128
PEER_TOPK = 16
PEER_HALF = 64
VMEM_LIMIT = 56 * 1024 * 1024


def _rms(x, g):
    return x * lax.rsqrt(jnp.mean(x * x, axis=-1, keepdims=True) + RMS_EPS) * g


def _gelu(x):
    c = math.sqrt(2.0 / math.pi)
    return 0.5 * x * (1.0 + jnp.tanh(c * (x + 0.044715 * (x * x * x))))


def _sigmoid(x):
    return 1.0 / (1.0 + jnp.exp(-x))


def _in_proj_kernel(h_ref, g_ref, wm_ref, wf_ref, bf_ref, u_ref, qa_ref, ka_ref, v_ref, cum_ref):
    j = pl.program_id(1)

    @pl.when(j == 0)
    def _():
        cum_ref[...] = jnp.zeros_like(cum_ref)

    x = h_ref[0]
    tm = x.shape[0]
    xn = _rms(x, g_ref[...])
    proj = jnp.dot(xn.astype(BF16), wm_ref[...], preferred_element_type=F32)
    f = jnp.dot(xn, wf_ref[...], precision=lax.Precision.HIGHEST, preferred_element_type=F32) + bf_ref[...]
    logf = jnp.minimum(f, 0.0) - jnp.log(1.0 + jnp.exp(-jnp.abs(f)))
    row = lax.broadcasted_iota(I32, (tm, tm), 0)
    col = lax.broadcasted_iota(I32, (tm, tm), 1)
    tri = (row >= col).astype(F32)
    cum = jnp.dot(tri, logf, precision=lax.Precision.HIGHEST, preferred_element_type=F32) + cum_ref[0:1, :]
    cum_ref[0:1, :] = cum[tm - 1:tm, :]

    w = HEAD_DIM * N_HEADS
    u_ref[0] = proj[:, :w].astype(BF16)
    lane = lax.broadcasted_iota(I32, (tm, AUG), 1)
    scale = HEAD_DIM ** -0.5
    for hh in range(N_HEADS):
        pair = hh // 2
        q2 = proj[:, w + 128 * pair: w + 128 * pair + 128]
        k2 = proj[:, 2 * w + 128 * pair: 2 * w + 128 * pair + 128]
        v2 = proj[:, 3 * w + 128 * pair: 3 * w + 128 * pair + 128]
        if hh % 2 == 1:
            q2 = pltpu.roll(q2, 64, axis=1)
            k2 = pltpu.roll(k2, 64, axis=1)
            vh = v2[:, 64:]
        else:
            vh = v2[:, :64]
        c = jnp.broadcast_to(cum[:, hh:hh + 1], (tm, AUG))
        c1 = c.astype(BF16).astype(F32)
        r1 = c - c1
        c2 = r1.astype(BF16).astype(F32)
        c3 = r1 - c2
        one = jnp.ones((tm, AUG), F32)
        zero = jnp.zeros((tm, AUG), F32)
        qa = jnp.where(lane < 64, q2 * scale,
             jnp.where(lane == 64, c1, jnp.where(lane == 65, c2, jnp.where(lane == 66, c3,
             jnp.where(lane < 70, one, zero)))))
        ka = jnp.where(lane < 64, k2,
             jnp.where(lane < 67, one, jnp.where(lane == 67, -c1, jnp.where(lane == 68, -c2,
             jnp.where(lane == 69, -c3, zero)))))
        qa_ref[0, hh] = qa.astype(BF16)
        ka_ref[0, hh] = ka.astype(BF16)
        v_ref[0, hh] = vh.astype(BF16)


def _in_proj(h, g, wm, wf, bf, tm):
    b, s, d = h.shape
    w = HEAD_DIM * N_HEADS
    return pl.pallas_call(
        _in_proj_kernel,
        grid=(b, s // tm),
        in_specs=[
            pl.BlockSpec((1, tm, d), lambda i, j: (i, j, 0)),
            pl.BlockSpec((1, d), lambda i, j: (0, 0)),
            pl.BlockSpec((d, 4 * w), lambda i, j: (0, 0)),
            pl.BlockSpec((d, 128), lambda i, j: (0, 0)),
            pl.BlockSpec((1, 128), lambda i, j: (0, 0)),
        ],
        out_specs=[
            pl.BlockSpec((1, tm, w), lambda i, j: (i, j, 0)),
            pl.BlockSpec((1, N_HEADS, tm, AUG), lambda i, j: (i, 0, j, 0)),
            pl.BlockSpec((1, N_HEADS, tm, AUG), lambda i, j: (i, 0, j, 0)),
            pl.BlockSpec((1, N_HEADS, tm, HEAD_DIM), lambda i, j: (i, 0, j, 0)),
        ],
        out_shape=[
            jax.ShapeDtypeStruct((b, s, w), BF16),
            jax.ShapeDtypeStruct((b, N_HEADS, s, AUG), BF16),
            jax.ShapeDtypeStruct((b, N_HEADS, s, AUG), BF16),
            jax.ShapeDtypeStruct((b, N_HEADS, s, HEAD_DIM), BF16),
        ],
        scratch_shapes=[pltpu.VMEM((8, 128), F32)],
        compiler_params=pltpu.CompilerParams(
            dimension_semantics=("parallel", "arbitrary"), vmem_limit_bytes=VMEM_LIMIT),
        cost_estimate=pl.CostEstimate(flops=2 * b * s * d * (4 * w + 128), transcendentals=2 * b * s * 128,
                                      bytes_accessed=4 * b * s * d + 2 * d * (4 * w) + 2 * b * s * (w + 2 * N_HEADS * AUG + w)),
        name="in_proj",
    )(h, g, wm, wf, bf)


def _ssm_kernel(u_ref, m_ref, w_ref, r_ref, a_ref, d_ref, y_ref, e_scr, *, n_chunks, batch):
    u = u_ref[0]
    y = jnp.dot(u, m_ref[0], preferred_element_type=F32)
    e_scr[...] = jnp.dot(u, w_ref[0], preferred_element_type=F32)
    ar = a_ref[0, 0:1, :]
    ai = a_ref[0, 1:2, :]

    def body(c, s):
        off = pl.multiple_of(c * batch, batch)
        e_c = e_scr[pl.ds(off, batch), :]
        e_scr[pl.ds(off, batch), :] = s
        return ar * s + ai * pltpu.roll(s, SSM_STATE, axis=1) + e_c

    lax.fori_loop(0, n_chunks, body, jnp.zeros((batch, 2 * SSM_STATE), F32))
    y = y + jnp.dot(e_scr[...].astype(BF16), r_ref[0], preferred_element_type=F32)
    y_ref[0] = y + u.astype(F32) * d_ref[0]


def _ssm(ug, m, w, r, a, d, n_chunks, batch):
    g, rows, width = ug.shape
    kern = functools.partial(_ssm_kernel, n_chunks=n_chunks, batch=batch)
    return pl.pallas_call(
        kern,
        grid=(g,),
        in_specs=[
            pl.BlockSpec((1, rows, width), lambda i: (i, 0, 0)),
            pl.BlockSpec((1, width, width), lambda i: (i, 0, 0)),
            pl.BlockSpec((1, width, 2 * SSM_STATE), lambda i: (i, 0, 0)),
            pl.BlockSpec((1, 2 * SSM_STATE, width), lambda i: (i, 0, 0)),
            pl.BlockSpec((1, 8, 2 * SSM_STATE), lambda i: (i, 0, 0)),
            pl.BlockSpec((1, 1, width), lambda i: (i, 0, 0)),
        ],
        out_specs=pl.BlockSpec((1, rows, width), lambda i: (i, 0, 0)),
        out_shape=jax.ShapeDtypeStruct((g, rows, width), F32),
        scratch_shapes=[pltpu.VMEM((rows, 2 * SSM_STATE), F32)],
        compiler_params=pltpu.CompilerParams(
            dimension_semantics=("parallel",), vmem_limit_bytes=VMEM_LIMIT),
        cost_estimate=pl.CostEstimate(flops=2 * g * rows * width * (width + 4 * SSM_STATE), transcendentals=0,
                                      bytes_accessed=g * (2 * width * (width + 4 * SSM_STATE) + 6 * rows * width)),
        name="ssm",
    )(ug, m, w, r, a, d)


def _ssm_tables(lam_re, lam_im, log_dt, b_re, b_im, c_re, c_im, d_skip):
    hp = lax.Precision.HIGHEST
    lc = SSM_CHUNK
    g, p = lam_re.shape
    hc = SSM_GROUP_CH
    dt = jnp.exp(log_dt)[:, None]
    a_re = jnp.exp(lam_re * dt) * jnp.cos(lam_im * dt)
    a_im = jnp.exp(lam_re * dt) * jnp.sin(lam_im * dt)
    den = lam_re * lam_re + lam_im * lam_im
    nr = a_re - 1.0
    z_re = (nr * lam_re + a_im * lam_im) / den
    z_im = (a_im * lam_re - nr * lam_im) / den
    bb_re = z_re[..., None] * b_re - z_im[..., None] * b_im
    bb_im = z_re[..., None] * b_im + z_im[..., None] * b_re
    tau = jnp.arange(lc + 1, dtype=F32)[:, None, None]
    mag = jnp.exp(tau * (lam_re * dt)[None])
    ang = tau * (lam_im * dt)[None]
    p_re = mag * jnp.cos(ang)
    p_im = mag * jnp.sin(ang)
    ab_re = p_re[:lc, :, :, None] * bb_re[None] - p_im[:lc, :, :, None] * bb_im[None]
    ab_im = p_re[:lc, :, :, None] * bb_im[None] + p_im[:lc, :, :, None] * bb_re[None]
    kk = (jnp.einsum('ghp,tgpk->tghk', c_re, ab_re, precision=hp)
          - jnp.einsum('ghp,tgpk->tghk', c_im, ab_im, precision=hp))
    t_idx = jnp.arange(lc)
    lag = t_idx[None, :] - t_idx[:, None]
    toe = jnp.where((lag >= 0)[:, :, None, None, None], kk[jnp.clip(lag, 0, lc - 1)], 0.0)
    m = toe.transpose(2, 0, 4, 1, 3).reshape(g, lc * hc, lc * hc)
    w_re = ab_re[::-1].transpose(1, 0, 3, 2).reshape(g, lc * hc, p)
    w_im = ab_im[::-1].transpose(1, 0, 3, 2).reshape(g, lc * hc, p)
    w = jnp.concatenate([w_re, w_im], axis=-1)
    q_re = p_re[1:, :, None, :] * c_re[None] - p_im[1:, :, None, :] * c_im[None]
    q_im = p_re[1:, :, None, :] * c_im[None] + p_im[1:, :, None, :] * c_re[None]
    r = jnp.concatenate([q_re.transpose(1, 3, 0, 2).reshape(g, p, lc * hc),
                         -q_im.transpose(1, 3, 0, 2).reshape(g, p, lc * hc)], axis=1)
    al_re, al_im = p_re[lc], p_im[lc]
    a = jnp.zeros((g, 8, 2 * p), F32)
    a = a.at[:, 0, :].set(jnp.concatenate([al_re, al_re], axis=-1))
    a = a.at[:, 1, :].set(jnp.concatenate([-al_im, al_im], axis=-1))
    d = jnp.tile(d_skip[:, None, :], (1, lc, 1)).reshape(g, 1, lc * hc)
    return m.astype(BF16), w.astype(BF16), r.astype(BF16), a, d


def _attn_kernel(q_ref, k_ref, v_ref, o_ref, m_ref, l_ref, acc_ref, *, blk):
    i = pl.program_id(1)
    q = q_ref[0]
    m_ref[...] = jnp.full_like(m_ref, -jnp.inf)
    l_ref[...] = jnp.zeros_like(l_ref)
    acc_ref[...] = jnp.zeros_like(acc_ref)

    def step(j, masked):
        off = pl.multiple_of(j * blk, blk)
        k = k_ref[0, pl.ds(off, blk), :]
        v = v_ref[0, pl.ds(off, blk), :]
        s = lax.dot_general(q, k, (((1,), (1,)), ((), ())), preferred_element_type=F32)
        if masked:
            row = lax.broadcasted_iota(I32, s.shape, 0)
            col = lax.broadcasted_iota(I32, s.shape, 1)
            s = jnp.where(row >= col, s, -jnp.inf)
        m_prev = m_ref[...]
        m_new = jnp.maximum(m_prev, jnp.max(s, axis=1, keepdims=True))
        p = jnp.exp(s - m_new)
        alpha = jnp.exp(m_prev - m_new)
        l_ref[...] = alpha * l_ref[...] + jnp.sum(p, axis=1, keepdims=True)
        acc_ref[...] = alpha * acc_ref[...] + jnp.dot(p.astype(BF16), v, preferred_element_type=F32)
        m_ref[...] = m_new

    def body(j, c):
        step(j, False)
        return c

    lax.fori_loop(0, i, body, 0)
    step(i, True)
    o_ref[0] = acc_ref[...] / l_ref[...]


def _attention(qa, ka, v, blk):
    bh, s, _ = qa.shape
    kern = functools.partial(_attn_kernel, blk=blk)
    return pl.pallas_call(
        kern,
        grid=(bh, s // blk),
        in_specs=[
            pl.BlockSpec((1, blk, AUG), lambda b, i: (b, i, 0)),
            pl.BlockSpec((1, s, AUG), lambda b, i: (b, 0, 0)),
            pl.BlockSpec((1, s, HEAD_DIM), lambda b, i: (b, 0, 0)),
        ],
        out_specs=pl.BlockSpec((1, blk, HEAD_DIM), lambda b, i: (b, i, 0)),
        out_shape=jax.ShapeDtypeStruct((bh, s, HEAD_DIM), F32),
        scratch_shapes=[pltpu.VMEM((blk, 1), F32), pltpu.VMEM((blk, 1), F32), pltpu.VMEM((blk, HEAD_DIM), F32)],
        compiler_params=pltpu.CompilerParams(
            dimension_semantics=("parallel", "arbitrary"), vmem_limit_bytes=VMEM_LIMIT),
        cost_estimate=pl.CostEstimate(flops=bh * s * s * (AUG + HEAD_DIM), transcendentals=bh * s * s // 2,
                                      bytes_accessed=bh * s * (2 * 2 * AUG + 2 * HEAD_DIM + 4 * HEAD_DIM)),
        name="fox_attn",
    )(qa, ka, v)


def _out_proj_kernel(y_ref, att_ref, h_ref, wg_ref, bg_ref, gs_ref, ga_ref, wos_ref, woa_ref, o_ref):
    g = _gelu(y_ref[0])
    z = jnp.dot(g.astype(BF16), wg_ref[...], preferred_element_type=F32) + bg_ref[...]
    o = g * _sigmoid(z)
    a = _rms(o, gs_ref[...])
    acc = h_ref[0] + jnp.dot(a.astype(BF16), wos_ref[...], preferred_element_type=F32)
    ssq = jnp.zeros((o.shape[0], 1), F32)
    for hh in range(N_HEADS):
        t = att_ref[0, hh]
        ssq = ssq + jnp.sum(t * t, axis=1, keepdims=True)
    inv = lax.rsqrt(ssq / (N_HEADS * HEAD_DIM) + RMS_EPS)
    for hh in range(N_HEADS):
        bh = att_ref[0, hh] * inv * ga_ref[hh]
        acc = acc + jnp.dot(bh.astype(BF16), woa_ref[hh], preferred_element_type=F32)
    o_ref[0] = acc


def _out_proj(y_ssm, y_att, h, wg, bg, gs, ga, wos, woa, tm):
    b, s, d = h.shape
    w = y_ssm.shape[-1]
    return pl.pallas_call(
        _out_proj_kernel,
        grid=(b, s // tm),
        in_specs=[
            pl.BlockSpec((1, tm, w), lambda i, j: (i, j, 0)),
            pl.BlockSpec((1, N_HEADS, tm, HEAD_DIM), lambda i, j: (i, 0, j, 0)),
            pl.BlockSpec((1, tm, d), lambda i, j: (i, j, 0)),
            pl.BlockSpec((w, w), lambda i, j: (0, 0)),
            pl.BlockSpec((1, w), lambda i, j: (0, 0)),
            pl.BlockSpec((1, w), lambda i, j: (0, 0)),
            pl.BlockSpec((N_HEADS, 1, HEAD_DIM), lambda i, j: (0, 0, 0)),
            pl.BlockSpec((w, d), lambda i, j: (0, 0)),
            pl.BlockSpec((N_HEADS, HEAD_DIM, d), lambda i, j: (0, 0, 0)),
        ],
        out_specs=pl.BlockSpec((1, tm, d), lambda i, j: (i, j, 0)),
        out_shape=jax.ShapeDtypeStruct((b, s, d), F32),
        compiler_params=pltpu.CompilerParams(
            dimension_semantics=("parallel", "parallel"), vmem_limit_bytes=VMEM_LIMIT),
        cost_estimate=pl.CostEstimate(flops=2 * b * s * (w * w + 2 * w * d), transcendentals=2 * b * s * w,
                                      bytes_accessed=b * s * (8 * w + 8 * d) + 2 * (w * w + 2 * w * d)),
        name="out_proj",
    )(y_ssm, y_att, h, wg, bg, gs, ga, wos, woa)


def _take_top(vals, payload, k):
    n_rows = vals.shape[0]
    rows = lax.broadcasted_iota(I32, vals.shape, 0)
    tops, picks = [], []
    for _ in range(k):
        m = jnp.max(vals, axis=0, keepdims=True)
        arg = jnp.min(jnp.where(vals == m, rows, n_rows), axis=0, keepdims=True)
        hit = rows == arg
        tops.append(m)
        picks.append(arg if payload is None else jnp.max(jnp.where(hit, payload, -1), axis=0, keepdims=True))
        vals = jnp.where(hit, -jnp.inf, vals)
    return jnp.concatenate(tops, axis=0), jnp.concatenate(picks, axis=0)


def _route_kernel(h_ref, g_ref, wq_ref, keys_ref, hn_ref, idx_ref, gate_ref):
    x = h_ref[...]
    hn = _rms(x, g_ref[...])
    hn_ref[...] = hn
    q = jnp.dot(hn.astype(BF16), wq_ref[...], preferred_element_type=F32)
    k = PEER_TOPK
    idx_rows, gate_rows = [], []
    for hh in range(PEER_HEADS):
        qh = q[:, 128 * hh: 128 * (hh + 1)].astype(BF16)
        sc = lax.dot_general(keys_ref[hh], qh, (((1,), (1,)), ((), ())), preferred_element_type=F32)
        t1, i1 = _take_top(sc[:PEER_KEYS], None, k)
        t2, i2 = _take_top(sc[PEER_KEYS:], None, k)
        cand = jnp.concatenate([t1[a:a + 1] + t2 for a in range(k)], axis=0)
        cidx = jnp.concatenate([i1[a:a + 1] * PEER_KEYS + i2 for a in range(k)], axis=0)
        best, idx = _take_top(cand, cidx, k)
        e = jnp.exp(best - best[0:1])
        gate = e / jnp.sum(e, axis=0, keepdims=True)
        idx_rows.append(idx)
        gate_rows.append(gate)
    idx_ref[...] = jnp.concatenate(idx_rows, axis=0).T
    gate_ref[...] = jnp.concatenate(gate_rows, axis=0).T


def _route(h2, g, wq, keys_cat, tm):
    t, d = h2.shape
    nk = PEER_HEADS * PEER_TOPK
    return pl.pallas_call(
        _route_kernel,
        grid=(t // tm,),
        in_specs=[
            pl.BlockSpec((tm, d), lambda i: (i, 0)),
            pl.BlockSpec((1, d), lambda i: (0, 0)),
            pl.BlockSpec((d, PEER_HEADS * 128), lambda i: (0, 0)),
            pl.BlockSpec((PEER_HEADS, 2 * PEER_KEYS, 128), lambda i: (0, 0, 0)),
        ],
        out_specs=[
            pl.BlockSpec((tm, d), lambda i: (i, 0)),
            pl.BlockSpec((tm, nk), lambda i: (i, 0)),
            pl.BlockSpec((tm, nk), lambda i: (i, 0)),
        ],
        out_shape=[
            jax.ShapeDtypeStruct((t, d), F32),
            jax.ShapeDtypeStruct((t, nk), I32),
            jax.ShapeDtypeStruct((t, nk), F32),
        ],
        compiler_params=pltpu.CompilerParams(
            dimension_semantics=("parallel",), vmem_limit_bytes=VMEM_LIMIT),
        cost_estimate=pl.CostEstimate(flops=2 * t * d * PEER_HEADS * 128 + 2 * t * PEER_HEADS * 256 * 128,
                                      transcendentals=t * nk, bytes_accessed=t * (8 * d + 8 * nk) + 2 * d * PEER_HEADS * 128),
        name="peer_route",
    )(h2, g, wq, keys_cat)


def _final_norm_kernel(x_ref, g_ref, o_ref):
    o_ref[...] = _rms(x_ref[...], g_ref[...])


def _final_norm(x2, g, tm):
    t, d = x2.shape
    return pl.pallas_call(
        _final_norm_kernel,
        grid=(t // tm,),
        in_specs=[pl.BlockSpec((tm, d), lambda i: (i, 0)), pl.BlockSpec((1, d), lambda i: (0, 0))],
        out_specs=pl.BlockSpec((tm, d), lambda i: (i, 0)),
        out_shape=jax.ShapeDtypeStruct((t, d), F32),
        compiler_params=pltpu.CompilerParams(dimension_semantics=("parallel",)),
        cost_estimate=pl.CostEstimate(flops=4 * t * d, transcendentals=t, bytes_accessed=8 * t * d),
        name="final_norm",
    )(x2, g)


SC_WORKERS = 32
SC_CORES = 2
SC_LANES = 16
SC_TOK = 8
SC_ROWS = 16
SC_BUFS = 4
SC_COLS = 256


def _sc_params():
    cp = pltpu.CompilerParams()
    if "needs_layout_passes" in pltpu.CompilerParams.__dataclass_fields__:
        cp = pltpu.CompilerParams(needs_layout_passes=False)
    return cp


def _sc_worker_id():
    return lax.axis_index("s") * SC_CORES + lax.axis_index("c")


def _tree_sum(xs):
    xs = list(xs)
    while len(xs) > 1:
        xs = [xs[i] + xs[i + 1] for i in range(0, len(xs) - 1, 2)] + ([xs[-1]] if len(xs) % 2 else [])
    return xs[0]


def _peer_scores_body(hn_hbm, idx_hbm, tab_hbm, out_hbm, h_v, idx_v, acc_v, act_v, *bufs, tok_per_worker):
    rows, sems = bufs[:SC_BUFS], bufs[SC_BUFS:]
    d = hn_hbm.shape[1]
    per_tok = PEER_HEADS * PEER_TOPK // SC_ROWS
    n_chunks = SC_TOK * per_tok
    n_reg = SC_COLS // SC_LANES
    lane = lax.iota(I32, SC_LANES)
    wid = _sc_worker_id()

    def compute(rw, k):
        t = k // per_tok
        c = k % per_tok
        for cc in range(d // SC_COLS):
            hv = [h_v[t, pl.ds(cc * SC_COLS + SC_LANES * j, SC_LANES)] for j in range(n_reg)]

            @plsc.parallel_loop(0, SC_ROWS, 1)
            def _(r):
                p = _tree_sum([rw[r, pl.ds(cc * SC_COLS + SC_LANES * j, SC_LANES)] * hv[j] for j in range(n_reg)])
                off = pl.multiple_of(r * SC_LANES, SC_LANES)
                if cc == 0:
                    acc_v[pl.ds(off, SC_LANES)] = p
                else:
                    acc_v[pl.ds(off, SC_LANES)] = acc_v[pl.ds(off, SC_LANES)] + p

        tot = _tree_sum([plsc.load_gather(acc_v, [lane * SC_LANES + j]) for j in range(SC_LANES)])
        off = pl.multiple_of(c * SC_ROWS, SC_LANES)
        act_v[t, pl.ds(off, SC_LANES)] = tot

    def batch(bi, carry):
        t0 = pl.multiple_of(wid * tok_per_worker + bi * SC_TOK, SC_TOK)
        pltpu.sync_copy(hn_hbm.at[pl.ds(t0, SC_TOK)], h_v)
        pltpu.sync_copy(idx_hbm.at[pl.ds(t0 * per_tok, n_chunks)], idx_v)
        _sc_pipeline(tab_hbm, idx_v, rows, sems, n_chunks, compute)
        pltpu.sync_copy(act_v, out_hbm.at[pl.ds(t0, SC_TOK)])
        return carry

    lax.fori_loop(0, tok_per_worker // SC_TOK, batch, 0)


def _peer_scores(hn, idx4, u_tab):
    t, d = hn.shape
    nk = PEER_HEADS * PEER_TOPK
    mesh = plsc.VectorSubcoreMesh(core_axis_name="c", subcore_axis_name="s")
    body = functools.partial(_peer_scores_body, tok_per_worker=t // SC_WORKERS)
    return pl.kernel(
        body,
        out_type=jax.ShapeDtypeStruct((t, nk), F32),
        mesh=mesh,
        scratch_types=[
            pltpu.VMEM((SC_TOK, d), F32),
            pltpu.VMEM((SC_TOK * nk // SC_ROWS, SC_ROWS), I32),
            pltpu.VMEM((SC_ROWS * SC_LANES,), F32),
            pltpu.VMEM((SC_TOK, nk), F32),
        ] + [pltpu.VMEM((SC_ROWS, d), F32)] * SC_BUFS + [pltpu.SemaphoreType.DMA] * SC_BUFS,
        compiler_params=_sc_params(),
        cost_estimate=pl.CostEstimate(flops=2 * t * nk * d, transcendentals=0, bytes_accessed=4 * t * nk * d + 4 * t * d),
        name="peer_scores_sc",
    )(hn, idx4, u_tab)


def _peer_combine_body(h_hbm, idx_hbm, coef_hbm, tab_hbm, out_hbm, o_v, idx_v, coef_v, *bufs, tok_per_worker):
    rows, sems = bufs[:SC_BUFS], bufs[SC_BUFS:]
    d = h_hbm.shape[1]
    nk = PEER_HEADS * PEER_TOPK
    per_tok = nk // SC_ROWS
    n_chunks = SC_TOK * per_tok
    n_reg = SC_COLS // SC_LANES
    wid = _sc_worker_id()

    def compute(rw, k):
        t = k // per_tok
        for cc in range(d // SC_COLS):
            acc0 = tuple(o_v[t, pl.ds(cc * SC_COLS + SC_LANES * j, SC_LANES)] for j in range(n_reg))

            def rbody(r, acc):
                w = plsc.load_gather(coef_v, [jnp.full((SC_LANES,), k * SC_ROWS + r, I32)])
                return tuple(acc[j] + w * rw[r, pl.ds(cc * SC_COLS + SC_LANES * j, SC_LANES)]
                             for j in range(n_reg))

            acc = lax.fori_loop(0, SC_ROWS, rbody, acc0)
            for j in range(n_reg):
                o_v[t, pl.ds(cc * SC_COLS + SC_LANES * j, SC_LANES)] = acc[j]

    def batch(bi, carry):
        t0 = pl.multiple_of(wid * tok_per_worker + bi * SC_TOK, SC_TOK)
        pltpu.sync_copy(h_hbm.at[pl.ds(t0, SC_TOK)], o_v)
        pltpu.sync_copy(idx_hbm.at[pl.ds(t0 * per_tok, n_chunks)], idx_v)
        pltpu.sync_copy(coef_hbm.at[pl.ds(t0 * nk, SC_TOK * nk)], coef_v)
        _sc_pipeline(tab_hbm, idx_v, rows, sems, n_chunks, compute)
        pltpu.sync_copy(o_v, out_hbm.at[pl.ds(t0, SC_TOK)])
        return carry

    lax.fori_loop(0, tok_per_worker // SC_TOK, batch, 0)


def _peer_combine(h2, idx4, coef_flat, v_tab):
    t, d = h2.shape
    nk = PEER_HEADS * PEER_TOPK
    mesh = plsc.VectorSubcoreMesh(core_axis_name="c", subcore_axis_name="s")
    body = functools.partial(_peer_combine_body, tok_per_worker=t // SC_WORKERS)
    return pl.kernel(
        body,
        out_type=jax.ShapeDtypeStruct((t, d), F32),
        mesh=mesh,
        scratch_types=[
            pltpu.VMEM((SC_TOK, d), F32),
            pltpu.VMEM((SC_TOK * nk // SC_ROWS, SC_ROWS), I32),
            pltpu.VMEM((SC_TOK * nk,), F32),
        ] + [pltpu.VMEM((SC_ROWS, d), F32)] * SC_BUFS + [pltpu.SemaphoreType.DMA] * SC_BUFS,
        compiler_params=_sc_params(),
        cost_estimate=pl.CostEstimate(flops=2 * t * nk * d, transcendentals=0, bytes_accessed=4 * t * nk * d + 8 * t * d),
        name="peer_combine_sc",
    )(h2, idx4, coef_flat, v_tab)


def _gelu_via_exp(x):
    c = math.sqrt(2.0 / math.pi)
    z = c * (x + 0.044715 * (x * x * x))
    return 0.5 * x * (2.0 - 2.0 / (1.0 + jnp.exp(2.0 * z)))


def _peer_experts_body(hn_hbm, h_hbm, idx_hbm, gate_hbm, u_hbm, v_hbm, out_hbm,
                       h_v, o_v, idx_v, gate_v, coef_v, acc_v, *bufs, tok_per_worker):
    rows, sems = bufs[:SC_BUFS], bufs[SC_BUFS:]
    d = hn_hbm.shape[1]
    nk = PEER_HEADS * PEER_TOPK
    per_tok = nk // SC_ROWS
    n = SC_TOK * per_tok
    n_reg = SC_COLS // SC_LANES
    lane = lax.iota(I32, SC_LANES)
    wid = _sc_worker_id()

    def gather_u(k, slot):
        return pltpu.make_async_copy(u_hbm.at[idx_v.at[k]], rows[slot], sems[slot])

    def gather_v(k, slot):
        return pltpu.make_async_copy(v_hbm.at[idx_v.at[k]], rows[slot], sems[slot])

    def start_ahead(ahead, slot):
        @pl.when(ahead < n)
        def _():
            gather_u(ahead, slot).start()

        @pl.when(jnp.logical_and(ahead >= n, ahead < 2 * n))
        def _():
            gather_v(ahead - n, slot).start()

    def score(rw, k):
        t = k // per_tok
        for cc in range(d // SC_COLS):
            hv = [h_v[t, pl.ds(cc * SC_COLS + SC_LANES * j, SC_LANES)] for j in range(n_reg)]

            @plsc.parallel_loop(0, SC_ROWS, 1)
            def _(r):
                p = _tree_sum([rw[r, pl.ds(cc * SC_COLS + SC_LANES * j, SC_LANES)] * hv[j] for j in range(n_reg)])
                off = pl.multiple_of(r * SC_LANES, SC_LANES)
                if cc == 0:
                    acc_v[pl.ds(off, SC_LANES)] = p
                else:
                    acc_v[pl.ds(off, SC_LANES)] = acc_v[pl.ds(off, SC_LANES)] + p

        tot = _tree_sum([plsc.load_gather(acc_v, [lane * SC_LANES + j]) for j in range(SC_LANES)])
        coef_v[pl.ds(pl.multiple_of(k * SC_ROWS, SC_LANES), SC_LANES)] = tot

    def combine(rw, k):
        t = k // per_tok
        for cc in range(d // SC_COLS):
            acc0 = tuple(o_v[t, pl.ds(cc * SC_COLS + SC_LANES * j, SC_LANES)] for j in range(n_reg))

            def rbody(r, acc):
                w = plsc.load_gather(coef_v, [jnp.full((SC_LANES,), k * SC_ROWS + r, I32)])
                return tuple(acc[j] + w * rw[r, pl.ds(cc * SC_COLS + SC_LANES * j, SC_LANES)]
                             for j in range(n_reg))

            acc = lax.fori_loop(0, SC_ROWS, rbody, acc0)
            for j in range(n_reg):
                o_v[t, pl.ds(cc * SC_COLS + SC_LANES * j, SC_LANES)] = acc[j]

    def batch(bi, carry):
        t0 = pl.multiple_of(wid * tok_per_worker + bi * SC_TOK, SC_TOK)
        pltpu.sync_copy(idx_hbm.at[pl.ds(t0 * per_tok, n)], idx_v)
        for s in range(SC_BUFS - 1):
            gather_u(s, s).start()
        pltpu.sync_copy(hn_hbm.at[pl.ds(t0, SC_TOK)], h_v)
        pltpu.sync_copy(h_hbm.at[pl.ds(t0, SC_TOK)], o_v)
        pltpu.sync_copy(gate_hbm.at[pl.ds(t0 * nk, SC_TOK * nk)], gate_v)

        def score_group(i, c):
            for s in range(SC_BUFS):
                k = i * SC_BUFS + s
                start_ahead(k + SC_BUFS - 1, (s + SC_BUFS - 1) % SC_BUFS)
                gather_u(k, s).wait()
                score(rows[s], k)
            return c

        lax.fori_loop(0, n // SC_BUFS, score_group, 0)

        @plsc.parallel_loop(0, SC_TOK * nk // SC_LANES, 1)
        def _(j):
            off = pl.multiple_of(j * SC_LANES, SC_LANES)
            coef_v[pl.ds(off, SC_LANES)] = _gelu_via_exp(coef_v[pl.ds(off, SC_LANES)]) * gate_v[pl.ds(off, SC_LANES)]

        def combine_group(i, c):
            for s in range(SC_BUFS):
                k = i * SC_BUFS + s
                start_ahead(n + k + SC_BUFS - 1, (s + SC_BUFS - 1) % SC_BUFS)
                gather_v(k, s).wait()
                combine(rows[s], k)
            return c

        lax.fori_loop(0, n // SC_BUFS, combine_group, 0)
        pltpu.sync_copy(o_v, out_hbm.at[pl.ds(t0, SC_TOK)])
        return carry

    lax.fori_loop(0, tok_per_worker // SC_TOK, batch, 0)


def _peer_experts(hn, h2, idx4, gate_flat, u_tab, v_tab):
    t, d = h2.shape
    nk = PEER_HEADS * PEER_TOPK
    mesh = plsc.VectorSubcoreMesh(core_axis_name="c", subcore_axis_name="s")
    body = functools.partial(_peer_experts_body, tok_per_worker=t // SC_WORKERS)
    return pl.kernel(
        body,
        out_type=jax.ShapeDtypeStruct((t, d), F32),
        mesh=mesh,
        scratch_types=[
            pltpu.VMEM((SC_TOK, d), F32),
            pltpu.VMEM((SC_TOK, d), F32),
            pltpu.VMEM((SC_TOK * nk // SC_ROWS, SC_ROWS), I32),
            pltpu.VMEM((SC_TOK * nk,), F32),
            pltpu.VMEM((SC_TOK * nk,), F32),
            pltpu.VMEM((SC_ROWS * SC_LANES,), F32),
        ] + [pltpu.VMEM((SC_ROWS, d), F32)] * SC_BUFS + [pltpu.SemaphoreType.DMA] * SC_BUFS,
        compiler_params=_sc_params(),
        cost_estimate=pl.CostEstimate(flops=4 * t * nk * d, transcendentals=t * nk,
                                      bytes_accessed=8 * t * nk * d + 12 * t * d),
        name="peer_experts_sc",
    )(hn, h2, idx4, gate_flat, u_tab, v_tab)


def _mixers(h, l, norm1_g, w_in, ssm, fox_b_f, g_ssm_out, g_attn_out, w_o, ssm_w_glu, ssm_b_glu, tm, blk):
    b, s, d = h.shape
    w = N_HEADS * HEAD_DIM
    wl = w_in[l]
    wm = wl[:, :4 * w].astype(BF16)
    wf = jnp.pad(wl[:, 4 * w:], ((0, 0), (0, 128 - N_HEADS)))
    bf = jnp.pad(fox_b_f[l], (0, 128 - N_HEADS)).reshape(1, 128)
    u, qa, ka, v = _in_proj(h, norm1_g[l].reshape(1, d), wm, wf, bf, tm)

    lc = SSM_CHUNK
    nc = s // lc
    g = w // SSM_GROUP_CH
    ug = u.reshape(b, nc, lc, g, SSM_GROUP_CH).transpose(3, 1, 0, 2, 4).reshape(g, nc * b, lc * SSM_GROUP_CH)
    yg = _ssm(ug, *ssm, n_chunks=nc, batch=b)
    y_ssm = yg.reshape(g, nc, b, lc, SSM_GROUP_CH).transpose(2, 1, 3, 0, 4).reshape(b, s, w)

    y_att = _attention(qa.reshape(b * N_HEADS, s, AUG), ka.reshape(b * N_HEADS, s, AUG),
                       v.reshape(b * N_HEADS, s, HEAD_DIM), blk).reshape(b, N_HEADS, s, HEAD_DIM)

    wo = w_o[l].astype(BF16)
    return _out_proj(y_ssm, y_att, h, ssm_w_glu[l].astype(BF16), ssm_b_glu[l].reshape(1, w),
                     g_ssm_out[l].reshape(1, w), g_attn_out[l].reshape(N_HEADS, 1, HEAD_DIM),
                     wo[:w], wo[w:].reshape(N_HEADS, HEAD_DIM, d), tm)


def _keys_cat(keys_l):
    z = jnp.zeros_like(keys_l[:, 0])
    top = jnp.concatenate([keys_l[:, 0], z], axis=-1)
    bot = jnp.concatenate([z, keys_l[:, 1]], axis=-1)
    return jnp.concatenate([top, bot], axis=1).astype(BF16)


def kernel(x, norm1_g, w_in, ssm_lambda_re, ssm_lambda_im, ssm_log_dt, ssm_b_re, ssm_b_im, ssm_c_re, ssm_c_im, ssm_d, ssm_w_glu, ssm_b_glu, fox_b_f, g_ssm_out, g_attn_out, w_o, norm2_g, peer_w_q, peer_keys, peer_u, peer_v, norm_f):
    b, s, d = x.shape
    depth = w_in.shape[0]
    nk = PEER_HEADS * PEER_TOPK
    tm = min(512, s)
    blk = min(256, s)
    assert b % 2 == 0
    bh = b // 2
    t = bh * s
    ssm_tabs = [_ssm_tables(ssm_lambda_re[l], ssm_lambda_im[l], ssm_log_dt[l], ssm_b_re[l], ssm_b_im[l],
                            ssm_c_re[l], ssm_c_im[l], ssm_d[l]) for l in range(depth)]

    def dense_stage(h, l):
        h = _mixers(h, l, norm1_g, w_in, ssm_tabs[l], fox_b_f, g_ssm_out, g_attn_out, w_o, ssm_w_glu, ssm_b_glu,
                    tm, blk)
        h2 = h.reshape(t, d)
        hn, idx, gate = _route(h2, norm2_g[l].reshape(1, d), peer_w_q[l].astype(BF16), _keys_cat(peer_keys[l]),
                               min(256, t))
        return h2, hn, idx.reshape(t * nk // SC_ROWS, SC_ROWS), gate.reshape(t * nk)

    def expert_stage(st, l):
        h2, hn, idx4, gate = st
        return _peer_experts(hn, h2, idx4, gate, peer_u[l], peer_v[l]).reshape(bh, s, d)

    def after(value, st):
        value, gate = lax.optimization_barrier((value, st[3]))
        return value, st[:3] + (gate,)

    st_a = dense_stage(x[:bh], 0)
    h_b, st_a = after(x[bh:], st_a)
    for l in range(depth):
        h_a = expert_stage(st_a, l)
        st_b = dense_stage(h_b, l)
        h_a, st_b = after(h_a, st_b)
        h_b = expert_stage(st_b, l)
        if l + 1 < depth:
            st_a = dense_stage(h_a, l + 1)
            h_b, st_a = after(h_b, st_a)
    outs = [_final_norm(h.reshape(t, d), norm_f.reshape(1, d), min(512, t)).reshape(bh, s, d) for h in (h_a, h_b)]
    return jnp.concatenate(outs, axis=0)
```

```python
import functools
import math

import jax
import jax.numpy as jnp
from jax import lax
from jax.experimental import pallas as pl
from jax.experimental.pallas import tpu as pltpu
from jax.experimental.pallas import tpu_sc as plsc

F32 = jnp.float32
BF16 = jnp.bfloat16
I32 = jnp.int32

RMS_EPS = 1e-6
SSM_GROUP_CH = 16
SSM_STATE = 64
SSM_CHUNK = 64
HEAD_DIM = 64
N_HEADS = 8
AUG = 128
PEER_HEADS = 8
PEER_KEYS = 128
PEER_TOPK = 16
PEER_HALF = 64
VMEM_LIMIT = 56 * 1024 * 1024


def _rms(x, g):
    return x * lax.rsqrt(jnp.mean(x * x, axis=-1, keepdims=True) + RMS_EPS) * g


def _gelu(x):
    c = math.sqrt(2.0 / math.pi)
    return 0.5 * x * (1.0 + jnp.tanh(c * (x + 0.044715 * (x * x * x))))


def _sigmoid(x):
    return 1.0 / (1.0 + jnp.exp(-x))


def _in_proj_kernel(h_ref, g_ref, wm_ref, wf_ref, bf_ref, u_ref, qa_ref, ka_ref, v_ref, cum_ref):
    j = pl.program_id(1)

    @pl.when(j == 0)
    def _():
        cum_ref[...] = jnp.zeros_like(cum_ref)

    x = h_ref[0]
    tm = x.shape[0]
    xn = _rms(x, g_ref[...])
    proj = jnp.dot(xn.astype(BF16), wm_ref[...], preferred_element_type=F32)
    f = jnp.dot(xn, wf_ref[...], precision=lax.Precision.HIGHEST, preferred_element_type=F32) + bf_ref[...]
    logf = jnp.minimum(f, 0.0) - jnp.log(1.0 + jnp.exp(-jnp.abs(f)))
    row = lax.broadcasted_iota(I32, (tm, tm), 0)
    col = lax.broadcasted_iota(I32, (tm, tm), 1)
    tri = (row >= col).astype(F32)
    cum = jnp.dot(tri, logf, precision=lax.Precision.HIGHEST, preferred_element_type=F32) + cum_ref[0:1, :]
    cum_ref[0:1, :] = cum[tm - 1:tm, :]

    w = HEAD_DIM * N_HEADS
    u_ref[0] = proj[:, :w].astype(BF16)
    lane = lax.broadcasted_iota(I32, (tm, AUG), 1)
    scale = HEAD_DIM ** -0.5
    for hh in range(N_HEADS):
        pair = hh // 2
        q2 = proj[:, w + 128 * pair: w + 128 * pair + 128]
        k2 = proj[:, 2 * w + 128 * pair: 2 * w + 128 * pair + 128]
        v2 = proj[:, 3 * w + 128 * pair: 3 * w + 128 * pair + 128]
        if hh % 2 == 1:
            q2 = pltpu.roll(q2, 64, axis=1)
            k2 = pltpu.roll(k2, 64, axis=1)
            vh = v2[:, 64:]
        else:
            vh = v2[:, :64]
        c = jnp.broadcast_to(cum[:, hh:hh + 1], (tm, AUG))
        c1 = c.astype(BF16).astype(F32)
        r1 = c - c1
        c2 = r1.astype(BF16).astype(F32)
        c3 = r1 - c2
        one = jnp.ones((tm, AUG), F32)
        zero = jnp.zeros((tm, AUG), F32)
        qa = jnp.where(lane < 64, q2 * scale,
             jnp.where(lane == 64, c1, jnp.where(lane == 65, c2, jnp.where(lane == 66, c3,
             jnp.where(lane < 70, one, zero)))))
        ka = jnp.where(lane < 64, k2,
             jnp.where(lane < 67, one, jnp.where(lane == 67, -c1, jnp.where(lane == 68, -c2,
             jnp.where(lane == 69, -c3, zero)))))
        qa_ref[0, hh] = qa.astype(BF16)
        ka_ref[0, hh] = ka.astype(BF16)
        v_ref[0, hh] = vh.astype(BF16)


def _in_proj(h, g, wm, wf, bf, tm):
    b, s, d = h.shape
    w = HEAD_DIM * N_HEADS
    return pl.pallas_call(
        _in_proj_kernel,
        grid=(b, s // tm),
        in_specs=[
            pl.BlockSpec((1, tm, d), lambda i, j: (i, j, 0)),
            pl.BlockSpec((1, d), lambda i, j: (0, 0)),
            pl.BlockSpec((d, 4 * w), lambda i, j: (0, 0)),
            pl.BlockSpec((d, 128), lambda i, j: (0, 0)),
            pl.BlockSpec((1, 128), lambda i, j: (0, 0)),
        ],
        out_specs=[
            pl.BlockSpec((1, tm, w), lambda i, j: (i, j, 0)),
            pl.BlockSpec((1, N_HEADS, tm, AUG), lambda i, j: (i, 0, j, 0)),
            pl.BlockSpec((1, N_HEADS, tm, AUG), lambda i, j: (i, 0, j, 0)),
            pl.BlockSpec((1, N_HEADS, tm, HEAD_DIM), lambda i, j: (i, 0, j, 0)),
        ],
        out_shape=[
            jax.ShapeDtypeStruct((b, s, w), BF16),
            jax.ShapeDtypeStruct((b, N_HEADS, s, AUG), BF16),
            jax.ShapeDtypeStruct((b, N_HEADS, s, AUG), BF16),
            jax.ShapeDtypeStruct((b, N_HEADS, s, HEAD_DIM), BF16),
        ],
        scratch_shapes=[pltpu.VMEM((8, 128), F32)],
        compiler_params=pltpu.CompilerParams(
            dimension_semantics=("parallel", "arbitrary"), vmem_limit_bytes=VMEM_LIMIT),
        cost_estimate=pl.CostEstimate(flops=2 * b * s * d * (4 * w + 128), transcendentals=2 * b * s * 128,
                                      bytes_accessed=4 * b * s * d + 2 * d * (4 * w) + 2 * b * s * (w + 2 * N_HEADS * AUG + w)),
        name="in_proj",
    )(h, g, wm, wf, bf)


def _ssm_kernel(u_ref, m_ref, w_ref, r_ref, a_ref, d_ref, y_ref, e_scr, *, n_chunks, batch):
    u = u_ref[0]
    y = jnp.dot(u, m_ref[0], preferred_element_type=F32)
    e_scr[...] = jnp.dot(u, w_ref[0], preferred_element_type=F32)
    ar = a_ref[0, 0:1, :]
    ai = a_ref[0, 1:2, :]

    def body(c, s):
        off = pl.multiple_of(c * batch, batch)
        e_c = e_scr[pl.ds(off, batch), :]
        e_scr[pl.ds(off, batch), :] = s
        return ar * s + ai * pltpu.roll(s, SSM_STATE, axis=1) + e_c

    lax.fori_loop(0, n_chunks, body, jnp.zeros((batch, 2 * SSM_STATE), F32))
    y = y + jnp.dot(e_scr[...].astype(BF16), r_ref[0], preferred_element_type=F32)
    y_ref[0] = y + u.astype(F32) * d_ref[0]


def _ssm(ug, m, w, r, a, d, n_chunks, batch):
    g, rows, width = ug.shape
    kern = functools.partial(_ssm_kernel, n_chunks=n_chunks, batch=batch)
    return pl.pallas_call(
        kern,
        grid=(g,),
        in_specs=[
            pl.BlockSpec((1, rows, width), lambda i: (i, 0, 0)),
            pl.BlockSpec((1, width, width), lambda i: (i, 0, 0)),
            pl.BlockSpec((1, width, 2 * SSM_STATE), lambda i: (i, 0, 0)),
            pl.BlockSpec((1, 2 * SSM_STATE, width), lambda i: (i, 0, 0)),
            pl.BlockSpec((1, 8, 2 * SSM_STATE), lambda i: (i, 0, 0)),
            pl.BlockSpec((1, 1, width), lambda i: (i, 0, 0)),
        ],
        out_specs=pl.BlockSpec((1, rows, width), lambda i: (i, 0, 0)),
        out_shape=jax.ShapeDtypeStruct((g, rows, width), F32),
        scratch_shapes=[pltpu.VMEM((rows, 2 * SSM_STATE), F32)],
        compiler_params=pltpu.CompilerParams(
            dimension_semantics=("parallel",), vmem_limit_bytes=VMEM_LIMIT),
        cost_estimate=pl.CostEstimate(flops=2 * g * rows * width * (width + 4 * SSM_STATE), transcendentals=0,
                                      bytes_accessed=g * (2 * width * (width + 4 * SSM_STATE) + 6 * rows * width)),
        name="ssm",
    )(ug, m, w, r, a, d)


def _ssm_tables(lam_re, lam_im, log_dt, b_re, b_im, c_re, c_im, d_skip):
    hp = lax.Precision.HIGHEST
    lc = SSM_CHUNK
    g, p = lam_re.shape
    hc = SSM_GROUP_CH
    dt = jnp.exp(log_dt)[:, None]
    a_re = jnp.exp(lam_re * dt) * jnp.cos(lam_im * dt)
    a_im = jnp.exp(lam_re * dt) * jnp.sin(lam_im * dt)
    den = lam_re * lam_re + lam_im * lam_im
    nr = a_re - 1.0
    z_re = (nr * lam_re + a_im * lam_im) / den
    z_im = (a_im * lam_re - nr * lam_im) / den
    bb_re = z_re[..., None] * b_re - z_im[..., None] * b_im
    bb_im = z_re[..., None] * b_im + z_im[..., None] * b_re
    tau = jnp.arange(lc + 1, dtype=F32)[:, None, None]
    mag = jnp.exp(tau * (lam_re * dt)[None])
    ang = tau * (lam_im * dt)[None]
    p_re = mag * jnp.cos(ang)
    p_im = mag * jnp.sin(ang)
    ab_re = p_re[:lc, :, :, None] * bb_re[None] - p_im[:lc, :, :, None] * bb_im[None]
    ab_im = p_re[:lc, :, :, None] * bb_im[None] + p_im[:lc, :, :, None] * bb_re[None]
    kk = (jnp.einsum('ghp,tgpk->tghk', c_re, ab_re, precision=hp)
          - jnp.einsum('ghp,tgpk->tghk', c_im, ab_im, precision=hp))
    t_idx = jnp.arange(lc)
    lag = t_idx[None, :] - t_idx[:, None]
    toe = jnp.where((lag >= 0)[:, :, None, None, None], kk[jnp.clip(lag, 0, lc - 1)], 0.0)
    m = toe.transpose(2, 0, 4, 1, 3).reshape(g, lc * hc, lc * hc)
    w_re = ab_re[::-1].transpose(1, 0, 3, 2).reshape(g, lc * hc, p)
    w_im = ab_im[::-1].transpose(1, 0, 3, 2).reshape(g, lc * hc, p)
    w = jnp.concatenate([w_re, w_im], axis=-1)
    q_re = p_re[1:, :, None, :] * c_re[None] - p_im[1:, :, None, :] * c_im[None]
    q_im = p_re[1:, :, None, :] * c_im[None] + p_im[1:, :, None, :] * c_re[None]
    r = jnp.concatenate([q_re.transpose(1, 3, 0, 2).reshape(g, p, lc * hc),
                         -q_im.transpose(1, 3, 0, 2).reshape(g, p, lc * hc)], axis=1)
    al_re, al_im = p_re[lc], p_im[lc]
    a = jnp.zeros((g, 8, 2 * p), F32)
    a = a.at[:, 0, :].set(jnp.concatenate([al_re, al_re], axis=-1))
    a = a.at[:, 1, :].set(jnp.concatenate([-al_im, al_im], axis=-1))
    d = jnp.tile(d_skip[:, None, :], (1, lc, 1)).reshape(g, 1, lc * hc)
    return m.astype(BF16), w.astype(BF16), r.astype(BF16), a, d


def _attn_kernel(q_ref, k_ref, v_ref, o_ref, m_ref, l_ref, acc_ref, *, blk):
    i = pl.program_id(1)
    q = q_ref[0]
    m_ref[...] = jnp.full_like(m_ref, -jnp.inf)
    l_ref[...] = jnp.zeros_like(l_ref)
    acc_ref[...] = jnp.zeros_like(acc_ref)

    def step(j, masked):
        off = pl.multiple_of(j * blk, blk)
        k = k_ref[0, pl.ds(off, blk), :]
        v = v_ref[0, pl.ds(off, blk), :]
        s = lax.dot_general(q, k, (((1,), (1,)), ((), ())), preferred_element_type=F32)
        if masked:
            row = lax.broadcasted_iota(I32, s.shape, 0)
            col = lax.broadcasted_iota(I32, s.shape, 1)
            s = jnp.where(row >= col, s, -jnp.inf)
        m_prev = m_ref[...]
        m_new = jnp.maximum(m_prev, jnp.max(s, axis=1, keepdims=True))
        p = jnp.exp(s - m_new)
        alpha = jnp.exp(m_prev - m_new)
        l_ref[...] = alpha * l_ref[...] + jnp.sum(p, axis=1, keepdims=True)
        acc_ref[...] = alpha * acc_ref[...] + jnp.dot(p.astype(BF16), v, preferred_element_type=F32)
        m_ref[...] = m_new

    def body(j, c):
        step(j, False)
        return c

    lax.fori_loop(0, i, body, 0)
    step(i, True)
    o_ref[0] = acc_ref[...] / l_ref[...]


def _attention(qa, ka, v, blk):
    bh, s, _ = qa.shape
    kern = functools.partial(_attn_kernel, blk=blk)
    return pl.pallas_call(
        kern,
        grid=(bh, s // blk),
        in_specs=[
            pl.BlockSpec((1, blk, AUG), lambda b, i: (b, i, 0)),
            pl.BlockSpec((1, s, AUG), lambda b, i: (b, 0, 0)),
            pl.BlockSpec((1, s, HEAD_DIM), lambda b, i: (b, 0, 0)),
        ],
        out_specs=pl.BlockSpec((1, blk, HEAD_DIM), lambda b, i: (b, i, 0)),
        out_shape=jax.ShapeDtypeStruct((bh, s, HEAD_DIM), F32),
        scratch_shapes=[pltpu.VMEM((blk, 1), F32), pltpu.VMEM((blk, 1), F32), pltpu.VMEM((blk, HEAD_DIM), F32)],
        compiler_params=pltpu.CompilerParams(
            dimension_semantics=("parallel", "arbitrary"), vmem_limit_bytes=VMEM_LIMIT),
        cost_estimate=pl.CostEstimate(flops=bh * s * s * (AUG + HEAD_DIM), transcendentals=bh * s * s // 2,
                                      bytes_accessed=bh * s * (2 * 2 * AUG + 2 * HEAD_DIM + 4 * HEAD_DIM)),
        name="fox_attn",
    )(qa, ka, v)


def _out_proj_kernel(y_ref, att_ref, h_ref, wg_ref, bg_ref, gs_ref, ga_ref, wos_ref, woa_ref, o_ref):
    g = _gelu(y_ref[0])
    z = jnp.dot(g.astype(BF16), wg_ref[...], preferred_element_type=F32) + bg_ref[...]
    o = g * _sigmoid(z)
    a = _rms(o, gs_ref[...])
    acc = h_ref[0] + jnp.dot(a.astype(BF16), wos_ref[...], preferred_element_type=F32)
    ssq = jnp.zeros((o.shape[0], 1), F32)
    for hh in range(N_HEADS):
        t = att_ref[0, hh]
        ssq = ssq + jnp.sum(t * t, axis=1, keepdims=True)
    inv = lax.rsqrt(ssq / (N_HEADS * HEAD_DIM) + RMS_EPS)
    for hh in range(N_HEADS):
        bh = att_ref[0, hh] * inv * ga_ref[hh]
        acc = acc + jnp.dot(bh.astype(BF16), woa_ref[hh], preferred_element_type=F32)
    o_ref[0] = acc


def _out_proj(y_ssm, y_att, h, wg, bg, gs, ga, wos, woa, tm):
    b, s, d = h.shape
    w = y_ssm.shape[-1]
    return pl.pallas_call(
        _out_proj_kernel,
        grid=(b, s // tm),
        in_specs=[
            pl.BlockSpec((1, tm, w), lambda i, j: (i, j, 0)),
            pl.BlockSpec((1, N_HEADS, tm, HEAD_DIM), lambda i, j: (i, 0, j, 0)),
            pl.BlockSpec((1, tm, d), lambda i, j: (i, j, 0)),
            pl.BlockSpec((w, w), lambda i, j: (0, 0)),
            pl.BlockSpec((1, w), lambda i, j: (0, 0)),
            pl.BlockSpec((1, w), lambda i, j: (0, 0)),
            pl.BlockSpec((N_HEADS, 1, HEAD_DIM), lambda i, j: (0, 0, 0)),
            pl.BlockSpec((w, d), lambda i, j: (0, 0)),
            pl.BlockSpec((N_HEADS, HEAD_DIM, d), lambda i, j: (0, 0, 0)),
        ],
        out_specs=pl.BlockSpec((1, tm, d), lambda i, j: (i, j, 0)),
        out_shape=jax.ShapeDtypeStruct((b, s, d), F32),
        compiler_params=pltpu.CompilerParams(
            dimension_semantics=("parallel", "parallel"), vmem_limit_bytes=VMEM_LIMIT),
        cost_estimate=pl.CostEstimate(flops=2 * b * s * (w * w + 2 * w * d), transcendentals=2 * b * s * w,
                                      bytes_accessed=b * s * (8 * w + 8 * d) + 2 * (w * w + 2 * w * d)),
        name="out_proj",
    )(y_ssm, y_att, h, wg, bg, gs, ga, wos, woa)


def _take_top(vals, payload, k):
    n_rows = vals.shape[0]
    rows = lax.broadcasted_iota(I32, vals.shape, 0)
    tops, picks = [], []
    for _ in range(k):
        m = jnp.max(vals, axis=0, keepdims=True)
        arg = jnp.min(jnp.where(vals == m, rows, n_rows), axis=0, keepdims=True)
        hit = rows == arg
        tops.append(m)
        picks.append(arg if payload is None else jnp.max(jnp.where(hit, payload, -1), axis=0, keepdims=True))
        vals = jnp.where(hit, -jnp.inf, vals)
    return jnp.concatenate(tops, axis=0), jnp.concatenate(picks, axis=0)


def _route_kernel(h_ref, g_ref, wq_ref, keys_ref, hn_ref, idx_ref, gate_ref):
    x = h_ref[...]
    hn = _rms(x, g_ref[...])
    hn_ref[...] = _pack_bf16_pairs(hn)
    q = jnp.dot(hn.astype(BF16), wq_ref[...], preferred_element_type=F32)
    k = PEER_TOPK
    idx_rows, gate_rows = [], []
    for hh in range(PEER_HEADS):
        qh = q[:, 128 * hh: 128 * (hh + 1)].astype(BF16)
        sc = lax.dot_general(keys_ref[hh], qh, (((1,), (1,)), ((), ())), preferred_element_type=F32)
        t1, i1 = _take_top(sc[:PEER_KEYS], None, k)
        t2, i2 = _take_top(sc[PEER_KEYS:], None, k)
        cand = jnp.concatenate([t1[a:a + 1] + t2 for a in range(k)], axis=0)
        cidx = jnp.concatenate([i1[a:a + 1] * PEER_KEYS + i2 for a in range(k)], axis=0)
        best, idx = _take_top(cand, cidx, k)
        e = jnp.exp(best - best[0:1])
        gate = e / jnp.sum(e, axis=0, keepdims=True)
        idx_rows.append(idx)
        gate_rows.append(gate)
    idx_ref[...] = jnp.concatenate(idx_rows, axis=0).T
    gate_ref[...] = jnp.concatenate(gate_rows, axis=0).T


def _route(h2, g, wq, keys_cat, tm):
    t, d = h2.shape
    nk = PEER_HEADS * PEER_TOPK
    return pl.pallas_call(
        _route_kernel,
        grid=(t // tm,),
        in_specs=[
            pl.BlockSpec((tm, d), lambda i: (i, 0)),
            pl.BlockSpec((1, d), lambda i: (0, 0)),
            pl.BlockSpec((d, PEER_HEADS * 128), lambda i: (0, 0)),
            pl.BlockSpec((PEER_HEADS, 2 * PEER_KEYS, 128), lambda i: (0, 0, 0)),
        ],
        out_specs=[
            pl.BlockSpec((tm, d // 2), lambda i: (i, 0)),
            pl.BlockSpec((tm, nk), lambda i: (i, 0)),
            pl.BlockSpec((tm, nk), lambda i: (i, 0)),
        ],
        out_shape=[
            jax.ShapeDtypeStruct((t, d // 2), I32),
            jax.ShapeDtypeStruct((t, nk), I32),
            jax.ShapeDtypeStruct((t, nk), F32),
        ],
        compiler_params=pltpu.CompilerParams(
            dimension_semantics=("parallel",), vmem_limit_bytes=VMEM_LIMIT),
        cost_estimate=pl.CostEstimate(flops=2 * t * d * PEER_HEADS * 128 + 2 * t * PEER_HEADS * 256 * 128,
                                      transcendentals=t * nk, bytes_accessed=t * (8 * d + 8 * nk) + 2 * d * PEER_HEADS * 128),
        name="peer_route",
    )(h2, g, wq, keys_cat)


def _final_norm_kernel(x_ref, g_ref, o_ref):
    o_ref[...] = _rms(x_ref[...], g_ref[...])


def _final_norm(x2, g, tm):
    t, d = x2.shape
    return pl.pallas_call(
        _final_norm_kernel,
        grid=(t // tm,),
        in_specs=[pl.BlockSpec((tm, d), lambda i: (i, 0)), pl.BlockSpec((1, d), lambda i: (0, 0))],
        out_specs=pl.BlockSpec((tm, d), lambda i: (i, 0)),
        out_shape=jax.ShapeDtypeStruct((t, d), F32),
        compiler_params=pltpu.CompilerParams(dimension_semantics=("parallel",)),
        cost_estimate=pl.CostEstimate(flops=4 * t * d, transcendentals=t, bytes_accessed=8 * t * d),
        name="final_norm",
    )(x2, g)


SC_WORKERS = 32
SC_CORES = 2
SC_LANES = 16
SC_TOK = 16
SC_ROWS = 16
SC_BUFS = 4
SC_COLS = 256
SC_SLAB = 8


def _sc_params():
    cp = pltpu.CompilerParams()
    if "needs_layout_passes" in pltpu.CompilerParams.__dataclass_fields__:
        cp = pltpu.CompilerParams(needs_layout_passes=False)
    return cp


def _sc_worker_id():
    return lax.axis_index("s") * SC_CORES + lax.axis_index("c")


def _tree_sum(xs):
    xs = list(xs)
    while len(xs) > 1:
        xs = [xs[i] + xs[i + 1] for i in range(0, len(xs) - 1, 2)] + ([xs[-1]] if len(xs) % 2 else [])
    return xs[0]


def _peer_scores_body(hn_hbm, idx_hbm, tab_hbm, out_hbm, h_v, idx_v, acc_v, act_v, *bufs, tok_per_worker):
    rows, sems = bufs[:SC_BUFS], bufs[SC_BUFS:]
    d = hn_hbm.shape[1]
    per_tok = PEER_HEADS * PEER_TOPK // SC_ROWS
    n_chunks = SC_TOK * per_tok
    n_reg = SC_COLS // SC_LANES
    lane = lax.iota(I32, SC_LANES)
    wid = _sc_worker_id()

    def compute(rw, k):
        t = k // per_tok
        c = k % per_tok
        for cc in range(d // SC_COLS):
            hv = [h_v[t, pl.ds(cc * SC_COLS + SC_LANES * j, SC_LANES)] for j in range(n_reg)]

            @plsc.parallel_loop(0, SC_ROWS, 1)
            def _(r):
                p = _tree_sum([rw[r, pl.ds(cc * SC_COLS + SC_LANES * j, SC_LANES)] * hv[j] for j in range(n_reg)])
                off = pl.multiple_of(r * SC_LANES, SC_LANES)
                if cc == 0:
                    acc_v[pl.ds(off, SC_LANES)] = p
                else:
                    acc_v[pl.ds(off, SC_LANES)] = acc_v[pl.ds(off, SC_LANES)] + p

        tot = _tree_sum([plsc.load_gather(acc_v, [lane * SC_LANES + j]) for j in range(SC_LANES)])
        off = pl.multiple_of(c * SC_ROWS, SC_LANES)
        act_v[t, pl.ds(off, SC_LANES)] = tot

    def batch(bi, carry):
        t0 = pl.multiple_of(wid * tok_per_worker + bi * SC_TOK, SC_TOK)
        pltpu.sync_copy(hn_hbm.at[pl.ds(t0, SC_TOK)], h_v)
        pltpu.sync_copy(idx_hbm.at[pl.ds(t0 * per_tok, n_chunks)], idx_v)
        _sc_pipeline(tab_hbm, idx_v, rows, sems, n_chunks, compute)
        pltpu.sync_copy(act_v, out_hbm.at[pl.ds(t0, SC_TOK)])
        return carry

    lax.fori_loop(0, tok_per_worker // SC_TOK, batch, 0)


def _peer_scores(hn, idx4, u_tab):
    t, d = hn.shape
    nk = PEER_HEADS * PEER_TOPK
    mesh = plsc.VectorSubcoreMesh(core_axis_name="c", subcore_axis_name="s")
    body = functools.partial(_peer_scores_body, tok_per_worker=t // SC_WORKERS)
    return pl.kernel(
        body,
        out_type=jax.ShapeDtypeStruct((t, nk), F32),
        mesh=mesh,
        scratch_types=[
            pltpu.VMEM((SC_TOK, d), F32),
            pltpu.VMEM((SC_TOK * nk // SC_ROWS, SC_ROWS), I32),
            pltpu.VMEM((SC_ROWS * SC_LANES,), F32),
            pltpu.VMEM((SC_TOK, nk), F32),
        ] + [pltpu.VMEM((SC_ROWS, d), F32)] * SC_BUFS + [pltpu.SemaphoreType.DMA] * SC_BUFS,
        compiler_params=_sc_params(),
        cost_estimate=pl.CostEstimate(flops=2 * t * nk * d, transcendentals=0, bytes_accessed=4 * t * nk * d + 4 * t * d),
        name="peer_scores_sc",
    )(hn, idx4, u_tab)


def _peer_combine_body(h_hbm, idx_hbm, coef_hbm, tab_hbm, out_hbm, o_v, idx_v, coef_v, *bufs, tok_per_worker):
    rows, sems = bufs[:SC_BUFS], bufs[SC_BUFS:]
    d = h_hbm.shape[1]
    nk = PEER_HEADS * PEER_TOPK
    per_tok = nk // SC_ROWS
    n_chunks = SC_TOK * per_tok
    n_reg = SC_COLS // SC_LANES
    wid = _sc_worker_id()

    def compute(rw, k):
        t = k // per_tok
        for cc in range(d // SC_COLS):
            acc0 = tuple(o_v[t, pl.ds(cc * SC_COLS + SC_LANES * j, SC_LANES)] for j in range(n_reg))

            def rbody(r, acc):
                w = plsc.load_gather(coef_v, [jnp.full((SC_LANES,), k * SC_ROWS + r, I32)])
                return tuple(acc[j] + w * rw[r, pl.ds(cc * SC_COLS + SC_LANES * j, SC_LANES)]
                             for j in range(n_reg))

            acc = lax.fori_loop(0, SC_ROWS, rbody, acc0)
            for j in range(n_reg):
                o_v[t, pl.ds(cc * SC_COLS + SC_LANES * j, SC_LANES)] = acc[j]

    def batch(bi, carry):
        t0 = pl.multiple_of(wid * tok_per_worker + bi * SC_TOK, SC_TOK)
        pltpu.sync_copy(h_hbm.at[pl.ds(t0, SC_TOK)], o_v)
        pltpu.sync_copy(idx_hbm.at[pl.ds(t0 * per_tok, n_chunks)], idx_v)
        pltpu.sync_copy(coef_hbm.at[pl.ds(t0 * nk, SC_TOK * nk)], coef_v)
        _sc_pipeline(tab_hbm, idx_v, rows, sems, n_chunks, compute)
        pltpu.sync_copy(o_v, out_hbm.at[pl.ds(t0, SC_TOK)])
        return carry

    lax.fori_loop(0, tok_per_worker // SC_TOK, batch, 0)


def _peer_combine(h2, idx4, coef_flat, v_tab):
    t, d = h2.shape
    nk = PEER_HEADS * PEER_TOPK
    mesh = plsc.VectorSubcoreMesh(core_axis_name="c", subcore_axis_name="s")
    body = functools.partial(_peer_combine_body, tok_per_worker=t // SC_WORKERS)
    return pl.kernel(
        body,
        out_type=jax.ShapeDtypeStruct((t, d), F32),
        mesh=mesh,
        scratch_types=[
            pltpu.VMEM((SC_TOK, d), F32),
            pltpu.VMEM((SC_TOK * nk // SC_ROWS, SC_ROWS), I32),
            pltpu.VMEM((SC_TOK * nk,), F32),
        ] + [pltpu.VMEM((SC_ROWS, d), F32)] * SC_BUFS + [pltpu.SemaphoreType.DMA] * SC_BUFS,
        compiler_params=_sc_params(),
        cost_estimate=pl.CostEstimate(flops=2 * t * nk * d, transcendentals=0, bytes_accessed=4 * t * nk * d + 8 * t * d),
        name="peer_combine_sc",
    )(h2, idx4, coef_flat, v_tab)


def _gelu_via_exp(x):
    c = math.sqrt(2.0 / math.pi)
    z = c * (x + 0.044715 * (x * x * x))
    return 0.5 * x * (2.0 - 2.0 / (1.0 + jnp.exp(2.0 * z)))


HI16 = -65536


def _round_to_bf16_bits(bits):
    return bits + 0x7FFF + (lax.shift_right_logical(bits, jnp.int32(16)) & 1)


def _pack_bf16_pairs(x):
    w = x.shape[-1] // 2
    r = _round_to_bf16_bits(lax.bitcast_convert_type(x, I32))
    return lax.shift_right_logical(r[..., :w], jnp.int32(16)) | (r[..., w:] & HI16)


def _lo_f32(word):
    return lax.bitcast_convert_type(lax.shift_left(word, jnp.int32(16)), F32)


def _hi_f32(word):
    return lax.bitcast_convert_type(word & HI16, F32)


def _peer_experts_body(hn_hbm, h_hbm, idx_hbm, gate_hbm, u_hbm, v_hbm, out_hbm,
                       h_v, o_v, idx_v, gate_v, coef_v, cw_v, acc_v, *bufs, tok_per_worker):
    rows, sems = bufs[:SC_BUFS], bufs[SC_BUFS:]
    half = hn_hbm.shape[1]
    nk = PEER_HEADS * PEER_TOPK
    per_tok = nk // SC_ROWS
    n = SC_TOK * per_tok
    n_slab = half // (SC_SLAB * SC_LANES)
    lane = lax.iota(I32, SC_LANES)
    wid = _sc_worker_id()

    def packed_mul(word, other_bf16):
        return plsc.bitcast(plsc.bitcast(word, BF16) * other_bf16, I32)

    def gather_u(k, slot):
        return pltpu.make_async_copy(u_hbm.at[idx_v.at[k]], rows[slot], sems[slot])

    def gather_v(k, slot):
        return pltpu.make_async_copy(v_hbm.at[idx_v.at[k]], rows[slot], sems[slot])

    def start_ahead(ahead, slot):
        @pl.when(ahead < n)
        def _():
            gather_u(ahead, slot).start()

        @pl.when(jnp.logical_and(ahead >= n, ahead < 2 * n))
        def _():
            gather_v(ahead - n, slot).start()

    def score(rw, k):
        t = k // per_tok
        for sl in range(n_slab):
            base = sl * SC_SLAB * SC_LANES
            hw = [plsc.bitcast(h_v[t, pl.ds(base + SC_LANES * j, SC_LANES)], BF16) for j in range(SC_SLAB)]

            @plsc.parallel_loop(0, SC_ROWS, 1)
            def _(r):
                terms = []
                for j in range(SC_SLAB):
                    p = packed_mul(rw[r, pl.ds(base + SC_LANES * j, SC_LANES)], hw[j])
                    terms += [_lo_f32(p), _hi_f32(p)]
                p = _tree_sum(terms)
                off = pl.multiple_of(r * SC_LANES, SC_LANES)
                if sl == 0:
                    acc_v[pl.ds(off, SC_LANES)] = p
                else:
                    acc_v[pl.ds(off, SC_LANES)] = acc_v[pl.ds(off, SC_LANES)] + p

        tot = _tree_sum([plsc.load_gather(acc_v, [lane * SC_LANES + j]) for j in range(SC_LANES)])
        coef_v[pl.ds(pl.multiple_of(k * SC_ROWS, SC_LANES), SC_LANES)] = tot

    def combine(rw, k):
        t = k // per_tok
        for sl in range(n_slab):
            base = sl * SC_SLAB * SC_LANES
            acc0 = (tuple(o_v[t, pl.ds(base + SC_LANES * j, SC_LANES)] for j in range(SC_SLAB))
                    + tuple(o_v[t, pl.ds(half + base + SC_LANES * j, SC_LANES)] for j in range(SC_SLAB)))

            def rbody(r, acc):
                w = plsc.bitcast(plsc.load_gather(cw_v, [jnp.full((SC_LANES,), k * SC_ROWS + r, I32)]), BF16)
                new = list(acc)
                for j in range(SC_SLAB):
                    p = packed_mul(rw[r, pl.ds(base + SC_LANES * j, SC_LANES)], w)
                    new[j] = acc[j] + _lo_f32(p)
                    new[SC_SLAB + j] = acc[SC_SLAB + j] + _hi_f32(p)
                return tuple(new)

            acc = lax.fori_loop(0, SC_ROWS, rbody, acc0)
            for j in range(SC_SLAB):
                o_v[t, pl.ds(base + SC_LANES * j, SC_LANES)] = acc[j]
                o_v[t, pl.ds(half + base + SC_LANES * j, SC_LANES)] = acc[SC_SLAB + j]

    def batch(bi, carry):
        t0 = pl.multiple_of(wid * tok_per_worker + bi * SC_TOK, SC_TOK)
        pltpu.sync_copy(idx_hbm.at[pl.ds(t0 * per_tok, n)], idx_v)
        for s in range(SC_BUFS - 1):
            gather_u(s, s).start()
        pltpu.sync_copy(hn_hbm.at[pl.ds(t0, SC_TOK)], h_v)
        pltpu.sync_copy(h_hbm.at[pl.ds(t0, SC_TOK)], o_v)
        pltpu.sync_copy(gate_hbm.at[pl.ds(t0 * nk, SC_TOK * nk)], gate_v)

        def score_group(i, c):
            for s in range(SC_BUFS):
                k = i * SC_BUFS + s
                start_ahead(k + SC_BUFS - 1, (s + SC_BUFS - 1) % SC_BUFS)
                gather_u(k, s).wait()
                score(rows[s], k)
            return c

        lax.fori_loop(0, n // SC_BUFS, score_group, 0)

        @plsc.parallel_loop(0, SC_TOK * nk // SC_LANES, 1)
        def _(j):
            off = pl.multiple_of(j * SC_LANES, SC_LANES)
            c = _gelu_via_exp(coef_v[pl.ds(off, SC_LANES)]) * gate_v[pl.ds(off, SC_LANES)]
            hi = _round_to_bf16_bits(lax.bitcast_convert_type(c, I32)) & HI16
            cw_v[pl.ds(off, SC_LANES)] = hi | lax.shift_right_logical(hi, jnp.int32(16))

        def combine_group(i, c):
            for s in range(SC_BUFS):
                k = i * SC_BUFS + s
                start_ahead(n + k + SC_BUFS - 1, (s + SC_BUFS - 1) % SC_BUFS)
                gather_v(k, s).wait()
                combine(rows[s], k)
            return c

        lax.fori_loop(0, n // SC_BUFS, combine_group, 0)
        pltpu.sync_copy(o_v, out_hbm.at[pl.ds(t0, SC_TOK)])
        return carry

    lax.fori_loop(0, tok_per_worker // SC_TOK, batch, 0)


def _peer_experts(hn_words, h2, idx4, gate_flat, u_words, v_words):
    t, d = h2.shape
    nk = PEER_HEADS * PEER_TOPK
    mesh = plsc.VectorSubcoreMesh(core_axis_name="c", subcore_axis_name="s")
    body = functools.partial(_peer_experts_body, tok_per_worker=t // SC_WORKERS)
    return pl.kernel(
        body,
        out_type=jax.ShapeDtypeStruct((t, d), F32),
        mesh=mesh,
        scratch_types=[
            pltpu.VMEM((SC_TOK, d // 2), I32),
            pltpu.VMEM((SC_TOK, d), F32),
            pltpu.VMEM((SC_TOK * nk // SC_ROWS, SC_ROWS), I32),
            pltpu.VMEM((SC_TOK * nk,), F32),
            pltpu.VMEM((SC_TOK * nk,), F32),
            pltpu.VMEM((SC_TOK * nk,), I32),
            pltpu.VMEM((SC_ROWS * SC_LANES,), F32),
        ] + [pltpu.VMEM((SC_ROWS, d // 2), I32)] * SC_BUFS + [pltpu.SemaphoreType.DMA] * SC_BUFS,
        compiler_params=_sc_params(),
        cost_estimate=pl.CostEstimate(flops=4 * t * nk * d, transcendentals=t * nk,
                                      bytes_accessed=4 * t * nk * d + 10 * t * d),
        name="peer_experts_sc",
    )(hn_words, h2, idx4, gate_flat, u_words, v_words)


def _mixers(h, l, norm1_g, w_in, ssm, fox_b_f, g_ssm_out, g_attn_out, w_o, ssm_w_glu, ssm_b_glu, tm, blk):
    b, s, d = h.shape
    w = N_HEADS * HEAD_DIM
    wl = w_in[l]
    wm = wl[:, :4 * w].astype(BF16)
    wf = jnp.pad(wl[:, 4 * w:], ((0, 0), (0, 128 - N_HEADS)))
    bf = jnp.pad(fox_b_f[l], (0, 128 - N_HEADS)).reshape(1, 128)
    u, qa, ka, v = _in_proj(h, norm1_g[l].reshape(1, d), wm, wf, bf, tm)

    lc = SSM_CHUNK
    nc = s // lc
    g = w // SSM_GROUP_CH
    ug = u.reshape(b, nc, lc, g, SSM_GROUP_CH).transpose(3, 1, 0, 2, 4).reshape(g, nc * b, lc * SSM_GROUP_CH)
    yg = _ssm(ug, *ssm, n_chunks=nc, batch=b)
    y_ssm = yg.reshape(g, nc, b, lc, SSM_GROUP_CH).transpose(2, 1, 3, 0, 4).reshape(b, s, w)

    y_att = _attention(qa.reshape(b * N_HEADS, s, AUG), ka.reshape(b * N_HEADS, s, AUG),
                       v.reshape(b * N_HEADS, s, HEAD_DIM), blk).reshape(b, N_HEADS, s, HEAD_DIM)

    wo = w_o[l].astype(BF16)
    return _out_proj(y_ssm, y_att, h, ssm_w_glu[l].astype(BF16), ssm_b_glu[l].reshape(1, w),
                     g_ssm_out[l].reshape(1, w), g_attn_out[l].reshape(N_HEADS, 1, HEAD_DIM),
                     wo[:w], wo[w:].reshape(N_HEADS, HEAD_DIM, d), tm)


def _keys_cat(keys_l):
    z = jnp.zeros_like(keys_l[:, 0])
    top = jnp.concatenate([keys_l[:, 0], z], axis=-1)
    bot = jnp.concatenate([z, keys_l[:, 1]], axis=-1)
    return jnp.concatenate([top, bot], axis=1).astype(BF16)


def kernel(x, norm1_g, w_in, ssm_lambda_re, ssm_lambda_im, ssm_log_dt, ssm_b_re, ssm_b_im, ssm_c_re, ssm_c_im, ssm_d, ssm_w_glu, ssm_b_glu, fox_b_f, g_ssm_out, g_attn_out, w_o, norm2_g, peer_w_q, peer_keys, peer_u, peer_v, norm_f):
    b, s, d = x.shape
    depth = w_in.shape[0]
    nk = PEER_HEADS * PEER_TOPK
    tm = min(512, s)
    blk = min(256, s)
    assert b % 2 == 0
    bh = b // 2
    t = bh * s
    ssm_tabs = [_ssm_tables(ssm_lambda_re[l], ssm_lambda_im[l], ssm_log_dt[l], ssm_b_re[l], ssm_b_im[l],
                            ssm_c_re[l], ssm_c_im[l], ssm_d[l]) for l in range(depth)]

    def dense_stage(h, l):
        h = _mixers(h, l, norm1_g, w_in, ssm_tabs[l], fox_b_f, g_ssm_out, g_attn_out, w_o, ssm_w_glu, ssm_b_glu,
                    tm, blk)
        h2 = h.reshape(t, d)
        hn, idx, gate = _route(h2, norm2_g[l].reshape(1, d), peer_w_q[l].astype(BF16), _keys_cat(peer_keys[l]),
                               min(256, t))
        return h2, hn, idx.reshape(t * nk // SC_ROWS, SC_ROWS), gate.reshape(t * nk)

    tabs = [(_pack_bf16_pairs(peer_u[l]), _pack_bf16_pairs(peer_v[l])) for l in range(depth)]

    def expert_stage(st, l):
        h2, hn, idx4, gate = st
        return _peer_experts(hn, h2, idx4, gate, tabs[l][0], tabs[l][1]).reshape(bh, s, d)

    def after(value, st):
        value, gate = lax.optimization_barrier((value, st[3]))
        return value, st[:3] + (gate,)

    st_a = dense_stage(x[:bh], 0)
    h_b, st_a = after(x[bh:], st_a)
    for l in range(depth):
        h_a = expert_stage(st_a, l)
        st_b = dense_stage(h_b, l)
        h_a, st_b = after(h_a, st_b)
        h_b = expert_stage(st_b, l)
        if l + 1 < depth:
            st_a = dense_stage(h_a, l + 1)
            h_b, st_a = after(h_b, st_a)
    outs = [_final_norm(h.reshape(t, d), norm_f.reshape(1, d), min(512, t)).reshape(bh, s, d) for h in (h_a, h_b)]
    return jnp.concatenate(outs, axis=0)
```

```python
import functools
import math

import jax
import jax.numpy as jnp
from jax import lax
from jax.experimental import pallas as pl
from jax.experimental.pallas import tpu as pltpu
from jax.experimental.pallas import tpu_sc as plsc

F32 = jnp.float32
BF16 = jnp.bfloat16
I32 = jnp.int32

RMS_EPS = 1e-6
SSM_GROUP_CH = 16
SSM_STATE = 64
SSM_CHUNK = 128
HEAD_DIM = 64
N_HEADS = 8
AUG = 128
PEER_HEADS = 8
PEER_KEYS = 128
PEER_TOPK = 16
VMEM_LIMIT = 56 * 1024 * 1024
N_PARTS = 2


def _rms(x, g):
    return x * lax.rsqrt(jnp.mean(x * x, axis=-1, keepdims=True) + RMS_EPS) * g


def _gelu(x):
    c = math.sqrt(2.0 / math.pi)
    return 0.5 * x * (1.0 + jnp.tanh(c * (x + 0.044715 * (x * x * x))))


def _sigmoid(x):
    return 1.0 / (1.0 + jnp.exp(-x))


def _in_proj_kernel(h_ref, g_ref, wu_ref, wm_ref, wf_ref, bf_ref, u_ref, qa_ref, ka_ref, v_ref, cum_ref):
    j = pl.program_id(1)

    @pl.when(j == 0)
    def _():
        cum_ref[...] = jnp.zeros_like(cum_ref)

    x = h_ref[0]
    tm = x.shape[0]
    xn = _rms(x, g_ref[...])
    xb = xn.astype(BF16)
    proj = jnp.dot(xb, wm_ref[...], preferred_element_type=F32)
    u_ref[...] = lax.dot_general(wu_ref[...], xb, (((1,), (1,)), ((), ())),
                                 preferred_element_type=F32).astype(BF16)
    f = jnp.dot(xn, wf_ref[...], precision=lax.Precision.HIGHEST, preferred_element_type=F32) + bf_ref[...]
    logf = jnp.minimum(f, 0.0) - jnp.log(1.0 + jnp.exp(-jnp.abs(f)))
    row = lax.broadcasted_iota(I32, (tm, tm), 0)
    col = lax.broadcasted_iota(I32, (tm, tm), 1)
    tri = (row >= col).astype(F32)
    cum = jnp.dot(tri, logf, precision=lax.Precision.HIGHEST, preferred_element_type=F32) + cum_ref[0:1, :]
    cum_ref[0:1, :] = cum[tm - 1:tm, :]

    w = HEAD_DIM * N_HEADS
    lane = lax.broadcasted_iota(I32, (tm, AUG), 1)
    scale = HEAD_DIM ** -0.5
    for hh in range(N_HEADS):
        pair = hh // 2
        q2 = proj[:, 128 * pair: 128 * pair + 128]
        k2 = proj[:, w + 128 * pair: w + 128 * pair + 128]
        v2 = proj[:, 2 * w + 128 * pair: 2 * w + 128 * pair + 128]
        if hh % 2 == 1:
            q2 = pltpu.roll(q2, 64, axis=1)
            k2 = pltpu.roll(k2, 64, axis=1)
            vh = v2[:, 64:]
        else:
            vh = v2[:, :64]
        c = jnp.broadcast_to(cum[:, hh:hh + 1], (tm, AUG))
        c1 = c.astype(BF16).astype(F32)
        r1 = c - c1
        c2 = r1.astype(BF16).astype(F32)
        c3 = r1 - c2
        one = jnp.ones((tm, AUG), F32)
        zero = jnp.zeros((tm, AUG), F32)
        qa = jnp.where(lane < 64, q2 * scale,
             jnp.where(lane == 64, c1, jnp.where(lane == 65, c2, jnp.where(lane == 66, c3,
             jnp.where(lane < 70, one, zero)))))
        ka = jnp.where(lane < 64, k2,
             jnp.where(lane < 67, one, jnp.where(lane == 67, -c1, jnp.where(lane == 68, -c2,
             jnp.where(lane == 69, -c3, zero)))))
        qa_ref[0, hh] = qa.astype(BF16)
        ka_ref[0, hh] = ka.astype(BF16)
        v_ref[0, hh] = vh.astype(BF16)


def _in_proj(h, g, wu_t, wm, wf, bf, tm):
    b, s, d = h.shape
    w = HEAD_DIM * N_HEADS
    ns = s // tm
    return pl.pallas_call(
        _in_proj_kernel,
        grid=(b, ns),
        in_specs=[
            pl.BlockSpec((1, tm, d), lambda i, j: (i, j, 0)),
            pl.BlockSpec((1, d), lambda i, j: (0, 0)),
            pl.BlockSpec((w, d), lambda i, j: (0, 0)),
            pl.BlockSpec((d, 3 * w), lambda i, j: (0, 0)),
            pl.BlockSpec((d, 128), lambda i, j: (0, 0)),
            pl.BlockSpec((1, 128), lambda i, j: (0, 0)),
        ],
        out_specs=[
            pl.BlockSpec((w, tm), lambda i, j: (0, i * ns + j)),
            pl.BlockSpec((1, N_HEADS, tm, AUG), lambda i, j: (i, 0, j, 0)),
            pl.BlockSpec((1, N_HEADS, tm, AUG), lambda i, j: (i, 0, j, 0)),
            pl.BlockSpec((1, N_HEADS, tm, HEAD_DIM), lambda i, j: (i, 0, j, 0)),
        ],
        out_shape=[
            jax.ShapeDtypeStruct((w, b * s), BF16),
            jax.ShapeDtypeStruct((b, N_HEADS, s, AUG), BF16),
            jax.ShapeDtypeStruct((b, N_HEADS, s, AUG), BF16),
            jax.ShapeDtypeStruct((b, N_HEADS, s, HEAD_DIM), BF16),
        ],
        scratch_shapes=[pltpu.VMEM((8, 128), F32)],
        compiler_params=pltpu.CompilerParams(
            dimension_semantics=("parallel", "arbitrary"), vmem_limit_bytes=VMEM_LIMIT),
        cost_estimate=pl.CostEstimate(flops=2 * b * s * d * (4 * w + 128), transcendentals=2 * b * s * 128,
                                      bytes_accessed=4 * b * s * d + 2 * d * (4 * w) + 2 * b * s * (w + 2 * N_HEADS * AUG + w)),
        name="in_proj",
    )(h, g, wu_t, wm, wf, bf)


def _toeplitz_kernel(k_ref, m_ref):
    lc = SSM_CHUNK
    hc = SSM_GROUP_CH
    row = lax.broadcasted_iota(I32, (lc, lc), 0)
    col = lax.broadcasted_iota(I32, (lc, lc), 1)
    causal = col >= row

    def body(hi, carry):
        r0 = pl.multiple_of(hi * lc, lc)
        for ho in range(hc):
            k = jnp.broadcast_to(k_ref[0, pl.ds(hi * hc + ho, 1), :], (lc, lc))
            t = pltpu.roll(k, 0, 1, stride=1, stride_axis=0)
            m_ref[0, pl.ds(r0, lc), ho * lc:(ho + 1) * lc] = jnp.where(causal, t, 0.0).astype(BF16)
        return carry

    lax.fori_loop(0, hc, body, 0)


def _toeplitz(kq):
    g, hh, lc = kq.shape
    width = SSM_GROUP_CH * lc
    return pl.pallas_call(
        _toeplitz_kernel,
        grid=(g,),
        in_specs=[pl.BlockSpec((1, hh, lc), lambda i: (i, 0, 0))],
        out_specs=pl.BlockSpec((1, width, width), lambda i: (i, 0, 0)),
        out_shape=jax.ShapeDtypeStruct((g, width, width), BF16),
        compiler_params=pltpu.CompilerParams(dimension_semantics=("parallel",), vmem_limit_bytes=VMEM_LIMIT),
        cost_estimate=pl.CostEstimate(flops=g * width * width, transcendentals=0,
                                      bytes_accessed=2 * g * width * width + 4 * g * hh * lc),
        name="ssm_toeplitz",
    )(kq)


def _ssm_kernel(u_ref, m_ref, w_ref, r_ref, a_ref, d_ref, y_ref, e_scr, *, n_chunks, batch):
    hc = SSM_GROUP_CH
    lc = SSM_CHUNK
    u = jnp.concatenate([u_ref[0, r] for r in range(hc)], axis=1)
    y = jnp.dot(u, m_ref[0], preferred_element_type=F32)
    e_scr[...] = jnp.dot(u, w_ref[0], preferred_element_type=F32)
    ar = a_ref[0, 0:1, :]
    ai = a_ref[0, 1:2, :]

    def body(c, s):
        rows = pl.ds(c, batch, stride=n_chunks)
        e_c = e_scr[rows, :]
        e_scr[rows, :] = s
        return ar * s + ai * pltpu.roll(s, SSM_STATE, axis=1) + e_c

    lax.fori_loop(0, n_chunks, body, jnp.zeros((batch, 2 * SSM_STATE), F32))
    y = y + jnp.dot(e_scr[...].astype(BF16), r_ref[0], preferred_element_type=F32)
    y = y + u.astype(F32) * d_ref[0]
    for r in range(hc):
        y_ref[0, r] = y[:, r * lc:(r + 1) * lc]


def _ssm(u4, m, w, r, a, d, n_chunks, batch):
    g, hc, rows, lc = u4.shape
    width = hc * lc
    kern = functools.partial(_ssm_kernel, n_chunks=n_chunks, batch=batch)
    return pl.pallas_call(
        kern,
        grid=(g,),
        in_specs=[
            pl.BlockSpec((1, hc, rows, lc), lambda i: (i, 0, 0, 0)),
            pl.BlockSpec((1, width, width), lambda i: (i, 0, 0)),
            pl.BlockSpec((1, width, 2 * SSM_STATE), lambda i: (i, 0, 0)),
            pl.BlockSpec((1, 2 * SSM_STATE, width), lambda i: (i, 0, 0)),
            pl.BlockSpec((1, 8, 2 * SSM_STATE), lambda i: (i, 0, 0)),
            pl.BlockSpec((1, 1, width), lambda i: (i, 0, 0)),
        ],
        out_specs=pl.BlockSpec((1, hc, rows, lc), lambda i: (i, 0, 0, 0)),
        out_shape=jax.ShapeDtypeStruct((g, hc, rows, lc), F32),
        scratch_shapes=[pltpu.VMEM((rows, 2 * SSM_STATE), F32)],
        compiler_params=pltpu.CompilerParams(
            dimension_semantics=("parallel",), vmem_limit_bytes=VMEM_LIMIT),
        cost_estimate=pl.CostEstimate(flops=2 * g * rows * width * (width + 4 * SSM_STATE), transcendentals=0,
                                      bytes_accessed=g * (2 * width * (width + 4 * SSM_STATE) + 6 * rows * width)),
        name="ssm",
    )(u4, m, w, r, a, d)


def _ssm_tables(lam_re, lam_im, log_dt, b_re, b_im, c_re, c_im, d_skip):
    hp = lax.Precision.HIGHEST
    lc = SSM_CHUNK
    g, p = lam_re.shape
    hc = SSM_GROUP_CH
    dt = jnp.exp(log_dt)[:, None]
    a_re = jnp.exp(lam_re * dt) * jnp.cos(lam_im * dt)
    a_im = jnp.exp(lam_re * dt) * jnp.sin(lam_im * dt)
    den = lam_re * lam_re + lam_im * lam_im
    nr = a_re - 1.0
    z_re = (nr * lam_re + a_im * lam_im) / den
    z_im = (a_im * lam_re - nr * lam_im) / den
    bb_re = z_re[..., None] * b_re - z_im[..., None] * b_im
    bb_im = z_re[..., None] * b_im + z_im[..., None] * b_re
    tau = jnp.arange(lc + 1, dtype=F32)[:, None, None]
    mag = jnp.exp(tau * (lam_re * dt)[None])
    ang = tau * (lam_im * dt)[None]
    p_re = mag * jnp.cos(ang)
    p_im = mag * jnp.sin(ang)
    ab_re = p_re[:lc, :, :, None] * bb_re[None] - p_im[:lc, :, :, None] * bb_im[None]
    ab_im = p_re[:lc, :, :, None] * bb_im[None] + p_im[:lc, :, :, None] * bb_re[None]
    kq = (jnp.einsum('ghp,tgpk->gkht', c_re, ab_re, precision=hp)
          - jnp.einsum('ghp,tgpk->gkht', c_im, ab_im, precision=hp)).reshape(g, hc * hc, lc)
    m = _toeplitz(kq)
    w_re = ab_re[::-1].transpose(1, 3, 0, 2).reshape(g, hc * lc, p)
    w_im = ab_im[::-1].transpose(1, 3, 0, 2).reshape(g, hc * lc, p)
    w = jnp.concatenate([w_re, w_im], axis=-1)
    q_re = p_re[1:, :, None, :] * c_re[None] - p_im[1:, :, None, :] * c_im[None]
    q_im = p_re[1:, :, None, :] * c_im[None] + p_im[1:, :, None, :] * c_re[None]
    r = jnp.concatenate([q_re.transpose(1, 3, 2, 0).reshape(g, p, hc * lc),
                         -q_im.transpose(1, 3, 2, 0).reshape(g, p, hc * lc)], axis=1)
    al_re, al_im = p_re[lc], p_im[lc]
    a = jnp.zeros((g, 8, 2 * p), F32)
    a = a.at[:, 0, :].set(jnp.concatenate([al_re, al_re], axis=-1))
    a = a.at[:, 1, :].set(jnp.concatenate([-al_im, al_im], axis=-1))
    d = jnp.repeat(d_skip, lc, axis=-1).reshape(g, 1, hc * lc)
    return m, w.astype(BF16), r.astype(BF16), a, d


def _attn_kernel(q_ref, k_ref, v_ref, o_ref, m_ref, l_ref, acc_ref, *, blk):
    i = pl.program_id(1)
    q = q_ref[0]
    m_ref[...] = jnp.full_like(m_ref, -jnp.inf)
    l_ref[...] = jnp.zeros_like(l_ref)
    acc_ref[...] = jnp.zeros_like(acc_ref)

    def step(j, masked):
        off = pl.multiple_of(j * blk, blk)
        k = k_ref[0, pl.ds(off, blk), :]
        v = v_ref[0, pl.ds(off, blk), :]
        s = lax.dot_general(q, k, (((1,), (1,)), ((), ())), preferred_element_type=F32)
        if masked:
            row = lax.broadcasted_iota(I32, s.shape, 0)
            col = lax.broadcasted_iota(I32, s.shape, 1)
            s = jnp.where(row >= col, s, -jnp.inf)
        m_prev = m_ref[...]
        m_new = jnp.maximum(m_prev, jnp.max(s, axis=1, keepdims=True))
        p = jnp.exp(s - m_new)
        alpha = jnp.exp(m_prev - m_new)
        l_ref[...] = alpha * l_ref[...] + jnp.sum(p, axis=1, keepdims=True)
        acc_ref[...] = alpha * acc_ref[...] + jnp.dot(p.astype(BF16), v, preferred_element_type=F32)
        m_ref[...] = m_new

    def body(j, c):
        step(j, False)
        return c

    lax.fori_loop(0, i, body, 0)
    step(i, True)
    o_ref[0] = acc_ref[...] / l_ref[...]


def _attention(qa, ka, v, blk):
    bh, s, _ = qa.shape
    kern = functools.partial(_attn_kernel, blk=blk)
    return pl.pallas_call(
        kern,
        grid=(bh, s // blk),
        in_specs=[
            pl.BlockSpec((1, blk, AUG), lambda b, i: (b, i, 0)),
            pl.BlockSpec((1, s, AUG), lambda b, i: (b, 0, 0)),
            pl.BlockSpec((1, s, HEAD_DIM), lambda b, i: (b, 0, 0)),
        ],
        out_specs=pl.BlockSpec((1, blk, HEAD_DIM), lambda b, i: (b, i, 0)),
        out_shape=jax.ShapeDtypeStruct((bh, s, HEAD_DIM), F32),
        scratch_shapes=[pltpu.VMEM((blk, 1), F32), pltpu.VMEM((blk, 1), F32), pltpu.VMEM((blk, HEAD_DIM), F32)],
        compiler_params=pltpu.CompilerParams(
            dimension_semantics=("parallel", "arbitrary"), vmem_limit_bytes=VMEM_LIMIT),
        cost_estimate=pl.CostEstimate(flops=bh * s * s * (AUG + HEAD_DIM), transcendentals=bh * s * s // 2,
                                      bytes_accessed=bh * s * (2 * 2 * AUG + 2 * HEAD_DIM + 4 * HEAD_DIM)),
        name="fox_attn",
    )(qa, ka, v)


def _out_proj_kernel(y_ref, att_ref, h_ref, wg_ref, bg_ref, gs_ref, ga_ref, wos_ref, woa_ref, o_ref):
    g = _gelu(y_ref[...])
    z = jnp.dot(wg_ref[...], g.astype(BF16), preferred_element_type=F32) + bg_ref[...]
    o = g * _sigmoid(z)
    a = o * lax.rsqrt(jnp.mean(o * o, axis=0, keepdims=True) + RMS_EPS) * gs_ref[...]
    acc = h_ref[0] + lax.dot_general(a.astype(BF16), wos_ref[...], (((0,), (0,)), ((), ())),
                                     preferred_element_type=F32)
    ssq = jnp.zeros((acc.shape[0], 1), F32)
    for hh in range(N_HEADS):
        t = att_ref[0, hh]
        ssq = ssq + jnp.sum(t * t, axis=1, keepdims=True)
    inv = lax.rsqrt(ssq / (N_HEADS * HEAD_DIM) + RMS_EPS)
    for hh in range(N_HEADS):
        bh = att_ref[0, hh] * inv * ga_ref[hh]
        acc = acc + jnp.dot(bh.astype(BF16), woa_ref[hh], preferred_element_type=F32)
    o_ref[0] = acc


def _out_proj(y_t, y_att, h, wg_t, bg, gs, ga, wos, woa, tm):
    b, s, d = h.shape
    w = y_t.shape[0]
    ns = s // tm
    return pl.pallas_call(
        _out_proj_kernel,
        grid=(b, ns),
        in_specs=[
            pl.BlockSpec((w, tm), lambda i, j: (0, i * ns + j)),
            pl.BlockSpec((1, N_HEADS, tm, HEAD_DIM), lambda i, j: (i, 0, j, 0)),
            pl.BlockSpec((1, tm, d), lambda i, j: (i, j, 0)),
            pl.BlockSpec((w, w), lambda i, j: (0, 0)),
            pl.BlockSpec((w, 1), lambda i, j: (0, 0)),
            pl.BlockSpec((w, 1), lambda i, j: (0, 0)),
            pl.BlockSpec((N_HEADS, 1, HEAD_DIM), lambda i, j: (0, 0, 0)),
            pl.BlockSpec((w, d), lambda i, j: (0, 0)),
            pl.BlockSpec((N_HEADS, HEAD_DIM, d), lambda i, j: (0, 0, 0)),
        ],
        out_specs=pl.BlockSpec((1, tm, d), lambda i, j: (i, j, 0)),
        out_shape=jax.ShapeDtypeStruct((b, s, d), F32),
        compiler_params=pltpu.CompilerParams(
            dimension_semantics=("parallel", "parallel"), vmem_limit_bytes=VMEM_LIMIT),
        cost_estimate=pl.CostEstimate(flops=2 * b * s * (w * w + 2 * w * d), transcendentals=2 * b * s * w,
                                      bytes_accessed=b * s * (8 * w + 8 * d) + 2 * (w * w + 2 * w * d)),
        name="out_proj",
    )(y_t, y_att, h, wg_t, bg, gs, ga, wos, woa)


def _take_top(vals, payload, k):
    n_rows = vals.shape[0]
    rows = lax.broadcasted_iota(I32, vals.shape, 0)
    tops, picks = [], []
    for _ in range(k):
        m = jnp.max(vals, axis=0, keepdims=True)
        arg = jnp.min(jnp.where(vals == m, rows, n_rows), axis=0, keepdims=True)
        hit = rows == arg
        tops.append(m)
        picks.append(arg if payload is None else jnp.max(jnp.where(hit, payload, -1), axis=0, keepdims=True))
        vals = jnp.where(hit, -jnp.inf, vals)
    return jnp.concatenate(tops, axis=0), jnp.concatenate(picks, axis=0)


def _route_kernel(h_ref, g_ref, wq_ref, keys_ref, hn_ref, idx_ref, gate_ref):
    x = h_ref[...]
    hn = _rms(x, g_ref[...])
    hn_ref[...] = _pack_bf16_pairs(hn)
    q = jnp.dot(hn.astype(BF16), wq_ref[...], preferred_element_type=F32)
    k = PEER_TOPK
    idx_rows, gate_rows = [], []
    for hh in range(PEER_HEADS):
        qh = q[:, 128 * hh: 128 * (hh + 1)].astype(BF16)
        sc = lax.dot_general(keys_ref[hh], qh, (((1,), (1,)), ((), ())), preferred_element_type=F32)
        t1, i1 = _take_top(sc[:PEER_KEYS], None, k)
        t2, i2 = _take_top(sc[PEER_KEYS:], None, k)
        widths = [k // (a + 1) for a in range(k)]
        pad = -sum(widths) % 8
        cand = jnp.concatenate([t1[a:a + 1] + t2[:widths[a]] for a in range(k)]
                               + [jnp.full((pad, t1.shape[1]), -jnp.inf, F32)], axis=0)
        cidx = jnp.concatenate([i1[a:a + 1] * PEER_KEYS + i2[:widths[a]] for a in range(k)]
                               + [jnp.full((pad, t1.shape[1]), -1, I32)], axis=0)
        best, idx = _take_top(cand, cidx, k)
        e = jnp.exp(best - best[0:1])
        gate = e / jnp.sum(e, axis=0, keepdims=True)
        idx_rows.append(idx)
        gate_rows.append(gate)
    idx_ref[...] = jnp.concatenate(idx_rows, axis=0).T
    gate_ref[...] = jnp.concatenate(gate_rows, axis=0).T


def _route(h2, g, wq, keys_cat, tm):
    t, d = h2.shape
    nk = PEER_HEADS * PEER_TOPK
    return pl.pallas_call(
        _route_kernel,
        grid=(t // tm,),
        in_specs=[
            pl.BlockSpec((tm, d), lambda i: (i, 0)),
            pl.BlockSpec((1, d), lambda i: (0, 0)),
            pl.BlockSpec((d, PEER_HEADS * 128), lambda i: (0, 0)),
            pl.BlockSpec((PEER_HEADS, 2 * PEER_KEYS, 128), lambda i: (0, 0, 0)),
        ],
        out_specs=[
            pl.BlockSpec((tm, d // 2), lambda i: (i, 0)),
            pl.BlockSpec((tm, nk), lambda i: (i, 0)),
            pl.BlockSpec((tm, nk), lambda i: (i, 0)),
        ],
        out_shape=[
            jax.ShapeDtypeStruct((t, d // 2), I32),
            jax.ShapeDtypeStruct((t, nk), I32),
            jax.ShapeDtypeStruct((t, nk), F32),
        ],
        compiler_params=pltpu.CompilerParams(
            dimension_semantics=("parallel",), vmem_limit_bytes=VMEM_LIMIT),
        cost_estimate=pl.CostEstimate(flops=2 * t * d * PEER_HEADS * 128 + 2 * t * PEER_HEADS * 256 * 128,
                                      transcendentals=t * nk, bytes_accessed=t * (8 * d + 8 * nk) + 2 * d * PEER_HEADS * 128),
        name="peer_route",
    )(h2, g, wq, keys_cat)


def _final_norm_kernel(x_ref, g_ref, o_ref):
    o_ref[...] = _rms(x_ref[...], g_ref[...])


def _final_norm(x2, g, tm):
    t, d = x2.shape
    return pl.pallas_call(
        _final_norm_kernel,
        grid=(t // tm,),
        in_specs=[pl.BlockSpec((tm, d), lambda i: (i, 0)), pl.BlockSpec((1, d), lambda i: (0, 0))],
        out_specs=pl.BlockSpec((tm, d), lambda i: (i, 0)),
        out_shape=jax.ShapeDtypeStruct((t, d), F32),
        compiler_params=pltpu.CompilerParams(dimension_semantics=("parallel",)),
        cost_estimate=pl.CostEstimate(flops=4 * t * d, transcendentals=t, bytes_accessed=8 * t * d),
        name="final_norm",
    )(x2, g)


SC_WORKERS = 32
SC_CORES = 2
SC_LANES = 16
SC_TOK = 16
SC_ROWS = 16
SC_BUFS = 4
SC_SLAB = 8
HI16 = -65536


def _sc_params():
    cp = pltpu.CompilerParams()
    if "needs_layout_passes" in pltpu.CompilerParams.__dataclass_fields__:
        cp = pltpu.CompilerParams(needs_layout_passes=False)
    return cp


def _sc_worker_id():
    return lax.axis_index("s") * SC_CORES + lax.axis_index("c")


def _tree_sum(xs):
    xs = list(xs)
    while len(xs) > 1:
        xs = [xs[i] + xs[i + 1] for i in range(0, len(xs) - 1, 2)] + ([xs[-1]] if len(xs) % 2 else [])
    return xs[0]


def _gelu_via_exp(x):
    c = math.sqrt(2.0 / math.pi)
    z = c * (x + 0.044715 * (x * x * x))
    return 0.5 * x * (2.0 - 2.0 / (1.0 + jnp.exp(2.0 * z)))


def _round_to_bf16_bits(bits):
    return bits + 0x7FFF + (lax.shift_right_logical(bits, jnp.int32(16)) & 1)


def _pack_bf16_pairs(x):
    w = x.shape[-1] // 2
    r = _round_to_bf16_bits(lax.bitcast_convert_type(x, I32))
    return lax.shift_right_logical(r[..., :w], jnp.int32(16)) | (r[..., w:] & HI16)


def _lo_f32(word):
    return lax.bitcast_convert_type(lax.shift_left(word, jnp.int32(16)), F32)


def _hi_f32(word):
    return lax.bitcast_convert_type(word & HI16, F32)


def _peer_experts_body(hn_hbm, h_hbm, idx_hbm, gate_hbm, u_hbm, v_hbm, out_hbm,
                       h_v, o_v, idx_v, gate_v, coef_v, cw_v, acc_v, *bufs, tok_per_worker):
    rows, sems = bufs[:SC_BUFS], bufs[SC_BUFS:]
    half = hn_hbm.shape[1]
    nk = PEER_HEADS * PEER_TOPK
    per_tok = nk // SC_ROWS
    n = SC_TOK * per_tok
    n_slab = half // (SC_SLAB * SC_LANES)
    lane = lax.iota(I32, SC_LANES)
    wid = _sc_worker_id()

    def packed_mul(word, other_bf16):
        return plsc.bitcast(plsc.bitcast(word, BF16) * other_bf16, I32)

    def gather_u(k, slot):
        return pltpu.make_async_copy(u_hbm.at[idx_v.at[k]], rows[slot], sems[slot])

    def gather_v(k, slot):
        return pltpu.make_async_copy(v_hbm.at[idx_v.at[k]], rows[slot], sems[slot])

    def start_ahead(ahead, slot):
        @pl.when(ahead < n)
        def _():
            gather_u(ahead, slot).start()

        @pl.when(jnp.logical_and(ahead >= n, ahead < 2 * n))
        def _():
            gather_v(ahead - n, slot).start()

    def score(rw, k):
        t = k // per_tok
        for sl in range(n_slab):
            base = sl * SC_SLAB * SC_LANES
            hw = [plsc.bitcast(h_v[t, pl.ds(base + SC_LANES * j, SC_LANES)], BF16) for j in range(SC_SLAB)]

            @plsc.parallel_loop(0, SC_ROWS, 1)
            def _(r):
                terms = []
                for j in range(SC_SLAB):
                    p = packed_mul(rw[r, pl.ds(base + SC_LANES * j, SC_LANES)], hw[j])
                    terms += [_lo_f32(p), _hi_f32(p)]
                p = _tree_sum(terms)
                off = pl.multiple_of(r * SC_LANES, SC_LANES)
                if sl == 0:
                    acc_v[pl.ds(off, SC_LANES)] = p
                else:
                    acc_v[pl.ds(off, SC_LANES)] = acc_v[pl.ds(off, SC_LANES)] + p

        tot = _tree_sum([plsc.load_gather(acc_v, [lane * SC_LANES + j]) for j in range(SC_LANES)])
        coef_v[pl.ds(pl.multiple_of(k * SC_ROWS, SC_LANES), SC_LANES)] = tot

    def combine(rw, k):
        t = k // per_tok
        for sl in range(n_slab):
            base = sl * SC_SLAB * SC_LANES
            acc0 = (tuple(o_v[t, pl.ds(base + SC_LANES * j, SC_LANES)] for j in range(SC_SLAB))
                    + tuple(o_v[t, pl.ds(half + base + SC_LANES * j, SC_LANES)] for j in range(SC_SLAB)))

            def rbody(r, acc):
                w = plsc.bitcast(plsc.load_gather(cw_v, [jnp.full((SC_LANES,), k * SC_ROWS + r, I32)]), BF16)
                new = list(acc)
                for j in range(SC_SLAB):
                    p = packed_mul(rw[r, pl.ds(base + SC_LANES * j, SC_LANES)], w)
                    new[j] = acc[j] + _lo_f32(p)
                    new[SC_SLAB + j] = acc[SC_SLAB + j] + _hi_f32(p)
                return tuple(new)

            acc = lax.fori_loop(0, SC_ROWS, rbody, acc0)
            for j in range(SC_SLAB):
                o_v[t, pl.ds(base + SC_LANES * j, SC_LANES)] = acc[j]
                o_v[t, pl.ds(half + base + SC_LANES * j, SC_LANES)] = acc[SC_SLAB + j]

    def batch(bi, carry):
        t0 = pl.multiple_of(wid * tok_per_worker + bi * SC_TOK, SC_TOK)
        pltpu.sync_copy(idx_hbm.at[pl.ds(t0 * per_tok, n)], idx_v)
        for s in range(SC_BUFS - 1):
            gather_u(s, s).start()
        pltpu.sync_copy(hn_hbm.at[pl.ds(t0, SC_TOK)], h_v)
        pltpu.sync_copy(h_hbm.at[pl.ds(t0, SC_TOK)], o_v)
        pltpu.sync_copy(gate_hbm.at[pl.ds(t0 * nk, SC_TOK * nk)], gate_v)

        def score_group(i, c):
            for s in range(SC_BUFS):
                k = i * SC_BUFS + s
                start_ahead(k + SC_BUFS - 1, (s + SC_BUFS - 1) % SC_BUFS)
                gather_u(k, s).wait()
                score(rows[s], k)
            return c

        lax.fori_loop(0, n // SC_BUFS, score_group, 0)

        @plsc.parallel_loop(0, SC_TOK * nk // SC_LANES, 1)
        def _(j):
            off = pl.multiple_of(j * SC_LANES, SC_LANES)
            c = _gelu_via_exp(coef_v[pl.ds(off, SC_LANES)]) * gate_v[pl.ds(off, SC_LANES)]
            hi = _round_to_bf16_bits(lax.bitcast_convert_type(c, I32)) & HI16
            cw_v[pl.ds(off, SC_LANES)] = hi | lax.shift_right_logical(hi, jnp.int32(16))

        def combine_group(i, c):
            for s in range(SC_BUFS):
                k = i * SC_BUFS + s
                start_ahead(n + k + SC_BUFS - 1, (s + SC_BUFS - 1) % SC_BUFS)
                gather_v(k, s).wait()
                combine(rows[s], k)
            return c

        lax.fori_loop(0, n // SC_BUFS, combine_group, 0)
        pltpu.sync_copy(o_v, out_hbm.at[pl.ds(t0, SC_TOK)])
        return carry

    lax.fori_loop(0, tok_per_worker // SC_TOK, batch, 0)


def _peer_experts(hn_words, h2, idx4, gate_flat, u_words, v_words):
    t, d = h2.shape
    nk = PEER_HEADS * PEER_TOPK
    mesh = plsc.VectorSubcoreMesh(core_axis_name="c", subcore_axis_name="s")
    body = functools.partial(_peer_experts_body, tok_per_worker=t // SC_WORKERS)
    return pl.kernel(
        body,
        out_type=jax.ShapeDtypeStruct((t, d), F32),
        mesh=mesh,
        scratch_types=[
            pltpu.VMEM((SC_TOK, d // 2), I32),
            pltpu.VMEM((SC_TOK, d), F32),
            pltpu.VMEM((SC_TOK * nk // SC_ROWS, SC_ROWS), I32),
            pltpu.VMEM((SC_TOK * nk,), F32),
            pltpu.VMEM((SC_TOK * nk,), F32),
            pltpu.VMEM((SC_TOK * nk,), I32),
            pltpu.VMEM((SC_ROWS * SC_LANES,), F32),
        ] + [pltpu.VMEM((SC_ROWS, d // 2), I32)] * SC_BUFS + [pltpu.SemaphoreType.DMA] * SC_BUFS,
        compiler_params=_sc_params(),
        cost_estimate=pl.CostEstimate(flops=4 * t * nk * d, transcendentals=t * nk,
                                      bytes_accessed=4 * t * nk * d + 10 * t * d),
        name="peer_experts_sc",
    )(hn_words, h2, idx4, gate_flat, u_words, v_words)


def _mixers(h, l, norm1_g, w_in, ssm, fox_b_f, g_ssm_out, g_attn_out, w_o, ssm_w_glu, ssm_b_glu, tm, blk):
    b, s, d = h.shape
    w = N_HEADS * HEAD_DIM
    wl = w_in[l]
    wu_t = wl[:, :w].T.astype(BF16)
    wm = wl[:, w:4 * w].astype(BF16)
    wf = jnp.pad(wl[:, 4 * w:], ((0, 0), (0, 128 - N_HEADS)))
    bf = jnp.pad(fox_b_f[l], (0, 128 - N_HEADS)).reshape(1, 128)
    u_t, qa, ka, v = _in_proj(h, norm1_g[l].reshape(1, d), wu_t, wm, wf, bf, tm)

    lc = SSM_CHUNK
    nc = s // lc
    g = w // SSM_GROUP_CH
    y4 = _ssm(u_t.reshape(g, SSM_GROUP_CH, b * nc, lc), *ssm, n_chunks=nc, batch=b)
    y_t = y4.reshape(w, b * s)

    y_att = _attention(qa.reshape(b * N_HEADS, s, AUG), ka.reshape(b * N_HEADS, s, AUG),
                       v.reshape(b * N_HEADS, s, HEAD_DIM), blk).reshape(b, N_HEADS, s, HEAD_DIM)

    wo = w_o[l].astype(BF16)
    return _out_proj(y_t, y_att, h, ssm_w_glu[l].T.astype(BF16), ssm_b_glu[l].reshape(w, 1),
                     g_ssm_out[l].reshape(w, 1), g_attn_out[l].reshape(N_HEADS, 1, HEAD_DIM),
                     wo[:w], wo[w:].reshape(N_HEADS, HEAD_DIM, d), tm)


def _keys_cat(keys_l):
    z = jnp.zeros_like(keys_l[:, 0])
    top = jnp.concatenate([keys_l[:, 0], z], axis=-1)
    bot = jnp.concatenate([z, keys_l[:, 1]], axis=-1)
    return jnp.concatenate([top, bot], axis=1).astype(BF16)


def kernel(x, norm1_g, w_in, ssm_lambda_re, ssm_lambda_im, ssm_log_dt, ssm_b_re, ssm_b_im, ssm_c_re, ssm_c_im, ssm_d, ssm_w_glu, ssm_b_glu, fox_b_f, g_ssm_out, g_attn_out, w_o, norm2_g, peer_w_q, peer_keys, peer_u, peer_v, norm_f):
    b, s, d = x.shape
    depth = w_in.shape[0]
    nk = PEER_HEADS * PEER_TOPK
    tm = min(512, s)
    blk = min(512, s)
    assert b % N_PARTS == 0
    bh = b // N_PARTS
    t = bh * s
    ssm_tabs = [_ssm_tables(ssm_lambda_re[l], ssm_lambda_im[l], ssm_log_dt[l], ssm_b_re[l], ssm_b_im[l],
                            ssm_c_re[l], ssm_c_im[l], ssm_d[l]) for l in range(depth)]

    def dense_stage(h, l):
        h = _mixers(h, l, norm1_g, w_in, ssm_tabs[l], fox_b_f, g_ssm_out, g_attn_out, w_o, ssm_w_glu, ssm_b_glu,
                    tm, blk)
        h2 = h.reshape(t, d)
        hn, idx, gate = _route(h2, norm2_g[l].reshape(1, d), peer_w_q[l].astype(BF16), _keys_cat(peer_keys[l]),
                               min(256, t))
        return h2, hn, idx.reshape(t * nk // SC_ROWS, SC_ROWS), gate.reshape(t * nk)

    tabs = [(_pack_bf16_pairs(peer_u[l]), _pack_bf16_pairs(peer_v[l])) for l in range(depth)]

    def expert_stage(st, l):
        h2, hn, idx4, gate = st
        return _peer_experts(hn, h2, idx4, gate, tabs[l][0], tabs[l][1]).reshape(bh, s, d)

    def after(value, st):
        value, _ = lax.optimization_barrier((value, st[3]))
        return value

    hs = [x[p * bh:(p + 1) * bh] for p in range(N_PARTS)]
    prev = None
    for l in range(depth):
        for p in range(N_PARTS):
            st = dense_stage(hs[p] if prev is None else after(hs[p], prev), l)
            hs[p] = expert_stage(st, l)
            prev = st
    outs = [_final_norm(h.reshape(t, d), norm_f.reshape(1, d), min(512, t)).reshape(bh, s, d) for h in hs]
    return jnp.concatenate(outs, axis=0)
```

```python
import functools
import math

import jax
import jax.numpy as jnp
from jax import lax
from jax.experimental import pallas as pl
from jax.experimental.pallas import tpu as pltpu
from jax.experimental.pallas import tpu_sc as plsc

F32 = jnp.float32
BF16 = jnp.bfloat16
I32 = jnp.int32

RMS_EPS = 1e-6
SSM_GROUP_CH = 16
SSM_STATE = 64
SSM_CHUNK = 128
HEAD_DIM = 64
N_HEADS = 8
AUG = 128
PEER_HEADS = 8
PEER_KEYS = 128
PEER_TOPK = 16
VMEM_LIMIT = 56 * 1024 * 1024
N_PARTS = 2


def _rms(x, g):
    return x * lax.rsqrt(jnp.mean(x * x, axis=-1, keepdims=True) + RMS_EPS) * g


def _gelu(x):
    c = math.sqrt(2.0 / math.pi)
    return 0.5 * x * (1.0 + jnp.tanh(c * (x + 0.044715 * (x * x * x))))


def _sigmoid(x):
    return 1.0 / (1.0 + jnp.exp(-x))


def _in_proj_kernel(h_ref, g_ref, wu_ref, wm_ref, wf_ref, bf_ref, u_ref, qa_ref, ka_ref, v_ref, cum_ref):
    j = pl.program_id(1)

    @pl.when(j == 0)
    def _():
        cum_ref[...] = jnp.zeros_like(cum_ref)

    x = h_ref[0]
    tm = x.shape[0]
    xn = _rms(x, g_ref[...])
    xb = xn.astype(BF16)
    proj = jnp.dot(xb, wm_ref[...], preferred_element_type=F32)
    u_ref[...] = lax.dot_general(wu_ref[...], xb, (((1,), (1,)), ((), ())),
                                 preferred_element_type=F32).astype(BF16)
    f = jnp.dot(xn, wf_ref[...], precision=lax.Precision.HIGHEST, preferred_element_type=F32) + bf_ref[...]
    logf = jnp.minimum(f, 0.0) - jnp.log(1.0 + jnp.exp(-jnp.abs(f)))
    row = lax.broadcasted_iota(I32, (tm, tm), 0)
    col = lax.broadcasted_iota(I32, (tm, tm), 1)
    tri = (row >= col).astype(F32)
    cum = jnp.dot(tri, logf, precision=lax.Precision.HIGHEST, preferred_element_type=F32) + cum_ref[0:1, :]
    cum_ref[0:1, :] = cum[tm - 1:tm, :]

    w = HEAD_DIM * N_HEADS
    lane = lax.broadcasted_iota(I32, (tm, AUG), 1)
    scale = HEAD_DIM ** -0.5
    for hh in range(N_HEADS):
        pair = hh // 2
        q2 = proj[:, 128 * pair: 128 * pair + 128]
        k2 = proj[:, w + 128 * pair: w + 128 * pair + 128]
        v2 = proj[:, 2 * w + 128 * pair: 2 * w + 128 * pair + 128]
        if hh % 2 == 1:
            q2 = pltpu.roll(q2, 64, axis=1)
            k2 = pltpu.roll(k2, 64, axis=1)
            vh = v2[:, 64:]
        else:
            vh = v2[:, :64]
        c = jnp.broadcast_to(cum[:, hh:hh + 1], (tm, AUG))
        c1 = c.astype(BF16).astype(F32)
        r1 = c - c1
        c2 = r1.astype(BF16).astype(F32)
        c3 = r1 - c2
        one = jnp.ones((tm, AUG), F32)
        zero = jnp.zeros((tm, AUG), F32)
        qa = jnp.where(lane < 64, q2 * scale,
             jnp.where(lane == 64, c1, jnp.where(lane == 65, c2, jnp.where(lane == 66, c3,
             jnp.where(lane < 70, one, zero)))))
        ka = jnp.where(lane < 64, k2,
             jnp.where(lane < 67, one, jnp.where(lane == 67, -c1, jnp.where(lane == 68, -c2,
             jnp.where(lane == 69, -c3, zero)))))
        qa_ref[0, hh] = qa.astype(BF16)
        ka_ref[0, hh] = ka.astype(BF16)
        v_ref[0, hh] = vh.astype(BF16)


def _in_proj(h, g, wu_t, wm, wf, bf, tm):
    b, s, d = h.shape
    w = HEAD_DIM * N_HEADS
    ns = s // tm
    return pl.pallas_call(
        _in_proj_kernel,
        grid=(b, ns),
        in_specs=[
            pl.BlockSpec((1, tm, d), lambda i, j: (i, j, 0)),
            pl.BlockSpec((1, d), lambda i, j: (0, 0)),
            pl.BlockSpec((w, d), lambda i, j: (0, 0)),
            pl.BlockSpec((d, 3 * w), lambda i, j: (0, 0)),
            pl.BlockSpec((d, 128), lambda i, j: (0, 0)),
            pl.BlockSpec((1, 128), lambda i, j: (0, 0)),
        ],
        out_specs=[
            pl.BlockSpec((w, tm), lambda i, j: (0, i * ns + j)),
            pl.BlockSpec((1, N_HEADS, tm, AUG), lambda i, j: (i, 0, j, 0)),
            pl.BlockSpec((1, N_HEADS, tm, AUG), lambda i, j: (i, 0, j, 0)),
            pl.BlockSpec((1, N_HEADS, tm, HEAD_DIM), lambda i, j: (i, 0, j, 0)),
        ],
        out_shape=[
            jax.ShapeDtypeStruct((w, b * s), BF16),
            jax.ShapeDtypeStruct((b, N_HEADS, s, AUG), BF16),
            jax.ShapeDtypeStruct((b, N_HEADS, s, AUG), BF16),
            jax.ShapeDtypeStruct((b, N_HEADS, s, HEAD_DIM), BF16),
        ],
        scratch_shapes=[pltpu.VMEM((8, 128), F32)],
        compiler_params=pltpu.CompilerParams(
            dimension_semantics=("parallel", "arbitrary"), vmem_limit_bytes=VMEM_LIMIT),
        cost_estimate=pl.CostEstimate(flops=2 * b * s * d * (4 * w + 128), transcendentals=2 * b * s * 128,
                                      bytes_accessed=4 * b * s * d + 2 * d * (4 * w) + 2 * b * s * (w + 2 * N_HEADS * AUG + w)),
        name="in_proj",
    )(h, g, wu_t, wm, wf, bf)


def _toeplitz_kernel(k_ref, m_ref):
    lc = SSM_CHUNK
    hc = SSM_GROUP_CH
    row = lax.broadcasted_iota(I32, (lc, lc), 0)
    col = lax.broadcasted_iota(I32, (lc, lc), 1)
    causal = col >= row

    def body(hi, carry):
        r0 = pl.multiple_of(hi * lc, lc)
        for ho in range(hc):
            k = jnp.broadcast_to(k_ref[0, pl.ds(hi * hc + ho, 1), :], (lc, lc))
            t = pltpu.roll(k, 0, 1, stride=1, stride_axis=0)
            m_ref[0, pl.ds(r0, lc), ho * lc:(ho + 1) * lc] = jnp.where(causal, t, 0.0).astype(BF16)
        return carry

    lax.fori_loop(0, hc, body, 0)


def _toeplitz(kq):
    g, hh, lc = kq.shape
    width = SSM_GROUP_CH * lc
    return pl.pallas_call(
        _toeplitz_kernel,
        grid=(g,),
        in_specs=[pl.BlockSpec((1, hh, lc), lambda i: (i, 0, 0))],
        out_specs=pl.BlockSpec((1, width, width), lambda i: (i, 0, 0)),
        out_shape=jax.ShapeDtypeStruct((g, width, width), BF16),
        compiler_params=pltpu.CompilerParams(dimension_semantics=("parallel",), vmem_limit_bytes=VMEM_LIMIT),
        cost_estimate=pl.CostEstimate(flops=g * width * width, transcendentals=0,
                                      bytes_accessed=2 * g * width * width + 4 * g * hh * lc),
        name="ssm_toeplitz",
    )(kq)


def _ssm_kernel(u_ref, m_ref, w_ref, r_ref, a_ref, d_ref, y_ref, e_scr, *, n_chunks, batch):
    hc = SSM_GROUP_CH
    lc = SSM_CHUNK
    u = jnp.concatenate([u_ref[0, r] for r in range(hc)], axis=1)
    y = jnp.dot(u, m_ref[0], preferred_element_type=F32)
    e_scr[...] = jnp.dot(u, w_ref[0], preferred_element_type=F32)
    ar = a_ref[0, 0:1, :]
    ai = a_ref[0, 1:2, :]

    def body(c, s):
        rows = pl.ds(c, batch, stride=n_chunks)
        e_c = e_scr[rows, :]
        e_scr[rows, :] = s
        return ar * s + ai * pltpu.roll(s, SSM_STATE, axis=1) + e_c

    lax.fori_loop(0, n_chunks, body, jnp.zeros((batch, 2 * SSM_STATE), F32))
    y = y + jnp.dot(e_scr[...].astype(BF16), r_ref[0], preferred_element_type=F32)
    y = y + u.astype(F32) * d_ref[0]
    for r in range(hc):
        y_ref[0, r] = y[:, r * lc:(r + 1) * lc]


def _ssm(u4, m, w, r, a, d, n_chunks, batch):
    g, hc, rows, lc = u4.shape
    width = hc * lc
    kern = functools.partial(_ssm_kernel, n_chunks=n_chunks, batch=batch)
    return pl.pallas_call(
        kern,
        grid=(g,),
        in_specs=[
            pl.BlockSpec((1, hc, rows, lc), lambda i: (i, 0, 0, 0)),
            pl.BlockSpec((1, width, width), lambda i: (i, 0, 0)),
            pl.BlockSpec((1, width, 2 * SSM_STATE), lambda i: (i, 0, 0)),
            pl.BlockSpec((1, 2 * SSM_STATE, width), lambda i: (i, 0, 0)),
            pl.BlockSpec((1, 8, 2 * SSM_STATE), lambda i: (i, 0, 0)),
            pl.BlockSpec((1, 1, width), lambda i: (i, 0, 0)),
        ],
        out_specs=pl.BlockSpec((1, hc, rows, lc), lambda i: (i, 0, 0, 0)),
        out_shape=jax.ShapeDtypeStruct((g, hc, rows, lc), F32),
        scratch_shapes=[pltpu.VMEM((rows, 2 * SSM_STATE), F32)],
        compiler_params=pltpu.CompilerParams(
            dimension_semantics=("parallel",), vmem_limit_bytes=VMEM_LIMIT),
        cost_estimate=pl.CostEstimate(flops=2 * g * rows * width * (width + 4 * SSM_STATE), transcendentals=0,
                                      bytes_accessed=g * (2 * width * (width + 4 * SSM_STATE) + 6 * rows * width)),
        name="ssm",
    )(u4, m, w, r, a, d)


def _ssm_tables(lam_re, lam_im, log_dt, b_re, b_im, c_re, c_im, d_skip):
    hp = lax.Precision.HIGHEST
    lc = SSM_CHUNK
    g, p = lam_re.shape
    hc = SSM_GROUP_CH
    dt = jnp.exp(log_dt)[:, None]
    a_re = jnp.exp(lam_re * dt) * jnp.cos(lam_im * dt)
    a_im = jnp.exp(lam_re * dt) * jnp.sin(lam_im * dt)
    den = lam_re * lam_re + lam_im * lam_im
    nr = a_re - 1.0
    z_re = (nr * lam_re + a_im * lam_im) / den
    z_im = (a_im * lam_re - nr * lam_im) / den
    bb_re = z_re[..., None] * b_re - z_im[..., None] * b_im
    bb_im = z_re[..., None] * b_im + z_im[..., None] * b_re
    tau = jnp.arange(lc + 1, dtype=F32)[:, None, None]
    mag = jnp.exp(tau * (lam_re * dt)[None])
    ang = tau * (lam_im * dt)[None]
    p_re = mag * jnp.cos(ang)
    p_im = mag * jnp.sin(ang)
    ab_re = p_re[:lc, :, :, None] * bb_re[None] - p_im[:lc, :, :, None] * bb_im[None]
    ab_im = p_re[:lc, :, :, None] * bb_im[None] + p_im[:lc, :, :, None] * bb_re[None]
    kq = (jnp.einsum('ghp,tgpk->gkht', c_re, ab_re, precision=hp)
          - jnp.einsum('ghp,tgpk->gkht', c_im, ab_im, precision=hp)).reshape(g, hc * hc, lc)
    m = _toeplitz(kq)
    w_re = ab_re[::-1].transpose(1, 3, 0, 2).reshape(g, hc * lc, p)
    w_im = ab_im[::-1].transpose(1, 3, 0, 2).reshape(g, hc * lc, p)
    w = jnp.concatenate([w_re, w_im], axis=-1)
    q_re = p_re[1:, :, None, :] * c_re[None] - p_im[1:, :, None, :] * c_im[None]
    q_im = p_re[1:, :, None, :] * c_im[None] + p_im[1:, :, None, :] * c_re[None]
    r = jnp.concatenate([q_re.transpose(1, 3, 2, 0).reshape(g, p, hc * lc),
                         -q_im.transpose(1, 3, 2, 0).reshape(g, p, hc * lc)], axis=1)
    al_re, al_im = p_re[lc], p_im[lc]
    a = jnp.zeros((g, 8, 2 * p), F32)
    a = a.at[:, 0, :].set(jnp.concatenate([al_re, al_re], axis=-1))
    a = a.at[:, 1, :].set(jnp.concatenate([-al_im, al_im], axis=-1))
    d = jnp.repeat(d_skip, lc, axis=-1).reshape(g, 1, hc * lc)
    return m, w.astype(BF16), r.astype(BF16), a, d


def _attn_kernel(q_ref, k_ref, v_ref, o_ref, m_ref, l_ref, acc_ref, *, blk):
    i = pl.program_id(1)
    q = q_ref[0]
    m_ref[...] = jnp.full_like(m_ref, -jnp.inf)
    l_ref[...] = jnp.zeros_like(l_ref)
    acc_ref[...] = jnp.zeros_like(acc_ref)

    def step(j, masked):
        off = pl.multiple_of(j * blk, blk)
        k = k_ref[0, pl.ds(off, blk), :]
        v = v_ref[0, pl.ds(off, blk), :]
        s = lax.dot_general(q, k, (((1,), (1,)), ((), ())), preferred_element_type=F32)
        if masked:
            row = lax.broadcasted_iota(I32, s.shape, 0)
            col = lax.broadcasted_iota(I32, s.shape, 1)
            s = jnp.where(row >= col, s, -jnp.inf)
        m_prev = m_ref[...]
        m_new = jnp.maximum(m_prev, jnp.max(s, axis=1, keepdims=True))
        p = jnp.exp(s - m_new)
        alpha = jnp.exp(m_prev - m_new)
        l_ref[...] = alpha * l_ref[...] + jnp.sum(p, axis=1, keepdims=True)
        acc_ref[...] = alpha * acc_ref[...] + jnp.dot(p.astype(BF16), v, preferred_element_type=F32)
        m_ref[...] = m_new

    def body(j, c):
        step(j, False)
        return c

    lax.fori_loop(0, i, body, 0)
    step(i, True)
    o_ref[0] = acc_ref[...] / l_ref[...]


def _attention(qa, ka, v, blk):
    bh, s, _ = qa.shape
    kern = functools.partial(_attn_kernel, blk=blk)
    return pl.pallas_call(
        kern,
        grid=(bh, s // blk),
        in_specs=[
            pl.BlockSpec((1, blk, AUG), lambda b, i: (b, i, 0)),
            pl.BlockSpec((1, s, AUG), lambda b, i: (b, 0, 0)),
            pl.BlockSpec((1, s, HEAD_DIM), lambda b, i: (b, 0, 0)),
        ],
        out_specs=pl.BlockSpec((1, blk, HEAD_DIM), lambda b, i: (b, i, 0)),
        out_shape=jax.ShapeDtypeStruct((bh, s, HEAD_DIM), F32),
        scratch_shapes=[pltpu.VMEM((blk, 1), F32), pltpu.VMEM((blk, 1), F32), pltpu.VMEM((blk, HEAD_DIM), F32)],
        compiler_params=pltpu.CompilerParams(
            dimension_semantics=("parallel", "arbitrary"), vmem_limit_bytes=VMEM_LIMIT),
        cost_estimate=pl.CostEstimate(flops=bh * s * s * (AUG + HEAD_DIM), transcendentals=bh * s * s // 2,
                                      bytes_accessed=bh * s * (2 * 2 * AUG + 2 * HEAD_DIM + 4 * HEAD_DIM)),
        name="fox_attn",
    )(qa, ka, v)


def _out_proj_kernel(y_ref, att_ref, h_ref, wg_ref, bg_ref, gs_ref, ga_ref, wos_ref, woa_ref, o_ref):
    g = _gelu(y_ref[...])
    z = jnp.dot(wg_ref[...], g.astype(BF16), preferred_element_type=F32) + bg_ref[...]
    o = g * _sigmoid(z)
    a = o * lax.rsqrt(jnp.mean(o * o, axis=0, keepdims=True) + RMS_EPS) * gs_ref[...]
    acc = h_ref[0] + lax.dot_general(a.astype(BF16), wos_ref[...], (((0,), (0,)), ((), ())),
                                     preferred_element_type=F32)
    ssq = jnp.zeros((acc.shape[0], 1), F32)
    for hh in range(N_HEADS):
        t = att_ref[0, hh]
        ssq = ssq + jnp.sum(t * t, axis=1, keepdims=True)
    inv = lax.rsqrt(ssq / (N_HEADS * HEAD_DIM) + RMS_EPS)
    for hh in range(N_HEADS):
        bh = att_ref[0, hh] * inv * ga_ref[hh]
        acc = acc + jnp.dot(bh.astype(BF16), woa_ref[hh], preferred_element_type=F32)
    o_ref[0] = acc


def _out_proj(y_t, y_att, h, wg_t, bg, gs, ga, wos, woa, tm):
    b, s, d = h.shape
    w = y_t.shape[0]
    ns = s // tm
    return pl.pallas_call(
        _out_proj_kernel,
        grid=(b, ns),
        in_specs=[
            pl.BlockSpec((w, tm), lambda i, j: (0, i * ns + j)),
            pl.BlockSpec((1, N_HEADS, tm, HEAD_DIM), lambda i, j: (i, 0, j, 0)),
            pl.BlockSpec((1, tm, d), lambda i, j: (i, j, 0)),
            pl.BlockSpec((w, w), lambda i, j: (0, 0)),
            pl.BlockSpec((w, 1), lambda i, j: (0, 0)),
            pl.BlockSpec((w, 1), lambda i, j: (0, 0)),
            pl.BlockSpec((N_HEADS, 1, HEAD_DIM), lambda i, j: (0, 0, 0)),
            pl.BlockSpec((w, d), lambda i, j: (0, 0)),
            pl.BlockSpec((N_HEADS, HEAD_DIM, d), lambda i, j: (0, 0, 0)),
        ],
        out_specs=pl.BlockSpec((1, tm, d), lambda i, j: (i, j, 0)),
        out_shape=jax.ShapeDtypeStruct((b, s, d), F32),
        compiler_params=pltpu.CompilerParams(
            dimension_semantics=("parallel", "parallel"), vmem_limit_bytes=VMEM_LIMIT),
        cost_estimate=pl.CostEstimate(flops=2 * b * s * (w * w + 2 * w * d), transcendentals=2 * b * s * w,
                                      bytes_accessed=b * s * (8 * w + 8 * d) + 2 * (w * w + 2 * w * d)),
        name="out_proj",
    )(y_t, y_att, h, wg_t, bg, gs, ga, wos, woa)


def _take_top(vals, payload, k):
    n_rows = vals.shape[0]
    rows = lax.broadcasted_iota(I32, vals.shape, 0)
    tops, picks = [], []
    for _ in range(k):
        m = jnp.max(vals, axis=0, keepdims=True)
        arg = jnp.min(jnp.where(vals == m, rows, n_rows), axis=0, keepdims=True)
        hit = rows == arg
        tops.append(m)
        picks.append(arg if payload is None else jnp.max(jnp.where(hit, payload, -1), axis=0, keepdims=True))
        vals = jnp.where(hit, -jnp.inf, vals)
    return jnp.concatenate(tops, axis=0), jnp.concatenate(picks, axis=0)


def _route_kernel(h_ref, g_ref, wq_ref, keys_ref, hn_ref, idx_ref, gate_ref):
    x = h_ref[...]
    hn = _rms(x, g_ref[...])
    hn_ref[...] = _pack_bf16_pairs(hn)
    q = jnp.dot(hn.astype(BF16), wq_ref[...], preferred_element_type=F32)
    k = PEER_TOPK
    idx_rows, gate_rows = [], []
    for hh in range(PEER_HEADS):
        qh = q[:, 128 * hh: 128 * (hh + 1)].astype(BF16)
        sc = lax.dot_general(keys_ref[hh], qh, (((1,), (1,)), ((), ())), preferred_element_type=F32)
        t1, i1 = _take_top(sc[:PEER_KEYS], None, k)
        t2, i2 = _take_top(sc[PEER_KEYS:], None, k)
        widths = [k // (a + 1) for a in range(k)]
        pad = -sum(widths) % 8
        cand = jnp.concatenate([t1[a:a + 1] + t2[:widths[a]] for a in range(k)]
                               + [jnp.full((pad, t1.shape[1]), -jnp.inf, F32)], axis=0)
        cidx = jnp.concatenate([i1[a:a + 1] * PEER_KEYS + i2[:widths[a]] for a in range(k)]
                               + [jnp.full((pad, t1.shape[1]), -1, I32)], axis=0)
        best, idx = _take_top(cand, cidx, k)
        e = jnp.exp(best - best[0:1])
        gate = e / jnp.sum(e, axis=0, keepdims=True)
        idx_rows.append(idx)
        gate_rows.append(gate)
    idx_ref[...] = jnp.concatenate(idx_rows, axis=0).T
    gate_ref[...] = jnp.concatenate(gate_rows, axis=0).T


def _route(h2, g, wq, keys_cat, tm):
    t, d = h2.shape
    nk = PEER_HEADS * PEER_TOPK
    return pl.pallas_call(
        _route_kernel,
        grid=(t // tm,),
        in_specs=[
            pl.BlockSpec((tm, d), lambda i: (i, 0)),
            pl.BlockSpec((1, d), lambda i: (0, 0)),
            pl.BlockSpec((d, PEER_HEADS * 128), lambda i: (0, 0)),
            pl.BlockSpec((PEER_HEADS, 2 * PEER_KEYS, 128), lambda i: (0, 0, 0)),
        ],
        out_specs=[
            pl.BlockSpec((tm, d // 2), lambda i: (i, 0)),
            pl.BlockSpec((tm, nk), lambda i: (i, 0)),
            pl.BlockSpec((tm, nk), lambda i: (i, 0)),
        ],
        out_shape=[
            jax.ShapeDtypeStruct((t, d // 2), I32),
            jax.ShapeDtypeStruct((t, nk), I32),
            jax.ShapeDtypeStruct((t, nk), F32),
        ],
        compiler_params=pltpu.CompilerParams(
            dimension_semantics=("parallel",), vmem_limit_bytes=VMEM_LIMIT),
        cost_estimate=pl.CostEstimate(flops=2 * t * d * PEER_HEADS * 128 + 2 * t * PEER_HEADS * 256 * 128,
                                      transcendentals=t * nk, bytes_accessed=t * (8 * d + 8 * nk) + 2 * d * PEER_HEADS * 128),
        name="peer_route",
    )(h2, g, wq, keys_cat)


def _final_norm_kernel(x_ref, g_ref, o_ref):
    o_ref[...] = _rms(x_ref[...], g_ref[...])


def _final_norm(x2, g, tm):
    t, d = x2.shape
    return pl.pallas_call(
        _final_norm_kernel,
        grid=(t // tm,),
        in_specs=[pl.BlockSpec((tm, d), lambda i: (i, 0)), pl.BlockSpec((1, d), lambda i: (0, 0))],
        out_specs=pl.BlockSpec((tm, d), lambda i: (i, 0)),
        out_shape=jax.ShapeDtypeStruct((t, d), F32),
        compiler_params=pltpu.CompilerParams(dimension_semantics=("parallel",)),
        cost_estimate=pl.CostEstimate(flops=4 * t * d, transcendentals=t, bytes_accessed=8 * t * d),
        name="final_norm",
    )(x2, g)


SC_WORKERS = 32
SC_CORES = 2
SC_LANES = 16
SC_TOK = 16
SC_ROWS = 16
SC_BUFS = 4
SC_SLAB = 8
HI16 = -65536


def _sc_params():
    cp = pltpu.CompilerParams()
    if "needs_layout_passes" in pltpu.CompilerParams.__dataclass_fields__:
        cp = pltpu.CompilerParams(needs_layout_passes=False)
    return cp


def _sc_worker_id():
    return lax.axis_index("s") * SC_CORES + lax.axis_index("c")


def _tree_sum(xs):
    xs = list(xs)
    while len(xs) > 1:
        xs = [xs[i] + xs[i + 1] for i in range(0, len(xs) - 1, 2)] + ([xs[-1]] if len(xs) % 2 else [])
    return xs[0]


def _gelu_via_exp(x):
    c = math.sqrt(2.0 / math.pi)
    z = c * (x + 0.044715 * (x * x * x))
    return 0.5 * x * (2.0 - 2.0 / (1.0 + jnp.exp(2.0 * z)))


def _round_to_bf16_bits(bits):
    return bits + 0x7FFF + (lax.shift_right_logical(bits, jnp.int32(16)) & 1)


def _pack_bf16_pairs(x):
    w = x.shape[-1] // 2
    r = _round_to_bf16_bits(lax.bitcast_convert_type(x, I32))
    return lax.shift_right_logical(r[..., :w], jnp.int32(16)) | (r[..., w:] & HI16)


def _lo_f32(word):
    return lax.bitcast_convert_type(lax.shift_left(word, jnp.int32(16)), F32)


def _hi_f32(word):
    return lax.bitcast_convert_type(word & HI16, F32)


def _peer_experts_body(hn_hbm, h_hbm, idx_hbm, gate_hbm, u_hbm, v_hbm, out_hbm,
                       h_v, o_v, idx_v, gate_v, coef_v, cw_v, acc_v, *bufs, tok_per_worker):
    rows, sems = bufs[:SC_BUFS], bufs[SC_BUFS:]
    half = hn_hbm.shape[1]
    nk = PEER_HEADS * PEER_TOPK
    per_tok = nk // SC_ROWS
    n = SC_TOK * per_tok
    n_slab = half // (SC_SLAB * SC_LANES)
    lane = lax.iota(I32, SC_LANES)
    wid = _sc_worker_id()

    def packed_mul(word, other_bf16):
        return plsc.bitcast(plsc.bitcast(word, BF16) * other_bf16, I32)

    def gather_u(k, slot):
        return pltpu.make_async_copy(u_hbm.at[idx_v.at[k]], rows[slot], sems[slot])

    def gather_v(k, slot):
        return pltpu.make_async_copy(v_hbm.at[idx_v.at[k]], rows[slot], sems[slot])

    def start_ahead(ahead, slot):
        @pl.when(ahead < n)
        def _():
            gather_u(ahead, slot).start()

        @pl.when(jnp.logical_and(ahead >= n, ahead < 2 * n))
        def _():
            gather_v(ahead - n, slot).start()

    def score(rw, k):
        t = k // per_tok
        for sl in range(n_slab):
            base = sl * SC_SLAB * SC_LANES
            hw = [plsc.bitcast(h_v[t, pl.ds(base + SC_LANES * j, SC_LANES)], BF16) for j in range(SC_SLAB)]

            @plsc.parallel_loop(0, SC_ROWS, 1)
            def _(r):
                terms = []
                for j in range(SC_SLAB):
                    p = packed_mul(rw[r, pl.ds(base + SC_LANES * j, SC_LANES)], hw[j])
                    terms += [_lo_f32(p), _hi_f32(p)]
                p = _tree_sum(terms)
                off = pl.multiple_of(r * SC_LANES, SC_LANES)
                if sl == 0:
                    acc_v[pl.ds(off, SC_LANES)] = p
                else:
                    acc_v[pl.ds(off, SC_LANES)] = acc_v[pl.ds(off, SC_LANES)] + p

        for rg in range(SC_ROWS // SC_LANES):
            base = rg * SC_LANES * SC_LANES
            tot = _tree_sum([plsc.load_gather(acc_v, [lane * SC_LANES + (base + j)]) for j in range(SC_LANES)])
            coef_v[pl.ds(pl.multiple_of(k * SC_ROWS + rg * SC_LANES, SC_LANES), SC_LANES)] = tot

    def combine(rw, k):
        t = k // per_tok
        for sl in range(n_slab):
            base = sl * SC_SLAB * SC_LANES
            acc0 = (tuple(o_v[t, pl.ds(base + SC_LANES * j, SC_LANES)] for j in range(SC_SLAB))
                    + tuple(o_v[t, pl.ds(half + base + SC_LANES * j, SC_LANES)] for j in range(SC_SLAB)))

            def rbody(r, acc):
                w = plsc.bitcast(plsc.load_gather(cw_v, [jnp.full((SC_LANES,), k * SC_ROWS + r, I32)]), BF16)
                new = list(acc)
                for j in range(SC_SLAB):
                    p = packed_mul(rw[r, pl.ds(base + SC_LANES * j, SC_LANES)], w)
                    new[j] = acc[j] + _lo_f32(p)
                    new[SC_SLAB + j] = acc[SC_SLAB + j] + _hi_f32(p)
                return tuple(new)

            acc = lax.fori_loop(0, SC_ROWS, rbody, acc0)
            for j in range(SC_SLAB):
                o_v[t, pl.ds(base + SC_LANES * j, SC_LANES)] = acc[j]
                o_v[t, pl.ds(half + base + SC_LANES * j, SC_LANES)] = acc[SC_SLAB + j]

    def batch(bi, carry):
        t0 = pl.multiple_of(wid * tok_per_worker + bi * SC_TOK, SC_TOK)
        pltpu.sync_copy(idx_hbm.at[pl.ds(t0 * per_tok, n)], idx_v)
        for s in range(SC_BUFS - 1):
            gather_u(s, s).start()
        pltpu.sync_copy(hn_hbm.at[pl.ds(t0, SC_TOK)], h_v)
        pltpu.sync_copy(h_hbm.at[pl.ds(t0, SC_TOK)], o_v)
        pltpu.sync_copy(gate_hbm.at[pl.ds(t0 * nk, SC_TOK * nk)], gate_v)

        def score_group(i, c):
            for s in range(SC_BUFS):
                k = i * SC_BUFS + s
                start_ahead(k + SC_BUFS - 1, (s + SC_BUFS - 1) % SC_BUFS)
                gather_u(k, s).wait()
                score(rows[s], k)
            return c

        lax.fori_loop(0, n // SC_BUFS, score_group, 0)

        @plsc.parallel_loop(0, SC_TOK * nk // SC_LANES, 1)
        def _(j):
            off = pl.multiple_of(j * SC_LANES, SC_LANES)
            c = _gelu_via_exp(coef_v[pl.ds(off, SC_LANES)]) * gate_v[pl.ds(off, SC_LANES)]
            hi = _round_to_bf16_bits(lax.bitcast_convert_type(c, I32)) & HI16
            cw_v[pl.ds(off, SC_LANES)] = hi | lax.shift_right_logical(hi, jnp.int32(16))

        def combine_group(i, c):
            for s in range(SC_BUFS):
                k = i * SC_BUFS + s
                start_ahead(n + k + SC_BUFS - 1, (s + SC_BUFS - 1) % SC_BUFS)
                gather_v(k, s).wait()
                combine(rows[s], k)
            return c

        lax.fori_loop(0, n // SC_BUFS, combine_group, 0)
        pltpu.sync_copy(o_v, out_hbm.at[pl.ds(t0, SC_TOK)])
        return carry

    lax.fori_loop(0, tok_per_worker // SC_TOK, batch, 0)


def _peer_experts(hn_words, h2, idx4, gate_flat, u_words, v_words):
    t, d = h2.shape
    nk = PEER_HEADS * PEER_TOPK
    mesh = plsc.VectorSubcoreMesh(core_axis_name="c", subcore_axis_name="s")
    body = functools.partial(_peer_experts_body, tok_per_worker=t // SC_WORKERS)
    return pl.kernel(
        body,
        out_type=jax.ShapeDtypeStruct((t, d), F32),
        mesh=mesh,
        scratch_types=[
            pltpu.VMEM((SC_TOK, d // 2), I32),
            pltpu.VMEM((SC_TOK, d), F32),
            pltpu.VMEM((SC_TOK * nk // SC_ROWS, SC_ROWS), I32),
            pltpu.VMEM((SC_TOK * nk,), F32),
            pltpu.VMEM((SC_TOK * nk,), F32),
            pltpu.VMEM((SC_TOK * nk,), I32),
            pltpu.VMEM((SC_ROWS * SC_LANES,), F32),
        ] + [pltpu.VMEM((SC_ROWS, d // 2), I32)] * SC_BUFS + [pltpu.SemaphoreType.DMA] * SC_BUFS,
        compiler_params=_sc_params(),
        cost_estimate=pl.CostEstimate(flops=4 * t * nk * d, transcendentals=t * nk,
                                      bytes_accessed=4 * t * nk * d + 10 * t * d),
        name="peer_experts_sc",
    )(hn_words, h2, idx4, gate_flat, u_words, v_words)


def _mixers(h, l, norm1_g, w_in, ssm, fox_b_f, g_ssm_out, g_attn_out, w_o, ssm_w_glu, ssm_b_glu, tm, blk):
    b, s, d = h.shape
    w = N_HEADS * HEAD_DIM
    wl = w_in[l]
    wu_t = wl[:, :w].T.astype(BF16)
    wm = wl[:, w:4 * w].astype(BF16)
    wf = jnp.pad(wl[:, 4 * w:], ((0, 0), (0, 128 - N_HEADS)))
    bf = jnp.pad(fox_b_f[l], (0, 128 - N_HEADS)).reshape(1, 128)
    u_t, qa, ka, v = _in_proj(h, norm1_g[l].reshape(1, d), wu_t, wm, wf, bf, tm)

    lc = SSM_CHUNK
    nc = s // lc
    g = w // SSM_GROUP_CH
    y4 = _ssm(u_t.reshape(g, SSM_GROUP_CH, b * nc, lc), *ssm, n_chunks=nc, batch=b)
    y_t = y4.reshape(w, b * s)

    y_att = _attention(qa.reshape(b * N_HEADS, s, AUG), ka.reshape(b * N_HEADS, s, AUG),
                       v.reshape(b * N_HEADS, s, HEAD_DIM), blk).reshape(b, N_HEADS, s, HEAD_DIM)

    wo = w_o[l].astype(BF16)
    return _out_proj(y_t, y_att, h, ssm_w_glu[l].T.astype(BF16), ssm_b_glu[l].reshape(w, 1),
                     g_ssm_out[l].reshape(w, 1), g_attn_out[l].reshape(N_HEADS, 1, HEAD_DIM),
                     wo[:w], wo[w:].reshape(N_HEADS, HEAD_DIM, d), tm)


def _keys_cat(keys_l):
    z = jnp.zeros_like(keys_l[:, 0])
    top = jnp.concatenate([keys_l[:, 0], z], axis=-1)
    bot = jnp.concatenate([z, keys_l[:, 1]], axis=-1)
    return jnp.concatenate([top, bot], axis=1).astype(BF16)


def kernel(x, norm1_g, w_in, ssm_lambda_re, ssm_lambda_im, ssm_log_dt, ssm_b_re, ssm_b_im, ssm_c_re, ssm_c_im, ssm_d, ssm_w_glu, ssm_b_glu, fox_b_f, g_ssm_out, g_attn_out, w_o, norm2_g, peer_w_q, peer_keys, peer_u, peer_v, norm_f):
    b, s, d = x.shape
    depth = w_in.shape[0]
    nk = PEER_HEADS * PEER_TOPK
    tm = min(512, s)
    blk = min(512, s)
    assert b % N_PARTS == 0
    bh = b // N_PARTS
    t = bh * s
    ssm_tabs = [_ssm_tables(ssm_lambda_re[l], ssm_lambda_im[l], ssm_log_dt[l], ssm_b_re[l], ssm_b_im[l],
                            ssm_c_re[l], ssm_c_im[l], ssm_d[l]) for l in range(depth)]

    def dense_stage(h, l, early=None):
        h = _mixers(h, l, norm1_g, w_in, ssm_tabs[l], fox_b_f, g_ssm_out, g_attn_out, w_o, ssm_w_glu, ssm_b_glu,
                    tm, blk)
        if early is not None:
            h, _ = lax.optimization_barrier((h, early))
        h2 = h.reshape(t, d)
        hn, idx, gate = _route(h2, norm2_g[l].reshape(1, d), peer_w_q[l].astype(BF16), _keys_cat(peer_keys[l]),
                               min(256, t))
        return h2, hn, idx.reshape(t * nk // SC_ROWS, SC_ROWS), gate.reshape(t * nk)

    tabs = [(_pack_bf16_pairs(peer_u[l]), _pack_bf16_pairs(peer_v[l])) for l in range(depth)]

    def expert_stage(st, l):
        h2, hn, idx4, gate = st
        return _peer_experts(hn, h2, idx4, gate, tabs[l][0], tabs[l][1]).reshape(bh, s, d)

    def after(value, st):
        value, _ = lax.optimization_barrier((value, st[3]))
        return value

    hs = [x[p * bh:(p + 1) * bh] for p in range(N_PARTS)]
    prev = None
    for l in range(depth):
        for p in range(N_PARTS):
            early = tabs[0] if prev is None else (tabs[l + 1] if p == N_PARTS - 1 and l + 1 < depth else None)
            st = dense_stage(hs[p] if prev is None else after(hs[p], prev), l, early)
            hs[p] = expert_stage(st, l)
            prev = st
    outs = [_final_norm(h.reshape(t, d), norm_f.reshape(1, d), min(512, t)).reshape(bh, s, d) for h in hs]
    return jnp.concatenate(outs, axis=0)
```

```python
import functools
import math

import jax
import jax.numpy as jnp
from jax import lax
from jax.experimental import pallas as pl
from jax.experimental.pallas import tpu as pltpu
from jax.experimental.pallas import tpu_sc as plsc

F32 = jnp.float32
BF16 = jnp.bfloat16
I32 = jnp.int32

RMS_EPS = 1e-6
SSM_GROUP_CH = 16
SSM_STATE = 64
SSM_CHUNK = 128
HEAD_DIM = 64
N_HEADS = 8
AUG = 128
PEER_HEADS = 8
PEER_KEYS = 128
PEER_TOPK = 16
VMEM_LIMIT = 56 * 1024 * 1024
N_PARTS = 2


def _rms(x, g):
    return x * lax.rsqrt(jnp.mean(x * x, axis=-1, keepdims=True) + RMS_EPS) * g


def _gelu(x):
    c = math.sqrt(2.0 / math.pi)
    return 0.5 * x * (1.0 + jnp.tanh(c * (x + 0.044715 * (x * x * x))))


def _sigmoid(x):
    return 1.0 / (1.0 + jnp.exp(-x))


def _in_proj_kernel(h_ref, g_ref, wu_ref, wm_ref, wf_ref, bf_ref, u_ref, qa_ref, ka_ref, v_ref, cum_ref):
    j = pl.program_id(1)

    @pl.when(j == 0)
    def _():
        cum_ref[...] = jnp.zeros_like(cum_ref)

    x = h_ref[0]
    tm = x.shape[0]
    xn = _rms(x, g_ref[...])
    xb = xn.astype(BF16)
    proj = jnp.dot(xb, wm_ref[...], preferred_element_type=F32)
    u_ref[...] = lax.dot_general(wu_ref[...], xb, (((1,), (1,)), ((), ())),
                                 preferred_element_type=F32).astype(BF16)
    f = jnp.dot(xn, wf_ref[...], precision=lax.Precision.HIGHEST, preferred_element_type=F32) + bf_ref[...]
    logf = jnp.minimum(f, 0.0) - jnp.log(1.0 + jnp.exp(-jnp.abs(f)))
    row = lax.broadcasted_iota(I32, (tm, tm), 0)
    col = lax.broadcasted_iota(I32, (tm, tm), 1)
    tri = (row >= col).astype(F32)
    cum = jnp.dot(tri, logf, precision=lax.Precision.HIGHEST, preferred_element_type=F32) + cum_ref[0:1, :]
    cum_ref[0:1, :] = cum[tm - 1:tm, :]

    w = HEAD_DIM * N_HEADS
    lane = lax.broadcasted_iota(I32, (tm, AUG), 1)
    scale = HEAD_DIM ** -0.5
    for hh in range(N_HEADS):
        pair = hh // 2
        q2 = proj[:, 128 * pair: 128 * pair + 128]
        k2 = proj[:, w + 128 * pair: w + 128 * pair + 128]
        v2 = proj[:, 2 * w + 128 * pair: 2 * w + 128 * pair + 128]
        if hh % 2 == 1:
            q2 = pltpu.roll(q2, 64, axis=1)
            k2 = pltpu.roll(k2, 64, axis=1)
            vh = v2[:, 64:]
        else:
            vh = v2[:, :64]
        c = jnp.broadcast_to(cum[:, hh:hh + 1], (tm, AUG))
        c1 = c.astype(BF16).astype(F32)
        r1 = c - c1
        c2 = r1.astype(BF16).astype(F32)
        c3 = r1 - c2
        one = jnp.ones((tm, AUG), F32)
        zero = jnp.zeros((tm, AUG), F32)
        qa = jnp.where(lane < 64, q2 * scale,
             jnp.where(lane == 64, c1, jnp.where(lane == 65, c2, jnp.where(lane == 66, c3,
             jnp.where(lane < 70, one, zero)))))
        ka = jnp.where(lane < 64, k2,
             jnp.where(lane < 67, one, jnp.where(lane == 67, -c1, jnp.where(lane == 68, -c2,
             jnp.where(lane == 69, -c3, zero)))))
        qa_ref[0, hh] = qa.astype(BF16)
        ka_ref[0, hh] = ka.astype(BF16)
        v_ref[0, hh] = vh.astype(BF16)


def _in_proj(h, g, wu_t, wm, wf, bf, tm):
    b, s, d = h.shape
    w = HEAD_DIM * N_HEADS
    ns = s // tm
    return pl.pallas_call(
        _in_proj_kernel,
        grid=(b, ns),
        in_specs=[
            pl.BlockSpec((1, tm, d), lambda i, j: (i, j, 0)),
            pl.BlockSpec((1, d), lambda i, j: (0, 0)),
            pl.BlockSpec((w, d), lambda i, j: (0, 0)),
            pl.BlockSpec((d, 3 * w), lambda i, j: (0, 0)),
            pl.BlockSpec((d, 128), lambda i, j: (0, 0)),
            pl.BlockSpec((1, 128), lambda i, j: (0, 0)),
        ],
        out_specs=[
            pl.BlockSpec((w, tm), lambda i, j: (0, i * ns + j)),
            pl.BlockSpec((1, N_HEADS, tm, AUG), lambda i, j: (i, 0, j, 0)),
            pl.BlockSpec((1, N_HEADS, tm, AUG), lambda i, j: (i, 0, j, 0)),
            pl.BlockSpec((1, N_HEADS, tm, HEAD_DIM), lambda i, j: (i, 0, j, 0)),
        ],
        out_shape=[
            jax.ShapeDtypeStruct((w, b * s), BF16),
            jax.ShapeDtypeStruct((b, N_HEADS, s, AUG), BF16),
            jax.ShapeDtypeStruct((b, N_HEADS, s, AUG), BF16),
            jax.ShapeDtypeStruct((b, N_HEADS, s, HEAD_DIM), BF16),
        ],
        scratch_shapes=[pltpu.VMEM((8, 128), F32)],
        compiler_params=pltpu.CompilerParams(
            dimension_semantics=("parallel", "arbitrary"), vmem_limit_bytes=VMEM_LIMIT),
        cost_estimate=pl.CostEstimate(flops=2 * b * s * d * (4 * w + 128), transcendentals=2 * b * s * 128,
                                      bytes_accessed=4 * b * s * d + 2 * d * (4 * w) + 2 * b * s * (w + 2 * N_HEADS * AUG + w)),
        name="in_proj",
    )(h, g, wu_t, wm, wf, bf)


def _toeplitz_kernel(k_ref, m_ref):
    lc = SSM_CHUNK
    hc = SSM_GROUP_CH
    row = lax.broadcasted_iota(I32, (lc, lc), 0)
    col = lax.broadcasted_iota(I32, (lc, lc), 1)
    causal = col >= row

    def body(hi, carry):
        r0 = pl.multiple_of(hi * lc, lc)
        for ho in range(hc):
            k = jnp.broadcast_to(k_ref[0, pl.ds(hi * hc + ho, 1), :], (lc, lc))
            t = pltpu.roll(k, 0, 1, stride=1, stride_axis=0)
            m_ref[0, pl.ds(r0, lc), ho * lc:(ho + 1) * lc] = jnp.where(causal, t, 0.0).astype(BF16)
        return carry

    lax.fori_loop(0, hc, body, 0)


def _toeplitz(kq):
    g, hh, lc = kq.shape
    width = SSM_GROUP_CH * lc
    return pl.pallas_call(
        _toeplitz_kernel,
        grid=(g,),
        in_specs=[pl.BlockSpec((1, hh, lc), lambda i: (i, 0, 0))],
        out_specs=pl.BlockSpec((1, width, width), lambda i: (i, 0, 0)),
        out_shape=jax.ShapeDtypeStruct((g, width, width), BF16),
        compiler_params=pltpu.CompilerParams(dimension_semantics=("parallel",), vmem_limit_bytes=VMEM_LIMIT),
        cost_estimate=pl.CostEstimate(flops=g * width * width, transcendentals=0,
                                      bytes_accessed=2 * g * width * width + 4 * g * hh * lc),
        name="ssm_toeplitz",
    )(kq)


def _ssm_kernel(u_ref, m_ref, w_ref, r_ref, a_ref, d_ref, y_ref, e_scr, *, n_chunks, batch):
    hc = SSM_GROUP_CH
    lc = SSM_CHUNK
    u = jnp.concatenate([u_ref[0, r] for r in range(hc)], axis=1)
    y = jnp.dot(u, m_ref[0], preferred_element_type=F32)
    e_scr[...] = jnp.dot(u, w_ref[0], preferred_element_type=F32)
    ar = a_ref[0, 0:1, :]
    ai = a_ref[0, 1:2, :]

    def body(c, s):
        rows = pl.ds(c, batch, stride=n_chunks)
        e_c = e_scr[rows, :]
        e_scr[rows, :] = s
        return ar * s + ai * pltpu.roll(s, SSM_STATE, axis=1) + e_c

    lax.fori_loop(0, n_chunks, body, jnp.zeros((batch, 2 * SSM_STATE), F32))
    y = y + jnp.dot(e_scr[...].astype(BF16), r_ref[0], preferred_element_type=F32)
    y = y + u.astype(F32) * d_ref[0]
    for r in range(hc):
        y_ref[0, r] = y[:, r * lc:(r + 1) * lc]


def _ssm(u4, m, w, r, a, d, n_chunks, batch):
    g, hc, rows, lc = u4.shape
    width = hc * lc
    kern = functools.partial(_ssm_kernel, n_chunks=n_chunks, batch=batch)
    return pl.pallas_call(
        kern,
        grid=(g,),
        in_specs=[
            pl.BlockSpec((1, hc, rows, lc), lambda i: (i, 0, 0, 0)),
            pl.BlockSpec((1, width, width), lambda i: (i, 0, 0)),
            pl.BlockSpec((1, width, 2 * SSM_STATE), lambda i: (i, 0, 0)),
            pl.BlockSpec((1, 2 * SSM_STATE, width), lambda i: (i, 0, 0)),
            pl.BlockSpec((1, 8, 2 * SSM_STATE), lambda i: (i, 0, 0)),
            pl.BlockSpec((1, 1, width), lambda i: (i, 0, 0)),
        ],
        out_specs=pl.BlockSpec((1, hc, rows, lc), lambda i: (i, 0, 0, 0)),
        out_shape=jax.ShapeDtypeStruct((g, hc, rows, lc), F32),
        scratch_shapes=[pltpu.VMEM((rows, 2 * SSM_STATE), F32)],
        compiler_params=pltpu.CompilerParams(
            dimension_semantics=("parallel",), vmem_limit_bytes=VMEM_LIMIT),
        cost_estimate=pl.CostEstimate(flops=2 * g * rows * width * (width + 4 * SSM_STATE), transcendentals=0,
                                      bytes_accessed=g * (2 * width * (width + 4 * SSM_STATE) + 6 * rows * width)),
        name="ssm",
    )(u4, m, w, r, a, d)


def _ssm_tables(lam_re, lam_im, log_dt, b_re, b_im, c_re, c_im, d_skip):
    hp = lax.Precision.HIGHEST
    lc = SSM_CHUNK
    g, p = lam_re.shape
    hc = SSM_GROUP_CH
    dt = jnp.exp(log_dt)[:, None]
    a_re = jnp.exp(lam_re * dt) * jnp.cos(lam_im * dt)
    a_im = jnp.exp(lam_re * dt) * jnp.sin(lam_im * dt)
    den = lam_re * lam_re + lam_im * lam_im
    nr = a_re - 1.0
    z_re = (nr * lam_re + a_im * lam_im) / den
    z_im = (a_im * lam_re - nr * lam_im) / den
    bb_re = z_re[..., None] * b_re - z_im[..., None] * b_im
    bb_im = z_re[..., None] * b_im + z_im[..., None] * b_re
    tau = jnp.arange(lc + 1, dtype=F32)[:, None, None]
    mag = jnp.exp(tau * (lam_re * dt)[None])
    ang = tau * (lam_im * dt)[None]
    p_re = mag * jnp.cos(ang)
    p_im = mag * jnp.sin(ang)
    ab_re = p_re[:lc, :, :, None] * bb_re[None] - p_im[:lc, :, :, None] * bb_im[None]
    ab_im = p_re[:lc, :, :, None] * bb_im[None] + p_im[:lc, :, :, None] * bb_re[None]
    kq = (jnp.einsum('ghp,tgpk->gkht', c_re, ab_re, precision=hp)
          - jnp.einsum('ghp,tgpk->gkht', c_im, ab_im, precision=hp)).reshape(g, hc * hc, lc)
    m = _toeplitz(kq)
    w_re = ab_re[::-1].transpose(1, 3, 0, 2).reshape(g, hc * lc, p)
    w_im = ab_im[::-1].transpose(1, 3, 0, 2).reshape(g, hc * lc, p)
    w = jnp.concatenate([w_re, w_im], axis=-1)
    q_re = p_re[1:, :, None, :] * c_re[None] - p_im[1:, :, None, :] * c_im[None]
    q_im = p_re[1:, :, None, :] * c_im[None] + p_im[1:, :, None, :] * c_re[None]
    r = jnp.concatenate([q_re.transpose(1, 3, 2, 0).reshape(g, p, hc * lc),
                         -q_im.transpose(1, 3, 2, 0).reshape(g, p, hc * lc)], axis=1)
    al_re, al_im = p_re[lc], p_im[lc]
    a = jnp.zeros((g, 8, 2 * p), F32)
    a = a.at[:, 0, :].set(jnp.concatenate([al_re, al_re], axis=-1))
    a = a.at[:, 1, :].set(jnp.concatenate([-al_im, al_im], axis=-1))
    d = jnp.repeat(d_skip, lc, axis=-1).reshape(g, 1, hc * lc)
    return m, w.astype(BF16), r.astype(BF16), a, d


def _attn_kernel(q_ref, k_ref, v_ref, o_ref, m_ref, l_ref, acc_ref, *, blk):
    i = pl.program_id(1)
    q = q_ref[0]
    m_ref[...] = jnp.full_like(m_ref, -jnp.inf)
    l_ref[...] = jnp.zeros_like(l_ref)
    acc_ref[...] = jnp.zeros_like(acc_ref)

    def step(j, masked):
        off = pl.multiple_of(j * blk, blk)
        k = k_ref[0, pl.ds(off, blk), :]
        v = v_ref[0, pl.ds(off, blk), :]
        s = lax.dot_general(q, k, (((1,), (1,)), ((), ())), preferred_element_type=F32)
        if masked:
            row = lax.broadcasted_iota(I32, s.shape, 0)
            col = lax.broadcasted_iota(I32, s.shape, 1)
            s = jnp.where(row >= col, s, -jnp.inf)
        m_prev = m_ref[...]
        m_new = jnp.maximum(m_prev, jnp.max(s, axis=1, keepdims=True))
        p = jnp.exp(s - m_new)
        alpha = jnp.exp(m_prev - m_new)
        l_ref[...] = alpha * l_ref[...] + jnp.sum(p, axis=1, keepdims=True)
        acc_ref[...] = alpha * acc_ref[...] + jnp.dot(p.astype(BF16), v, preferred_element_type=F32)
        m_ref[...] = m_new

    def body(j, c):
        step(j, False)
        return c

    lax.fori_loop(0, i, body, 0)
    step(i, True)
    o_ref[0] = acc_ref[...] / l_ref[...]


def _attention(qa, ka, v, blk):
    bh, s, _ = qa.shape
    kern = functools.partial(_attn_kernel, blk=blk)
    return pl.pallas_call(
        kern,
        grid=(bh, s // blk),
        in_specs=[
            pl.BlockSpec((1, blk, AUG), lambda b, i: (b, i, 0)),
            pl.BlockSpec((1, s, AUG), lambda b, i: (b, 0, 0)),
            pl.BlockSpec((1, s, HEAD_DIM), lambda b, i: (b, 0, 0)),
        ],
        out_specs=pl.BlockSpec((1, blk, HEAD_DIM), lambda b, i: (b, i, 0)),
        out_shape=jax.ShapeDtypeStruct((bh, s, HEAD_DIM), F32),
        scratch_shapes=[pltpu.VMEM((blk, 1), F32), pltpu.VMEM((blk, 1), F32), pltpu.VMEM((blk, HEAD_DIM), F32)],
        compiler_params=pltpu.CompilerParams(
            dimension_semantics=("parallel", "arbitrary"), vmem_limit_bytes=VMEM_LIMIT),
        cost_estimate=pl.CostEstimate(flops=bh * s * s * (AUG + HEAD_DIM), transcendentals=bh * s * s // 2,
                                      bytes_accessed=bh * s * (2 * 2 * AUG + 2 * HEAD_DIM + 4 * HEAD_DIM)),
        name="fox_attn",
    )(qa, ka, v)


def _out_proj_kernel(y_ref, att_ref, h_ref, wg_ref, bg_ref, gs_ref, ga_ref, wos_ref, woa_ref, o_ref):
    g = _gelu(y_ref[...])
    z = jnp.dot(wg_ref[...], g.astype(BF16), preferred_element_type=F32) + bg_ref[...]
    o = g * _sigmoid(z)
    a = o * lax.rsqrt(jnp.mean(o * o, axis=0, keepdims=True) + RMS_EPS) * gs_ref[...]
    acc = h_ref[0] + lax.dot_general(a.astype(BF16), wos_ref[...], (((0,), (0,)), ((), ())),
                                     preferred_element_type=F32)
    ssq = jnp.zeros((acc.shape[0], 1), F32)
    for hh in range(N_HEADS):
        t = att_ref[0, hh]
        ssq = ssq + jnp.sum(t * t, axis=1, keepdims=True)
    inv = lax.rsqrt(ssq / (N_HEADS * HEAD_DIM) + RMS_EPS)
    for hh in range(N_HEADS):
        bh = att_ref[0, hh] * inv * ga_ref[hh]
        acc = acc + jnp.dot(bh.astype(BF16), woa_ref[hh], preferred_element_type=F32)
    o_ref[0] = acc


def _out_proj(y_t, y_att, h, wg_t, bg, gs, ga, wos, woa, tm):
    b, s, d = h.shape
    w = y_t.shape[0]
    ns = s // tm
    return pl.pallas_call(
        _out_proj_kernel,
        grid=(b, ns),
        in_specs=[
            pl.BlockSpec((w, tm), lambda i, j: (0, i * ns + j)),
            pl.BlockSpec((1, N_HEADS, tm, HEAD_DIM), lambda i, j: (i, 0, j, 0)),
            pl.BlockSpec((1, tm, d), lambda i, j: (i, j, 0)),
            pl.BlockSpec((w, w), lambda i, j: (0, 0)),
            pl.BlockSpec((w, 1), lambda i, j: (0, 0)),
            pl.BlockSpec((w, 1), lambda i, j: (0, 0)),
            pl.BlockSpec((N_HEADS, 1, HEAD_DIM), lambda i, j: (0, 0, 0)),
            pl.BlockSpec((w, d), lambda i, j: (0, 0)),
            pl.BlockSpec((N_HEADS, HEAD_DIM, d), lambda i, j: (0, 0, 0)),
        ],
        out_specs=pl.BlockSpec((1, tm, d), lambda i, j: (i, j, 0)),
        out_shape=jax.ShapeDtypeStruct((b, s, d), F32),
        compiler_params=pltpu.CompilerParams(
            dimension_semantics=("parallel", "parallel"), vmem_limit_bytes=VMEM_LIMIT),
        cost_estimate=pl.CostEstimate(flops=2 * b * s * (w * w + 2 * w * d), transcendentals=2 * b * s * w,
                                      bytes_accessed=b * s * (8 * w + 8 * d) + 2 * (w * w + 2 * w * d)),
        name="out_proj",
    )(y_t, y_att, h, wg_t, bg, gs, ga, wos, woa)


def _take_top(vals, payload, k):
    n_rows = vals.shape[0]
    rows = lax.broadcasted_iota(I32, vals.shape, 0)
    tops, picks = [], []
    for _ in range(k):
        m = jnp.max(vals, axis=0, keepdims=True)
        arg = jnp.min(jnp.where(vals == m, rows, n_rows), axis=0, keepdims=True)
        hit = rows == arg
        tops.append(m)
        picks.append(arg if payload is None else jnp.max(jnp.where(hit, payload, -1), axis=0, keepdims=True))
        vals = jnp.where(hit, -jnp.inf, vals)
    return jnp.concatenate(tops, axis=0), jnp.concatenate(picks, axis=0)


def _route_kernel(h_ref, g_ref, wq_ref, keys_ref, hn_ref, idx_ref, gate_ref):
    x = h_ref[...]
    hn = _rms(x, g_ref[...])
    hn_ref[...] = _pack_bf16_pairs(hn)
    q = jnp.dot(hn.astype(BF16), wq_ref[...], preferred_element_type=F32)
    k = PEER_TOPK
    idx_rows, gate_rows = [], []
    for hh in range(PEER_HEADS):
        qh = q[:, 128 * hh: 128 * (hh + 1)].astype(BF16)
        sc = lax.dot_general(keys_ref[hh], qh, (((1,), (1,)), ((), ())), preferred_element_type=F32)
        t1, i1 = _take_top(sc[:PEER_KEYS], None, k)
        t2, i2 = _take_top(sc[PEER_KEYS:], None, k)
        widths = [k // (a + 1) for a in range(k)]
        pad = -sum(widths) % 8
        cand = jnp.concatenate([t1[a:a + 1] + t2[:widths[a]] for a in range(k)]
                               + [jnp.full((pad, t1.shape[1]), -jnp.inf, F32)], axis=0)
        cidx = jnp.concatenate([i1[a:a + 1] * PEER_KEYS + i2[:widths[a]] for a in range(k)]
                               + [jnp.full((pad, t1.shape[1]), -1, I32)], axis=0)
        best, idx = _take_top(cand, cidx, k)
        e = jnp.exp(best - best[0:1])
        gate = e / jnp.sum(e, axis=0, keepdims=True)
        idx_rows.append(idx)
        gate_rows.append(gate)
    idx_ref[...] = jnp.concatenate(idx_rows, axis=0).T
    gate_ref[...] = jnp.concatenate(gate_rows, axis=0).T


def _route(h2, g, wq, keys_cat, tm):
    t, d = h2.shape
    nk = PEER_HEADS * PEER_TOPK
    return pl.pallas_call(
        _route_kernel,
        grid=(t // tm,),
        in_specs=[
            pl.BlockSpec((tm, d), lambda i: (i, 0)),
            pl.BlockSpec((1, d), lambda i: (0, 0)),
            pl.BlockSpec((d, PEER_HEADS * 128), lambda i: (0, 0)),
            pl.BlockSpec((PEER_HEADS, 2 * PEER_KEYS, 128), lambda i: (0, 0, 0)),
        ],
        out_specs=[
            pl.BlockSpec((tm, d // 2), lambda i: (i, 0)),
            pl.BlockSpec((tm, nk), lambda i: (i, 0)),
            pl.BlockSpec((tm, nk), lambda i: (i, 0)),
        ],
        out_shape=[
            jax.ShapeDtypeStruct((t, d // 2), I32),
            jax.ShapeDtypeStruct((t, nk), I32),
            jax.ShapeDtypeStruct((t, nk), F32),
        ],
        compiler_params=pltpu.CompilerParams(
            dimension_semantics=("parallel",), vmem_limit_bytes=VMEM_LIMIT),
        cost_estimate=pl.CostEstimate(flops=2 * t * d * PEER_HEADS * 128 + 2 * t * PEER_HEADS * 256 * 128,
                                      transcendentals=t * nk, bytes_accessed=t * (8 * d + 8 * nk) + 2 * d * PEER_HEADS * 128),
        name="peer_route",
    )(h2, g, wq, keys_cat)


def _final_norm_kernel(x_ref, g_ref, o_ref):
    o_ref[...] = _rms(x_ref[...], g_ref[...])


def _final_norm(x2, g, tm):
    t, d = x2.shape
    return pl.pallas_call(
        _final_norm_kernel,
        grid=(t // tm,),
        in_specs=[pl.BlockSpec((tm, d), lambda i: (i, 0)), pl.BlockSpec((1, d), lambda i: (0, 0))],
        out_specs=pl.BlockSpec((tm, d), lambda i: (i, 0)),
        out_shape=jax.ShapeDtypeStruct((t, d), F32),
        compiler_params=pltpu.CompilerParams(dimension_semantics=("parallel",)),
        cost_estimate=pl.CostEstimate(flops=4 * t * d, transcendentals=t, bytes_accessed=8 * t * d),
        name="final_norm",
    )(x2, g)


SC_WORKERS = 32
SC_CORES = 2
SC_LANES = 16
SC_TOK = 16
SC_ROWS = 16
SC_BUFS = 4
SC_SLAB = 8
HI16 = -65536


def _sc_params():
    cp = pltpu.CompilerParams()
    if "needs_layout_passes" in pltpu.CompilerParams.__dataclass_fields__:
        cp = pltpu.CompilerParams(needs_layout_passes=False)
    return cp


def _sc_worker_id():
    return lax.axis_index("s") * SC_CORES + lax.axis_index("c")


def _tree_sum(xs):
    xs = list(xs)
    while len(xs) > 1:
        xs = [xs[i] + xs[i + 1] for i in range(0, len(xs) - 1, 2)] + ([xs[-1]] if len(xs) % 2 else [])
    return xs[0]


def _gelu_via_exp(x):
    c = math.sqrt(2.0 / math.pi)
    z = c * (x + 0.044715 * (x * x * x))
    return 0.5 * x * (2.0 - 2.0 / (1.0 + jnp.exp(2.0 * z)))


def _round_to_bf16_bits(bits):
    return bits + 0x7FFF + (lax.shift_right_logical(bits, jnp.int32(16)) & 1)


def _pack_bf16_pairs(x):
    w = x.shape[-1] // 2
    r = _round_to_bf16_bits(lax.bitcast_convert_type(x, I32))
    return lax.shift_right_logical(r[..., :w], jnp.int32(16)) | (r[..., w:] & HI16)


def _lo_f32(word):
    return lax.bitcast_convert_type(lax.shift_left(word, jnp.int32(16)), F32)


def _hi_f32(word):
    return lax.bitcast_convert_type(word & HI16, F32)


def _peer_experts_body(hn_hbm, h_hbm, idx_hbm, gate_hbm, u_hbm, v_hbm, out_hbm,
                       h_v, o_v, idx_v, gate_v, coef_v, cw_v, acc_v, *bufs, tok_per_worker):
    rows, sems = bufs[:SC_BUFS], bufs[SC_BUFS:]
    half = hn_hbm.shape[1]
    nk = PEER_HEADS * PEER_TOPK
    per_tok = nk // SC_ROWS
    n = SC_TOK * per_tok
    n_slab = half // (SC_SLAB * SC_LANES)
    lane = lax.iota(I32, SC_LANES)
    wid = _sc_worker_id()

    def packed_mul(word, other_bf16):
        return plsc.bitcast(plsc.bitcast(word, BF16) * other_bf16, I32)

    def gather_u(k, slot):
        return pltpu.make_async_copy(u_hbm.at[idx_v.at[k]], rows[slot], sems[slot])

    def gather_v(k, slot):
        return pltpu.make_async_copy(v_hbm.at[idx_v.at[k]], rows[slot], sems[slot])

    def start_ahead(ahead, slot):
        @pl.when(ahead < n)
        def _():
            gather_u(ahead, slot).start()

        @pl.when(jnp.logical_and(ahead >= n, ahead < 2 * n))
        def _():
            gather_v(ahead - n, slot).start()

    def score(rw, k):
        t = k // per_tok
        for sl in range(n_slab):
            base = sl * SC_SLAB * SC_LANES
            hw = [plsc.bitcast(h_v[t, pl.ds(base + SC_LANES * j, SC_LANES)], BF16) for j in range(SC_SLAB)]

            @plsc.parallel_loop(0, SC_ROWS, 1)
            def _(r):
                terms = []
                for j in range(SC_SLAB):
                    p = packed_mul(rw[r, pl.ds(base + SC_LANES * j, SC_LANES)], hw[j])
                    terms += [_lo_f32(p), _hi_f32(p)]
                p = _tree_sum(terms)
                off = pl.multiple_of(r * SC_LANES, SC_LANES)
                if sl == 0:
                    acc_v[pl.ds(off, SC_LANES)] = p
                else:
                    acc_v[pl.ds(off, SC_LANES)] = acc_v[pl.ds(off, SC_LANES)] + p

        for rg in range(SC_ROWS // SC_LANES):
            base = rg * SC_LANES * SC_LANES
            tot = _tree_sum([plsc.load_gather(acc_v, [lane * SC_LANES + (base + j)]) for j in range(SC_LANES)])
            coef_v[pl.ds(pl.multiple_of(k * SC_ROWS + rg * SC_LANES, SC_LANES), SC_LANES)] = tot

    def combine(rw, k):
        t = k // per_tok
        for sl in range(n_slab):
            base = sl * SC_SLAB * SC_LANES
            acc0 = (tuple(o_v[t, pl.ds(base + SC_LANES * j, SC_LANES)] for j in range(SC_SLAB))
                    + tuple(o_v[t, pl.ds(half + base + SC_LANES * j, SC_LANES)] for j in range(SC_SLAB)))

            def rbody(r, acc):
                w = plsc.bitcast(plsc.load_gather(cw_v, [jnp.full((SC_LANES,), k * SC_ROWS + r, I32)]), BF16)
                new = list(acc)
                for j in range(SC_SLAB):
                    p = packed_mul(rw[r, pl.ds(base + SC_LANES * j, SC_LANES)], w)
                    new[j] = acc[j] + _lo_f32(p)
                    new[SC_SLAB + j] = acc[SC_SLAB + j] + _hi_f32(p)
                return tuple(new)

            acc = lax.fori_loop(0, SC_ROWS, rbody, acc0)
            for j in range(SC_SLAB):
                o_v[t, pl.ds(base + SC_LANES * j, SC_LANES)] = acc[j]
                o_v[t, pl.ds(half + base + SC_LANES * j, SC_LANES)] = acc[SC_SLAB + j]

    def batch(bi, carry):
        t0 = pl.multiple_of(wid * tok_per_worker + bi * SC_TOK, SC_TOK)
        pltpu.sync_copy(idx_hbm.at[pl.ds(t0 * per_tok, n)], idx_v)
        for s in range(SC_BUFS - 1):
            gather_u(s, s).start()
        pltpu.sync_copy(hn_hbm.at[pl.ds(t0, SC_TOK)], h_v)
        pltpu.sync_copy(h_hbm.at[pl.ds(t0, SC_TOK)], o_v)
        pltpu.sync_copy(gate_hbm.at[pl.ds(t0 * nk, SC_TOK * nk)], gate_v)

        def score_group(i, c):
            for s in range(SC_BUFS):
                k = i * SC_BUFS + s
                start_ahead(k + SC_BUFS - 1, (s + SC_BUFS - 1) % SC_BUFS)
                gather_u(k, s).wait()
                score(rows[s], k)
            return c

        lax.fori_loop(0, n // SC_BUFS, score_group, 0)

        @plsc.parallel_loop(0, SC_TOK * nk // SC_LANES, 1)
        def _(j):
            off = pl.multiple_of(j * SC_LANES, SC_LANES)
            c = _gelu_via_exp(coef_v[pl.ds(off, SC_LANES)]) * gate_v[pl.ds(off, SC_LANES)]
            hi = _round_to_bf16_bits(lax.bitcast_convert_type(c, I32)) & HI16
            cw_v[pl.ds(off, SC_LANES)] = hi | lax.shift_right_logical(hi, jnp.int32(16))

        def combine_group(i, c):
            for s in range(SC_BUFS):
                k = i * SC_BUFS + s
                start_ahead(n + k + SC_BUFS - 1, (s + SC_BUFS - 1) % SC_BUFS)
                gather_v(k, s).wait()
                combine(rows[s], k)
            return c

        lax.fori_loop(0, n // SC_BUFS, combine_group, 0)
        pltpu.sync_copy(o_v, out_hbm.at[pl.ds(t0, SC_TOK)])
        return carry

    lax.fori_loop(0, tok_per_worker // SC_TOK, batch, 0)


def _peer_experts(hn_words, h2, idx4, gate_flat, u_words, v_words):
    t, d = h2.shape
    nk = PEER_HEADS * PEER_TOPK
    mesh = plsc.VectorSubcoreMesh(core_axis_name="c", subcore_axis_name="s")
    body = functools.partial(_peer_experts_body, tok_per_worker=t // SC_WORKERS)
    return pl.kernel(
        body,
        out_type=jax.ShapeDtypeStruct((t, d), F32),
        mesh=mesh,
        scratch_types=[
            pltpu.VMEM((SC_TOK, d // 2), I32),
            pltpu.VMEM((SC_TOK, d), F32),
            pltpu.VMEM((SC_TOK * nk // SC_ROWS, SC_ROWS), I32),
            pltpu.VMEM((SC_TOK * nk,), F32),
            pltpu.VMEM((SC_TOK * nk,), F32),
            pltpu.VMEM((SC_TOK * nk,), I32),
            pltpu.VMEM((SC_ROWS * SC_LANES,), F32),
        ] + [pltpu.VMEM((SC_ROWS, d // 2), I32)] * SC_BUFS + [pltpu.SemaphoreType.DMA] * SC_BUFS,
        compiler_params=_sc_params(),
        cost_estimate=pl.CostEstimate(flops=4 * t * nk * d, transcendentals=t * nk,
                                      bytes_accessed=4 * t * nk * d + 10 * t * d),
        name="peer_experts_sc",
    )(hn_words, h2, idx4, gate_flat, u_words, v_words)


def _mixers(h, l, norm1_g, w_in, ssm, fox_b_f, g_ssm_out, g_attn_out, w_o, ssm_w_glu, ssm_b_glu, tm, blk):
    b, s, d = h.shape
    w = N_HEADS * HEAD_DIM
    wl = w_in[l]
    wu_t = wl[:, :w].T.astype(BF16)
    wm = wl[:, w:4 * w].astype(BF16)
    wf = jnp.pad(wl[:, 4 * w:], ((0, 0), (0, 128 - N_HEADS)))
    bf = jnp.pad(fox_b_f[l], (0, 128 - N_HEADS)).reshape(1, 128)
    u_t, qa, ka, v = _in_proj(h, norm1_g[l].reshape(1, d), wu_t, wm, wf, bf, tm)

    lc = SSM_CHUNK
    nc = s // lc
    g = w // SSM_GROUP_CH
    y4 = _ssm(u_t.reshape(g, SSM_GROUP_CH, b * nc, lc), *ssm, n_chunks=nc, batch=b)
    y_t = y4.reshape(w, b * s)

    y_att = _attention(qa.reshape(b * N_HEADS, s, AUG), ka.reshape(b * N_HEADS, s, AUG),
                       v.reshape(b * N_HEADS, s, HEAD_DIM), blk).reshape(b, N_HEADS, s, HEAD_DIM)

    wo = w_o[l].astype(BF16)
    return _out_proj(y_t, y_att, h, ssm_w_glu[l].T.astype(BF16), ssm_b_glu[l].reshape(w, 1),
                     g_ssm_out[l].reshape(w, 1), g_attn_out[l].reshape(N_HEADS, 1, HEAD_DIM),
                     wo[:w], wo[w:].reshape(N_HEADS, HEAD_DIM, d), tm)


def _keys_cat(keys_l):
    z = jnp.zeros_like(keys_l[:, 0])
    top = jnp.concatenate([keys_l[:, 0], z], axis=-1)
    bot = jnp.concatenate([z, keys_l[:, 1]], axis=-1)
    return jnp.concatenate([top, bot], axis=1).astype(BF16)


def kernel(x, norm1_g, w_in, ssm_lambda_re, ssm_lambda_im, ssm_log_dt, ssm_b_re, ssm_b_im, ssm_c_re, ssm_c_im, ssm_d, ssm_w_glu, ssm_b_glu, fox_b_f, g_ssm_out, g_attn_out, w_o, norm2_g, peer_w_q, peer_keys, peer_u, peer_v, norm_f):
    b, s, d = x.shape
    depth = w_in.shape[0]
    nk = PEER_HEADS * PEER_TOPK
    tm = min(512, s)
    blk = min(512, s)
    assert b % (2 * N_PARTS) == 0
    bh = b // N_PARTS
    ssm_tabs = [_ssm_tables(ssm_lambda_re[l], ssm_lambda_im[l], ssm_log_dt[l], ssm_b_re[l], ssm_b_im[l],
                            ssm_c_re[l], ssm_c_im[l], ssm_d[l]) for l in range(depth)]

    def dense_stage(h, l, early=None):
        t = h.shape[0] * s
        h = _mixers(h, l, norm1_g, w_in, ssm_tabs[l], fox_b_f, g_ssm_out, g_attn_out, w_o, ssm_w_glu, ssm_b_glu,
                    tm, blk)
        if early is not None:
            h, _ = lax.optimization_barrier((h, early))
        h2 = h.reshape(t, d)
        hn, idx, gate = _route(h2, norm2_g[l].reshape(1, d), peer_w_q[l].astype(BF16), _keys_cat(peer_keys[l]),
                               min(256, t))
        return h2, hn, idx.reshape(t * nk // SC_ROWS, SC_ROWS), gate.reshape(t * nk)

    tabs = [(_pack_bf16_pairs(peer_u[l]), _pack_bf16_pairs(peer_v[l])) for l in range(depth)]

    def expert_stage(st, l):
        h2, hn, idx4, gate = st
        return _peer_experts(hn, h2, idx4, gate, tabs[l][0], tabs[l][1]).reshape(-1, s, d)

    def after(value, st):
        value, _ = lax.optimization_barrier((value, st[3]))
        return value

    hs = [x[p * bh:(p + 1) * bh] for p in range(N_PARTS)]
    prev = None
    chain = []
    for l in range(depth):
        for p in range(N_PARTS):
            pieces = [hs[p][:bh // 2], hs[p][bh // 2:]] if prev is None else [hs[p]]
            done = []
            for piece in pieces:
                early = tabs[0] if prev is None else (tabs[l + 1] if p == N_PARTS - 1 and l + 1 < depth else None)
                if prev is not None:
                    piece = after(piece, prev)
                if len(chain) >= 2 and l == 0:
                    piece, _ = lax.optimization_barrier((piece, chain[-2]))
                st = dense_stage(piece, l, early)
                done.append(expert_stage(st, l))
                chain.append(done[-1])
                prev = st
            hs[p] = done[0] if len(done) == 1 else jnp.concatenate(done, axis=0)
    t = bh * s
    outs = [_final_norm(h.reshape(t, d), norm_f.reshape(1, d), min(512, t)).reshape(bh, s, d) for h in hs]
    return jnp.concatenate(outs, axis=0)
```

```python
import functools
import math

import jax
import jax.numpy as jnp
from jax import lax
from jax.experimental import pallas as pl
from jax.experimental.pallas import tpu as pltpu
from jax.experimental.pallas import tpu_sc as plsc

F32 = jnp.float32
BF16 = jnp.bfloat16
I32 = jnp.int32

RMS_EPS = 1e-6
SSM_GROUP_CH = 16
SSM_STATE = 64
SSM_CHUNK = 128
HEAD_DIM = 64
N_HEADS = 8
AUG = 128
PEER_HEADS = 8
PEER_KEYS = 128
PEER_TOPK = 16
VMEM_LIMIT = 56 * 1024 * 1024
N_PARTS = 2


def _rms(x, g):
    return x * lax.rsqrt(jnp.mean(x * x, axis=-1, keepdims=True) + RMS_EPS) * g


def _gelu(x):
    c = math.sqrt(2.0 / math.pi)
    return 0.5 * x * (1.0 + jnp.tanh(c * (x + 0.044715 * (x * x * x))))


def _sigmoid(x):
    return 1.0 / (1.0 + jnp.exp(-x))


def _in_proj_kernel(h_ref, g_ref, wu_ref, wm_ref, wf_ref, bf_ref, u_ref, qa_ref, ka_ref, v_ref, cum_ref):
    j = pl.program_id(1)

    @pl.when(j == 0)
    def _():
        cum_ref[...] = jnp.zeros_like(cum_ref)

    x = h_ref[0]
    tm = x.shape[0]
    xn = _rms(x, g_ref[...])
    xb = xn.astype(BF16)
    proj = jnp.dot(xb, wm_ref[...], preferred_element_type=F32)
    u_ref[...] = lax.dot_general(wu_ref[...], xb, (((1,), (1,)), ((), ())),
                                 preferred_element_type=F32).astype(BF16)
    f = jnp.dot(xn, wf_ref[...], precision=lax.Precision.HIGHEST, preferred_element_type=F32) + bf_ref[...]
    logf = jnp.minimum(f, 0.0) - jnp.log(1.0 + jnp.exp(-jnp.abs(f)))
    row = lax.broadcasted_iota(I32, (tm, tm), 0)
    col = lax.broadcasted_iota(I32, (tm, tm), 1)
    tri = (row >= col).astype(F32)
    cum = jnp.dot(tri, logf, precision=lax.Precision.HIGHEST, preferred_element_type=F32) + cum_ref[0:1, :]
    cum_ref[0:1, :] = cum[tm - 1:tm, :]

    w = HEAD_DIM * N_HEADS
    lane = lax.broadcasted_iota(I32, (tm, AUG), 1)
    scale = HEAD_DIM ** -0.5
    for hh in range(N_HEADS):
        pair = hh // 2
        q2 = proj[:, 128 * pair: 128 * pair + 128]
        k2 = proj[:, w + 128 * pair: w + 128 * pair + 128]
        v2 = proj[:, 2 * w + 128 * pair: 2 * w + 128 * pair + 128]
        if hh % 2 == 1:
            q2 = pltpu.roll(q2, 64, axis=1)
            k2 = pltpu.roll(k2, 64, axis=1)
            vh = v2[:, 64:]
        else:
            vh = v2[:, :64]
        c = jnp.broadcast_to(cum[:, hh:hh + 1], (tm, AUG))
        c1 = c.astype(BF16).astype(F32)
        r1 = c - c1
        c2 = r1.astype(BF16).astype(F32)
        c3 = r1 - c2
        one = jnp.ones((tm, AUG), F32)
        zero = jnp.zeros((tm, AUG), F32)
        qa = jnp.where(lane < 64, q2 * scale,
             jnp.where(lane == 64, c1, jnp.where(lane == 65, c2, jnp.where(lane == 66, c3,
             jnp.where(lane < 70, one, zero)))))
        ka = jnp.where(lane < 64, k2,
             jnp.where(lane < 67, one, jnp.where(lane == 67, -c1, jnp.where(lane == 68, -c2,
             jnp.where(lane == 69, -c3, zero)))))
        qa_ref[0, hh] = qa.astype(BF16)
        ka_ref[0, hh] = ka.astype(BF16)
        v_ref[0, hh] = vh.astype(BF16)


def _in_proj(h, g, wu_t, wm, wf, bf, tm):
    b, s, d = h.shape
    w = HEAD_DIM * N_HEADS
    ns = s // tm
    return pl.pallas_call(
        _in_proj_kernel,
        grid=(b, ns),
        in_specs=[
            pl.BlockSpec((1, tm, d), lambda i, j: (i, j, 0)),
            pl.BlockSpec((1, d), lambda i, j: (0, 0)),
            pl.BlockSpec((w, d), lambda i, j: (0, 0)),
            pl.BlockSpec((d, 3 * w), lambda i, j: (0, 0)),
            pl.BlockSpec((d, 128), lambda i, j: (0, 0)),
            pl.BlockSpec((1, 128), lambda i, j: (0, 0)),
        ],
        out_specs=[
            pl.BlockSpec((w, tm), lambda i, j: (0, i * ns + j)),
            pl.BlockSpec((1, N_HEADS, tm, AUG), lambda i, j: (i, 0, j, 0)),
            pl.BlockSpec((1, N_HEADS, tm, AUG), lambda i, j: (i, 0, j, 0)),
            pl.BlockSpec((1, N_HEADS, tm, HEAD_DIM), lambda i, j: (i, 0, j, 0)),
        ],
        out_shape=[
            jax.ShapeDtypeStruct((w, b * s), BF16),
            jax.ShapeDtypeStruct((b, N_HEADS, s, AUG), BF16),
            jax.ShapeDtypeStruct((b, N_HEADS, s, AUG), BF16),
            jax.ShapeDtypeStruct((b, N_HEADS, s, HEAD_DIM), BF16),
        ],
        scratch_shapes=[pltpu.VMEM((8, 128), F32)],
        compiler_params=pltpu.CompilerParams(
            dimension_semantics=("parallel", "arbitrary"), vmem_limit_bytes=VMEM_LIMIT),
        cost_estimate=pl.CostEstimate(flops=2 * b * s * d * (4 * w + 128), transcendentals=2 * b * s * 128,
                                      bytes_accessed=4 * b * s * d + 2 * d * (4 * w) + 2 * b * s * (w + 2 * N_HEADS * AUG + w)),
        name="in_proj",
    )(h, g, wu_t, wm, wf, bf)


def _toeplitz_kernel(k_ref, m_ref):
    lc = SSM_CHUNK
    hc = SSM_GROUP_CH
    row = lax.broadcasted_iota(I32, (lc, lc), 0)
    col = lax.broadcasted_iota(I32, (lc, lc), 1)
    causal = col >= row

    def body(hi, carry):
        r0 = pl.multiple_of(hi * lc, lc)
        for ho in range(hc):
            k = jnp.broadcast_to(k_ref[0, pl.ds(hi * hc + ho, 1), :], (lc, lc))
            t = pltpu.roll(k, 0, 1, stride=1, stride_axis=0)
            m_ref[0, pl.ds(r0, lc), ho * lc:(ho + 1) * lc] = jnp.where(causal, t, 0.0).astype(BF16)
        return carry

    lax.fori_loop(0, hc, body, 0)


def _toeplitz(kq):
    g, hh, lc = kq.shape
    width = SSM_GROUP_CH * lc
    return pl.pallas_call(
        _toeplitz_kernel,
        grid=(g,),
        in_specs=[pl.BlockSpec((1, hh, lc), lambda i: (i, 0, 0))],
        out_specs=pl.BlockSpec((1, width, width), lambda i: (i, 0, 0)),
        out_shape=jax.ShapeDtypeStruct((g, width, width), BF16),
        compiler_params=pltpu.CompilerParams(dimension_semantics=("parallel",), vmem_limit_bytes=VMEM_LIMIT),
        cost_estimate=pl.CostEstimate(flops=g * width * width, transcendentals=0,
                                      bytes_accessed=2 * g * width * width + 4 * g * hh * lc),
        name="ssm_toeplitz",
    )(kq)


def _ssm_kernel(u_ref, m_ref, w_ref, r_ref, a_ref, d_ref, y_ref, e_scr, *, n_chunks, batch):
    hc = SSM_GROUP_CH
    lc = SSM_CHUNK
    u = jnp.concatenate([u_ref[0, r] for r in range(hc)], axis=1)
    y = jnp.dot(u, m_ref[0], preferred_element_type=F32)
    e_scr[...] = jnp.dot(u, w_ref[0], preferred_element_type=F32)
    ar = a_ref[0, 0:1, :]
    ai = a_ref[0, 1:2, :]

    def body(c, s):
        rows = pl.ds(c, batch, stride=n_chunks)
        e_c = e_scr[rows, :]
        e_scr[rows, :] = s
        return ar * s + ai * pltpu.roll(s, SSM_STATE, axis=1) + e_c

    lax.fori_loop(0, n_chunks, body, jnp.zeros((batch, 2 * SSM_STATE), F32))
    y = y + jnp.dot(e_scr[...].astype(BF16), r_ref[0], preferred_element_type=F32)
    y = y + u.astype(F32) * d_ref[0]
    for r in range(hc):
        y_ref[0, r] = y[:, r * lc:(r + 1) * lc]


def _ssm(u4, m, w, r, a, d, n_chunks, batch):
    g, hc, rows, lc = u4.shape
    width = hc * lc
    kern = functools.partial(_ssm_kernel, n_chunks=n_chunks, batch=batch)
    return pl.pallas_call(
        kern,
        grid=(g,),
        in_specs=[
            pl.BlockSpec((1, hc, rows, lc), lambda i: (i, 0, 0, 0)),
            pl.BlockSpec((1, width, width), lambda i: (i, 0, 0)),
            pl.BlockSpec((1, width, 2 * SSM_STATE), lambda i: (i, 0, 0)),
            pl.BlockSpec((1, 2 * SSM_STATE, width), lambda i: (i, 0, 0)),
            pl.BlockSpec((1, 8, 2 * SSM_STATE), lambda i: (i, 0, 0)),
            pl.BlockSpec((1, 1, width), lambda i: (i, 0, 0)),
        ],
        out_specs=pl.BlockSpec((1, hc, rows, lc), lambda i: (i, 0, 0, 0)),
        out_shape=jax.ShapeDtypeStruct((g, hc, rows, lc), F32),
        scratch_shapes=[pltpu.VMEM((rows, 2 * SSM_STATE), F32)],
        compiler_params=pltpu.CompilerParams(
            dimension_semantics=("parallel",), vmem_limit_bytes=VMEM_LIMIT),
        cost_estimate=pl.CostEstimate(flops=2 * g * rows * width * (width + 4 * SSM_STATE), transcendentals=0,
                                      bytes_accessed=g * (2 * width * (width + 4 * SSM_STATE) + 6 * rows * width)),
        name="ssm",
    )(u4, m, w, r, a, d)


def _ssm_tables(lam_re, lam_im, log_dt, b_re, b_im, c_re, c_im, d_skip):
    hp = lax.Precision.HIGHEST
    lc = SSM_CHUNK
    g, p = lam_re.shape
    hc = SSM_GROUP_CH
    dt = jnp.exp(log_dt)[:, None]
    a_re = jnp.exp(lam_re * dt) * jnp.cos(lam_im * dt)
    a_im = jnp.exp(lam_re * dt) * jnp.sin(lam_im * dt)
    den = lam_re * lam_re + lam_im * lam_im
    nr = a_re - 1.0
    z_re = (nr * lam_re + a_im * lam_im) / den
    z_im = (a_im * lam_re - nr * lam_im) / den
    bb_re = z_re[..., None] * b_re - z_im[..., None] * b_im
    bb_im = z_re[..., None] * b_im + z_im[..., None] * b_re
    tau = jnp.arange(lc + 1, dtype=F32)[:, None, None]
    mag = jnp.exp(tau * (lam_re * dt)[None])
    ang = tau * (lam_im * dt)[None]
    p_re = mag * jnp.cos(ang)
    p_im = mag * jnp.sin(ang)
    ab_re = p_re[:lc, :, :, None] * bb_re[None] - p_im[:lc, :, :, None] * bb_im[None]
    ab_im = p_re[:lc, :, :, None] * bb_im[None] + p_im[:lc, :, :, None] * bb_re[None]
    kq = (jnp.einsum('ghp,tgpk->gkht', c_re, ab_re, precision=hp)
          - jnp.einsum('ghp,tgpk->gkht', c_im, ab_im, precision=hp)).reshape(g, hc * hc, lc)
    m = _toeplitz(kq)
    w_re = ab_re[::-1].transpose(1, 3, 0, 2).reshape(g, hc * lc, p)
    w_im = ab_im[::-1].transpose(1, 3, 0, 2).reshape(g, hc * lc, p)
    w = jnp.concatenate([w_re, w_im], axis=-1)
    q_re = p_re[1:, :, None, :] * c_re[None] - p_im[1:, :, None, :] * c_im[None]
    q_im = p_re[1:, :, None, :] * c_im[None] + p_im[1:, :, None, :] * c_re[None]
    r = jnp.concatenate([q_re.transpose(1, 3, 2, 0).reshape(g, p, hc * lc),
                         -q_im.transpose(1, 3, 2, 0).reshape(g, p, hc * lc)], axis=1)
    al_re, al_im = p_re[lc], p_im[lc]
    a = jnp.zeros((g, 8, 2 * p), F32)
    a = a.at[:, 0, :].set(jnp.concatenate([al_re, al_re], axis=-1))
    a = a.at[:, 1, :].set(jnp.concatenate([-al_im, al_im], axis=-1))
    d = jnp.repeat(d_skip, lc, axis=-1).reshape(g, 1, hc * lc)
    return m, w.astype(BF16), r.astype(BF16), a, d


def _attn_kernel(q_ref, k_ref, v_ref, o_ref, m_ref, l_ref, acc_ref, *, blk):
    i = pl.program_id(1)
    q = q_ref[0]
    m_ref[...] = jnp.full_like(m_ref, -jnp.inf)
    l_ref[...] = jnp.zeros_like(l_ref)
    acc_ref[...] = jnp.zeros_like(acc_ref)

    def step(j, masked):
        off = pl.multiple_of(j * blk, blk)
        k = k_ref[0, pl.ds(off, blk), :]
        v = v_ref[0, pl.ds(off, blk), :]
        s = lax.dot_general(q, k, (((1,), (1,)), ((), ())), preferred_element_type=F32)
        if masked:
            row = lax.broadcasted_iota(I32, s.shape, 0)
            col = lax.broadcasted_iota(I32, s.shape, 1)
            s = jnp.where(row >= col, s, -jnp.inf)
        m_prev = m_ref[...]
        m_new = jnp.maximum(m_prev, jnp.max(s, axis=1, keepdims=True))
        p = jnp.exp(s - m_new)
        alpha = jnp.exp(m_prev - m_new)
        l_ref[...] = alpha * l_ref[...] + jnp.sum(p, axis=1, keepdims=True)
        acc_ref[...] = alpha * acc_ref[...] + jnp.dot(p.astype(BF16), v, preferred_element_type=F32)
        m_ref[...] = m_new

    def body(j, c):
        step(j, False)
        return c

    lax.fori_loop(0, i, body, 0)
    step(i, True)
    o_ref[0] = acc_ref[...] / l_ref[...]


def _attention(qa, ka, v, blk):
    bh, s, _ = qa.shape
    kern = functools.partial(_attn_kernel, blk=blk)
    return pl.pallas_call(
        kern,
        grid=(bh, s // blk),
        in_specs=[
            pl.BlockSpec((1, blk, AUG), lambda b, i: (b, i, 0)),
            pl.BlockSpec((1, s, AUG), lambda b, i: (b, 0, 0)),
            pl.BlockSpec((1, s, HEAD_DIM), lambda b, i: (b, 0, 0)),
        ],
        out_specs=pl.BlockSpec((1, blk, HEAD_DIM), lambda b, i: (b, i, 0)),
        out_shape=jax.ShapeDtypeStruct((bh, s, HEAD_DIM), F32),
        scratch_shapes=[pltpu.VMEM((blk, 1), F32), pltpu.VMEM((blk, 1), F32), pltpu.VMEM((blk, HEAD_DIM), F32)],
        compiler_params=pltpu.CompilerParams(
            dimension_semantics=("parallel", "arbitrary"), vmem_limit_bytes=VMEM_LIMIT),
        cost_estimate=pl.CostEstimate(flops=bh * s * s * (AUG + HEAD_DIM), transcendentals=bh * s * s // 2,
                                      bytes_accessed=bh * s * (2 * 2 * AUG + 2 * HEAD_DIM + 4 * HEAD_DIM)),
        name="fox_attn",
    )(qa, ka, v)


def _out_proj_kernel(y_ref, att_ref, h_ref, wg_ref, bg_ref, gs_ref, ga_ref, wos_ref, woa_ref, o_ref):
    g = _gelu(y_ref[...])
    z = jnp.dot(wg_ref[...], g.astype(BF16), preferred_element_type=F32) + bg_ref[...]
    o = g * _sigmoid(z)
    a = o * lax.rsqrt(jnp.mean(o * o, axis=0, keepdims=True) + RMS_EPS) * gs_ref[...]
    acc = h_ref[0] + lax.dot_general(a.astype(BF16), wos_ref[...], (((0,), (0,)), ((), ())),
                                     preferred_element_type=F32)
    ssq = jnp.zeros((acc.shape[0], 1), F32)
    for hh in range(N_HEADS):
        t = att_ref[0, hh]
        ssq = ssq + jnp.sum(t * t, axis=1, keepdims=True)
    inv = lax.rsqrt(ssq / (N_HEADS * HEAD_DIM) + RMS_EPS)
    for hh in range(N_HEADS):
        bh = att_ref[0, hh] * inv * ga_ref[hh]
        acc = acc + jnp.dot(bh.astype(BF16), woa_ref[hh], preferred_element_type=F32)
    o_ref[0] = acc


def _out_proj(y_t, y_att, h, wg_t, bg, gs, ga, wos, woa, tm):
    b, s, d = h.shape
    w = y_t.shape[0]
    ns = s // tm
    return pl.pallas_call(
        _out_proj_kernel,
        grid=(b, ns),
        in_specs=[
            pl.BlockSpec((w, tm), lambda i, j: (0, i * ns + j)),
            pl.BlockSpec((1, N_HEADS, tm, HEAD_DIM), lambda i, j: (i, 0, j, 0)),
            pl.BlockSpec((1, tm, d), lambda i, j: (i, j, 0)),
            pl.BlockSpec((w, w), lambda i, j: (0, 0)),
            pl.BlockSpec((w, 1), lambda i, j: (0, 0)),
            pl.BlockSpec((w, 1), lambda i, j: (0, 0)),
            pl.BlockSpec((N_HEADS, 1, HEAD_DIM), lambda i, j: (0, 0, 0)),
            pl.BlockSpec((w, d), lambda i, j: (0, 0)),
            pl.BlockSpec((N_HEADS, HEAD_DIM, d), lambda i, j: (0, 0, 0)),
        ],
        out_specs=pl.BlockSpec((1, tm, d), lambda i, j: (i, j, 0)),
        out_shape=jax.ShapeDtypeStruct((b, s, d), F32),
        compiler_params=pltpu.CompilerParams(
            dimension_semantics=("parallel", "parallel"), vmem_limit_bytes=VMEM_LIMIT),
        cost_estimate=pl.CostEstimate(flops=2 * b * s * (w * w + 2 * w * d), transcendentals=2 * b * s * w,
                                      bytes_accessed=b * s * (8 * w + 8 * d) + 2 * (w * w + 2 * w * d)),
        name="out_proj",
    )(y_t, y_att, h, wg_t, bg, gs, ga, wos, woa)


def _take_top(vals, payload, k):
    n_rows = vals.shape[0]
    rows = lax.broadcasted_iota(I32, vals.shape, 0)
    tops, picks = [], []
    for _ in range(k):
        m = jnp.max(vals, axis=0, keepdims=True)
        arg = jnp.min(jnp.where(vals == m, rows, n_rows), axis=0, keepdims=True)
        hit = rows == arg
        tops.append(m)
        picks.append(arg if payload is None else jnp.max(jnp.where(hit, payload, -1), axis=0, keepdims=True))
        vals = jnp.where(hit, -jnp.inf, vals)
    return jnp.concatenate(tops, axis=0), jnp.concatenate(picks, axis=0)


def _route_kernel(h_ref, g_ref, wq_ref, keys_ref, hn_ref, idx_ref, gate_ref):
    x = h_ref[...]
    hn = _rms(x, g_ref[...])
    hn_ref[...] = _pack_bf16_pairs(hn)
    q = jnp.dot(hn.astype(BF16), wq_ref[...], preferred_element_type=F32)
    k = PEER_TOPK
    idx_rows, gate_rows = [], []
    for hh in range(PEER_HEADS):
        qh = q[:, 128 * hh: 128 * (hh + 1)].astype(BF16)
        sc = lax.dot_general(keys_ref[hh], qh, (((1,), (1,)), ((), ())), preferred_element_type=F32)
        t1, i1 = _take_top(sc[:PEER_KEYS], None, k)
        t2, i2 = _take_top(sc[PEER_KEYS:], None, k)
        widths = [k // (a + 1) for a in range(k)]
        pad = -sum(widths) % 8
        cand = jnp.concatenate([t1[a:a + 1] + t2[:widths[a]] for a in range(k)]
                               + [jnp.full((pad, t1.shape[1]), -jnp.inf, F32)], axis=0)
        cidx = jnp.concatenate([i1[a:a + 1] * PEER_KEYS + i2[:widths[a]] for a in range(k)]
                               + [jnp.full((pad, t1.shape[1]), -1, I32)], axis=0)
        best, idx = _take_top(cand, cidx, k)
        e = jnp.exp(best - best[0:1])
        gate = e / jnp.sum(e, axis=0, keepdims=True)
        idx_rows.append(idx)
        gate_rows.append(gate)
    idx_ref[...] = jnp.concatenate(idx_rows, axis=0).T
    gate_ref[...] = jnp.concatenate(gate_rows, axis=0).T


def _route(h2, g, wq, keys_cat, tm):
    t, d = h2.shape
    nk = PEER_HEADS * PEER_TOPK
    return pl.pallas_call(
        _route_kernel,
        grid=(t // tm,),
        in_specs=[
            pl.BlockSpec((tm, d), lambda i: (i, 0)),
            pl.BlockSpec((1, d), lambda i: (0, 0)),
            pl.BlockSpec((d, PEER_HEADS * 128), lambda i: (0, 0)),
            pl.BlockSpec((PEER_HEADS, 2 * PEER_KEYS, 128), lambda i: (0, 0, 0)),
        ],
        out_specs=[
            pl.BlockSpec((tm, d // 2), lambda i: (i, 0)),
            pl.BlockSpec((tm, nk), lambda i: (i, 0)),
            pl.BlockSpec((tm, nk), lambda i: (i, 0)),
        ],
        out_shape=[
            jax.ShapeDtypeStruct((t, d // 2), I32),
            jax.ShapeDtypeStruct((t, nk), I32),
            jax.ShapeDtypeStruct((t, nk), F32),
        ],
        compiler_params=pltpu.CompilerParams(
            dimension_semantics=("parallel",), vmem_limit_bytes=VMEM_LIMIT),
        cost_estimate=pl.CostEstimate(flops=2 * t * d * PEER_HEADS * 128 + 2 * t * PEER_HEADS * 256 * 128,
                                      transcendentals=t * nk, bytes_accessed=t * (8 * d + 8 * nk) + 2 * d * PEER_HEADS * 128),
        name="peer_route",
    )(h2, g, wq, keys_cat)


def _final_norm_kernel(x_ref, g_ref, o_ref):
    o_ref[...] = _rms(x_ref[...], g_ref[...])


def _final_norm(x2, g, tm):
    t, d = x2.shape
    return pl.pallas_call(
        _final_norm_kernel,
        grid=(t // tm,),
        in_specs=[pl.BlockSpec((tm, d), lambda i: (i, 0)), pl.BlockSpec((1, d), lambda i: (0, 0))],
        out_specs=pl.BlockSpec((tm, d), lambda i: (i, 0)),
        out_shape=jax.ShapeDtypeStruct((t, d), F32),
        compiler_params=pltpu.CompilerParams(dimension_semantics=("parallel",)),
        cost_estimate=pl.CostEstimate(flops=4 * t * d, transcendentals=t, bytes_accessed=8 * t * d),
        name="final_norm",
    )(x2, g)


SC_WORKERS = 32
SC_CORES = 2
SC_LANES = 16
SC_TOK = 16
SC_ROWS = 16
SC_BUFS = 4
SC_SLAB = 8
HI16 = -65536


def _sc_params():
    cp = pltpu.CompilerParams()
    if "needs_layout_passes" in pltpu.CompilerParams.__dataclass_fields__:
        cp = pltpu.CompilerParams(needs_layout_passes=False, use_tc_tiling_on_sc=False)
    return cp


def _sc_worker_id():
    return lax.axis_index("s") * SC_CORES + lax.axis_index("c")


def _tree_sum(xs):
    xs = list(xs)
    while len(xs) > 1:
        xs = [xs[i] + xs[i + 1] for i in range(0, len(xs) - 1, 2)] + ([xs[-1]] if len(xs) % 2 else [])
    return xs[0]


def _gelu_via_exp(x):
    c = math.sqrt(2.0 / math.pi)
    z = c * (x + 0.044715 * (x * x * x))
    return 0.5 * x * (2.0 - 2.0 / (1.0 + jnp.exp(2.0 * z)))


def _round_to_bf16_bits(bits):
    return bits + 0x7FFF + (lax.shift_right_logical(bits, jnp.int32(16)) & 1)


def _pack_bf16_pairs(x):
    w = x.shape[-1] // 2
    r = _round_to_bf16_bits(lax.bitcast_convert_type(x, I32))
    return lax.shift_right_logical(r[..., :w], jnp.int32(16)) | (r[..., w:] & HI16)


def _lo_f32(word):
    return lax.bitcast_convert_type(lax.shift_left(word, jnp.int32(16)), F32)


def _hi_f32(word):
    return lax.bitcast_convert_type(word & HI16, F32)


def _peer_experts_body(hn_hbm, h_hbm, idx_hbm, gate_hbm, u_hbm, v_hbm, out_hbm,
                       h_v, o_v, idx_v, gate_v, coef_v, cw_v, acc_v, *bufs, tok_per_worker):
    rows, sems = bufs[:SC_BUFS], bufs[SC_BUFS:]
    half = hn_hbm.shape[1]
    nk = PEER_HEADS * PEER_TOPK
    per_tok = nk // SC_ROWS
    n = SC_TOK * per_tok
    n_slab = half // (SC_SLAB * SC_LANES)
    lane = lax.iota(I32, SC_LANES)
    wid = _sc_worker_id()

    def packed_mul(word, other_bf16):
        return plsc.bitcast(plsc.bitcast(word, BF16) * other_bf16, I32)

    def gather_u(k, slot):
        return pltpu.make_async_copy(u_hbm.at[idx_v.at[k]], rows[slot], sems[slot])

    def gather_v(k, slot):
        return pltpu.make_async_copy(v_hbm.at[idx_v.at[k]], rows[slot], sems[slot])

    def start_ahead(ahead, slot):
        @pl.when(ahead < n)
        def _():
            gather_u(ahead, slot).start()

        @pl.when(jnp.logical_and(ahead >= n, ahead < 2 * n))
        def _():
            gather_v(ahead - n, slot).start()

    def score(rw, k):
        t = k // per_tok
        for sl in range(n_slab):
            base = sl * SC_SLAB * SC_LANES
            hw = [plsc.bitcast(h_v[t, pl.ds(base + SC_LANES * j, SC_LANES)], BF16) for j in range(SC_SLAB)]

            @plsc.parallel_loop(0, SC_ROWS, 1)
            def _(r):
                terms = []
                for j in range(SC_SLAB):
                    p = packed_mul(rw[r, pl.ds(base + SC_LANES * j, SC_LANES)], hw[j])
                    terms += [_lo_f32(p), _hi_f32(p)]
                p = _tree_sum(terms)
                off = pl.multiple_of(r * SC_LANES, SC_LANES)
                if sl == 0:
                    acc_v[pl.ds(off, SC_LANES)] = p
                else:
                    acc_v[pl.ds(off, SC_LANES)] = acc_v[pl.ds(off, SC_LANES)] + p

        for rg in range(SC_ROWS // SC_LANES):
            base = rg * SC_LANES * SC_LANES
            tot = _tree_sum([plsc.load_gather(acc_v, [lane * SC_LANES + (base + j)]) for j in range(SC_LANES)])
            coef_v[pl.ds(pl.multiple_of(k * SC_ROWS + rg * SC_LANES, SC_LANES), SC_LANES)] = tot

    def combine(rw, k):
        t = k // per_tok
        for sl in range(n_slab):
            base = sl * SC_SLAB * SC_LANES
            acc0 = (tuple(o_v[t, pl.ds(base + SC_LANES * j, SC_LANES)] for j in range(SC_SLAB))
                    + tuple(o_v[t, pl.ds(half + base + SC_LANES * j, SC_LANES)] for j in range(SC_SLAB)))

            def rbody(r, acc):
                w = plsc.bitcast(plsc.load_gather(cw_v, [jnp.full((SC_LANES,), k * SC_ROWS + r, I32)]), BF16)
                new = list(acc)
                for j in range(SC_SLAB):
                    p = packed_mul(rw[r, pl.ds(base + SC_LANES * j, SC_LANES)], w)
                    new[j] = acc[j] + _lo_f32(p)
                    new[SC_SLAB + j] = acc[SC_SLAB + j] + _hi_f32(p)
                return tuple(new)

            acc = lax.fori_loop(0, SC_ROWS, rbody, acc0)
            for j in range(SC_SLAB):
                o_v[t, pl.ds(base + SC_LANES * j, SC_LANES)] = acc[j]
                o_v[t, pl.ds(half + base + SC_LANES * j, SC_LANES)] = acc[SC_SLAB + j]

    def batch(bi, carry):
        t0 = pl.multiple_of(wid * tok_per_worker + bi * SC_TOK, SC_TOK)
        pltpu.sync_copy(idx_hbm.at[pl.ds(t0 * per_tok, n)], idx_v)
        for s in range(SC_BUFS - 1):
            gather_u(s, s).start()
        pltpu.sync_copy(hn_hbm.at[pl.ds(t0, SC_TOK)], h_v)
        pltpu.sync_copy(h_hbm.at[pl.ds(t0, SC_TOK)], o_v)
        pltpu.sync_copy(gate_hbm.at[pl.ds(t0 * nk, SC_TOK * nk)], gate_v)

        def score_group(i, c):
            for s in range(SC_BUFS):
                k = i * SC_BUFS + s
                start_ahead(k + SC_BUFS - 1, (s + SC_BUFS - 1) % SC_BUFS)
                gather_u(k, s).wait()
                score(rows[s], k)
            return c

        lax.fori_loop(0, n // SC_BUFS, score_group, 0)

        @plsc.parallel_loop(0, SC_TOK * nk // SC_LANES, 1)
        def _(j):
            off = pl.multiple_of(j * SC_LANES, SC_LANES)
            c = _gelu_via_exp(coef_v[pl.ds(off, SC_LANES)]) * gate_v[pl.ds(off, SC_LANES)]
            hi = _round_to_bf16_bits(lax.bitcast_convert_type(c, I32)) & HI16
            cw_v[pl.ds(off, SC_LANES)] = hi | lax.shift_right_logical(hi, jnp.int32(16))

        def combine_group(i, c):
            for s in range(SC_BUFS):
                k = i * SC_BUFS + s
                start_ahead(n + k + SC_BUFS - 1, (s + SC_BUFS - 1) % SC_BUFS)
                gather_v(k, s).wait()
                combine(rows[s], k)
            return c

        lax.fori_loop(0, n // SC_BUFS, combine_group, 0)
        pltpu.sync_copy(o_v, out_hbm.at[pl.ds(t0, SC_TOK)])
        return carry

    lax.fori_loop(0, tok_per_worker // SC_TOK, batch, 0)


def _peer_experts(hn_words, h2, idx4, gate_flat, u_words, v_words):
    t, d = h2.shape
    nk = PEER_HEADS * PEER_TOPK
    mesh = plsc.VectorSubcoreMesh(core_axis_name="c", subcore_axis_name="s")
    body = functools.partial(_peer_experts_body, tok_per_worker=t // SC_WORKERS)
    return pl.kernel(
        body,
        out_type=jax.ShapeDtypeStruct((t, d), F32),
        mesh=mesh,
        scratch_types=[
            pltpu.VMEM((SC_TOK, d // 2), I32),
            pltpu.VMEM((SC_TOK, d), F32),
            pltpu.VMEM((SC_TOK * nk // SC_ROWS, SC_ROWS), I32),
            pltpu.VMEM((SC_TOK * nk,), F32),
            pltpu.VMEM((SC_TOK * nk,), F32),
            pltpu.VMEM((SC_TOK * nk,), I32),
            pltpu.VMEM((SC_ROWS * SC_LANES,), F32),
        ] + [pltpu.VMEM((SC_ROWS, d // 2), I32)] * SC_BUFS + [pltpu.SemaphoreType.DMA] * SC_BUFS,
        compiler_params=_sc_params(),
        cost_estimate=pl.CostEstimate(flops=4 * t * nk * d, transcendentals=t * nk,
                                      bytes_accessed=4 * t * nk * d + 10 * t * d),
        name="peer_experts_sc",
    )(hn_words, h2, idx4, gate_flat, u_words, v_words)


def _mixers(h, l, norm1_g, w_in, ssm, fox_b_f, g_ssm_out, g_attn_out, w_o, ssm_w_glu, ssm_b_glu, tm, blk):
    b, s, d = h.shape
    w = N_HEADS * HEAD_DIM
    wl = w_in[l]
    wu_t = wl[:, :w].T.astype(BF16)
    wm = wl[:, w:4 * w].astype(BF16)
    wf = jnp.pad(wl[:, 4 * w:], ((0, 0), (0, 128 - N_HEADS)))
    bf = jnp.pad(fox_b_f[l], (0, 128 - N_HEADS)).reshape(1, 128)
    u_t, qa, ka, v = _in_proj(h, norm1_g[l].reshape(1, d), wu_t, wm, wf, bf, tm)

    lc = SSM_CHUNK
    nc = s // lc
    g = w // SSM_GROUP_CH
    y4 = _ssm(u_t.reshape(g, SSM_GROUP_CH, b * nc, lc), *ssm, n_chunks=nc, batch=b)
    y_t = y4.reshape(w, b * s)

    y_att = _attention(qa.reshape(b * N_HEADS, s, AUG), ka.reshape(b * N_HEADS, s, AUG),
                       v.reshape(b * N_HEADS, s, HEAD_DIM), blk).reshape(b, N_HEADS, s, HEAD_DIM)

    wo = w_o[l].astype(BF16)
    return _out_proj(y_t, y_att, h, ssm_w_glu[l].T.astype(BF16), ssm_b_glu[l].reshape(w, 1),
                     g_ssm_out[l].reshape(w, 1), g_attn_out[l].reshape(N_HEADS, 1, HEAD_DIM),
                     wo[:w], wo[w:].reshape(N_HEADS, HEAD_DIM, d), tm)


def _keys_cat(keys_l):
    z = jnp.zeros_like(keys_l[:, 0])
    top = jnp.concatenate([keys_l[:, 0], z], axis=-1)
    bot = jnp.concatenate([z, keys_l[:, 1]], axis=-1)
    return jnp.concatenate([top, bot], axis=1).astype(BF16)


def kernel(x, norm1_g, w_in, ssm_lambda_re, ssm_lambda_im, ssm_log_dt, ssm_b_re, ssm_b_im, ssm_c_re, ssm_c_im, ssm_d, ssm_w_glu, ssm_b_glu, fox_b_f, g_ssm_out, g_attn_out, w_o, norm2_g, peer_w_q, peer_keys, peer_u, peer_v, norm_f):
    b, s, d = x.shape
    depth = w_in.shape[0]
    nk = PEER_HEADS * PEER_TOPK
    tm = min(512, s)
    blk = min(512, s)
    assert b % (2 * N_PARTS) == 0
    bh = b // N_PARTS
    ssm_tabs = [_ssm_tables(ssm_lambda_re[l], ssm_lambda_im[l], ssm_log_dt[l], ssm_b_re[l], ssm_b_im[l],
                            ssm_c_re[l], ssm_c_im[l], ssm_d[l]) for l in range(depth)]

    def dense_stage(h, l, early=None):
        t = h.shape[0] * s
        h = _mixers(h, l, norm1_g, w_in, ssm_tabs[l], fox_b_f, g_ssm_out, g_attn_out, w_o, ssm_w_glu, ssm_b_glu,
                    tm, blk)
        if early is not None:
            h, _ = lax.optimization_barrier((h, early))
        h2 = h.reshape(t, d)
        hn, idx, gate = _route(h2, norm2_g[l].reshape(1, d), peer_w_q[l].astype(BF16), _keys_cat(peer_keys[l]),
                               min(256, t))
        return h2, hn, idx.reshape(t * nk // SC_ROWS, SC_ROWS), gate.reshape(t * nk)

    tabs = [(_pack_bf16_pairs(peer_u[l]), _pack_bf16_pairs(peer_v[l])) for l in range(depth)]

    def expert_stage(st, l):
        h2, hn, idx4, gate = st
        return _peer_experts(hn, h2, idx4, gate, tabs[l][0], tabs[l][1]).reshape(-1, s, d)

    def after(value, st):
        value, _ = lax.optimization_barrier((value, st[3]))
        return value

    hs = [x[p * bh:(p + 1) * bh] for p in range(N_PARTS)]
    prev = None
    chain = []
    for l in range(depth):
        for p in range(N_PARTS):
            pieces = [hs[p][:bh // 2], hs[p][bh // 2:]] if prev is None else [hs[p]]
            done = []
            for piece in pieces:
                early = tabs[0] if prev is None else (tabs[l + 1] if p == N_PARTS - 1 and l + 1 < depth else None)
                if prev is not None:
                    piece = after(piece, prev)
                if len(chain) >= 2 and l == 0:
                    piece, _ = lax.optimization_barrier((piece, chain[-2]))
                st = dense_stage(piece, l, early)
                done.append(expert_stage(st, l))
                chain.append(done[-1])
                prev = st
            hs[p] = done[0] if len(done) == 1 else jnp.concatenate(done, axis=0)
    t = bh * s
    outs = [_final_norm(h.reshape(t, d), norm_f.reshape(1, d), min(512, t)).reshape(bh, s, d) for h in hs]
    return jnp.concatenate(outs, axis=0)
```

```python
import functools
import math

import jax
import jax.numpy as jnp
from jax import lax
from jax.experimental import pallas as pl
from jax.experimental.pallas import tpu as pltpu
from jax.experimental.pallas import tpu_sc as plsc

F32 = jnp.float32
BF16 = jnp.bfloat16
I32 = jnp.int32

RMS_EPS = 1e-6
SSM_GROUP_CH = 16
SSM_STATE = 64
SSM_CHUNK = 128
HEAD_DIM = 64
N_HEADS = 8
AUG = 128
PEER_HEADS = 8
PEER_KEYS = 128
PEER_TOPK = 16
VMEM_LIMIT = 56 * 1024 * 1024
N_PARTS = 2


def _rms(x, g):
    return x * lax.rsqrt(jnp.mean(x * x, axis=-1, keepdims=True) + RMS_EPS) * g


def _gelu(x):
    c = math.sqrt(2.0 / math.pi)
    return 0.5 * x * (1.0 + jnp.tanh(c * (x + 0.044715 * (x * x * x))))


def _sigmoid(x):
    return 1.0 / (1.0 + jnp.exp(-x))


def _in_proj_kernel(h_ref, g_ref, wu_ref, wm_ref, wf_ref, bf_ref, u_ref, qa_ref, ka_ref, v_ref, cum_ref):
    j = pl.program_id(1)

    @pl.when(j == 0)
    def _():
        cum_ref[...] = jnp.zeros_like(cum_ref)

    x = h_ref[0]
    tm = x.shape[0]
    xn = _rms(x, g_ref[...])
    xb = xn.astype(BF16)
    proj = jnp.dot(xb, wm_ref[...], preferred_element_type=F32)
    u_ref[...] = lax.dot_general(wu_ref[...], xb, (((1,), (1,)), ((), ())),
                                 preferred_element_type=F32).astype(BF16)
    f = jnp.dot(xn, wf_ref[...], precision=lax.Precision.HIGHEST, preferred_element_type=F32) + bf_ref[...]
    logf = jnp.minimum(f, 0.0) - jnp.log(1.0 + jnp.exp(-jnp.abs(f)))
    row = lax.broadcasted_iota(I32, (tm, tm), 0)
    col = lax.broadcasted_iota(I32, (tm, tm), 1)
    tri = (row >= col).astype(F32)
    cum = jnp.dot(tri, logf, precision=lax.Precision.HIGHEST, preferred_element_type=F32) + cum_ref[0:1, :]
    cum_ref[0:1, :] = cum[tm - 1:tm, :]

    w = HEAD_DIM * N_HEADS
    lane = lax.broadcasted_iota(I32, (tm, AUG), 1)
    scale = HEAD_DIM ** -0.5
    for hh in range(N_HEADS):
        pair = hh // 2
        q2 = proj[:, 128 * pair: 128 * pair + 128]
        k2 = proj[:, w + 128 * pair: w + 128 * pair + 128]
        v2 = proj[:, 2 * w + 128 * pair: 2 * w + 128 * pair + 128]
        if hh % 2 == 1:
            q2 = pltpu.roll(q2, 64, axis=1)
            k2 = pltpu.roll(k2, 64, axis=1)
            vh = v2[:, 64:]
        else:
            vh = v2[:, :64]
        c = jnp.broadcast_to(cum[:, hh:hh + 1], (tm, AUG))
        c1 = c.astype(BF16).astype(F32)
        r1 = c - c1
        c2 = r1.astype(BF16).astype(F32)
        c3 = r1 - c2
        one = jnp.ones((tm, AUG), F32)
        zero = jnp.zeros((tm, AUG), F32)
        qa = jnp.where(lane < 64, q2 * scale,
             jnp.where(lane == 64, c1, jnp.where(lane == 65, c2, jnp.where(lane == 66, c3,
             jnp.where(lane < 70, one, zero)))))
        ka = jnp.where(lane < 64, k2,
             jnp.where(lane < 67, one, jnp.where(lane == 67, -c1, jnp.where(lane == 68, -c2,
             jnp.where(lane == 69, -c3, zero)))))
        qa_ref[0, hh] = qa.astype(BF16)
        ka_ref[0, hh] = ka.astype(BF16)
        v_ref[0, hh] = vh.astype(BF16)


def _in_proj(h, g, wu_t, wm, wf, bf, tm):
    b, s, d = h.shape
    w = HEAD_DIM * N_HEADS
    ns = s // tm
    return pl.pallas_call(
        _in_proj_kernel,
        grid=(b, ns),
        in_specs=[
            pl.BlockSpec((1, tm, d), lambda i, j: (i, j, 0)),
            pl.BlockSpec((1, d), lambda i, j: (0, 0)),
            pl.BlockSpec((w, d), lambda i, j: (0, 0)),
            pl.BlockSpec((d, 3 * w), lambda i, j: (0, 0)),
            pl.BlockSpec((d, 128), lambda i, j: (0, 0)),
            pl.BlockSpec((1, 128), lambda i, j: (0, 0)),
        ],
        out_specs=[
            pl.BlockSpec((w, tm), lambda i, j: (0, i * ns + j)),
            pl.BlockSpec((1, N_HEADS, tm, AUG), lambda i, j: (i, 0, j, 0)),
            pl.BlockSpec((1, N_HEADS, tm, AUG), lambda i, j: (i, 0, j, 0)),
            pl.BlockSpec((1, N_HEADS, tm, HEAD_DIM), lambda i, j: (i, 0, j, 0)),
        ],
        out_shape=[
            jax.ShapeDtypeStruct((w, b * s), BF16),
            jax.ShapeDtypeStruct((b, N_HEADS, s, AUG), BF16),
            jax.ShapeDtypeStruct((b, N_HEADS, s, AUG), BF16),
            jax.ShapeDtypeStruct((b, N_HEADS, s, HEAD_DIM), BF16),
        ],
        scratch_shapes=[pltpu.VMEM((8, 128), F32)],
        compiler_params=pltpu.CompilerParams(
            dimension_semantics=("parallel", "arbitrary"), vmem_limit_bytes=VMEM_LIMIT),
        cost_estimate=pl.CostEstimate(flops=2 * b * s * d * (4 * w + 128), transcendentals=2 * b * s * 128,
                                      bytes_accessed=4 * b * s * d + 2 * d * (4 * w) + 2 * b * s * (w + 2 * N_HEADS * AUG + w)),
        name="in_proj",
    )(h, g, wu_t, wm, wf, bf)


def _toeplitz_kernel(k_ref, m_ref):
    lc = SSM_CHUNK
    hc = SSM_GROUP_CH
    row = lax.broadcasted_iota(I32, (lc, lc), 0)
    col = lax.broadcasted_iota(I32, (lc, lc), 1)
    causal = col >= row

    def body(hi, carry):
        r0 = pl.multiple_of(hi * lc, lc)
        for ho in range(hc):
            k = jnp.broadcast_to(k_ref[0, pl.ds(hi * hc + ho, 1), :], (lc, lc))
            t = pltpu.roll(k, 0, 1, stride=1, stride_axis=0)
            m_ref[0, pl.ds(r0, lc), ho * lc:(ho + 1) * lc] = jnp.where(causal, t, 0.0).astype(BF16)
        return carry

    lax.fori_loop(0, hc, body, 0)


def _toeplitz(kq):
    g, hh, lc = kq.shape
    width = SSM_GROUP_CH * lc
    return pl.pallas_call(
        _toeplitz_kernel,
        grid=(g,),
        in_specs=[pl.BlockSpec((1, hh, lc), lambda i: (i, 0, 0))],
        out_specs=pl.BlockSpec((1, width, width), lambda i: (i, 0, 0)),
        out_shape=jax.ShapeDtypeStruct((g, width, width), BF16),
        compiler_params=pltpu.CompilerParams(dimension_semantics=("parallel",), vmem_limit_bytes=VMEM_LIMIT),
        cost_estimate=pl.CostEstimate(flops=g * width * width, transcendentals=0,
                                      bytes_accessed=2 * g * width * width + 4 * g * hh * lc),
        name="ssm_toeplitz",
    )(kq)


def _ssm_kernel(u_ref, m_ref, w_ref, r_ref, a_ref, d_ref, y_ref, e_scr, *, n_chunks, batch):
    hc = SSM_GROUP_CH
    lc = SSM_CHUNK
    u = jnp.concatenate([u_ref[0, r] for r in range(hc)], axis=1)
    y = jnp.dot(u, m_ref[0], preferred_element_type=F32)
    e_scr[...] = jnp.dot(u, w_ref[0], preferred_element_type=F32)
    ar = a_ref[0, 0:1, :]
    ai = a_ref[0, 1:2, :]

    def body(c, s):
        rows = pl.ds(c, batch, stride=n_chunks)
        e_c = e_scr[rows, :]
        e_scr[rows, :] = s
        return ar * s + ai * pltpu.roll(s, SSM_STATE, axis=1) + e_c

    lax.fori_loop(0, n_chunks, body, jnp.zeros((batch, 2 * SSM_STATE), F32))
    y = y + jnp.dot(e_scr[...].astype(BF16), r_ref[0], preferred_element_type=F32)
    y = y + u.astype(F32) * d_ref[0]
    for r in range(hc):
        y_ref[0, r] = y[:, r * lc:(r + 1) * lc]


def _ssm(u4, m, w, r, a, d, n_chunks, batch):
    g, hc, rows, lc = u4.shape
    width = hc * lc
    kern = functools.partial(_ssm_kernel, n_chunks=n_chunks, batch=batch)
    return pl.pallas_call(
        kern,
        grid=(g,),
        in_specs=[
            pl.BlockSpec((1, hc, rows, lc), lambda i: (i, 0, 0, 0)),
            pl.BlockSpec((1, width, width), lambda i: (i, 0, 0)),
            pl.BlockSpec((1, width, 2 * SSM_STATE), lambda i: (i, 0, 0)),
            pl.BlockSpec((1, 2 * SSM_STATE, width), lambda i: (i, 0, 0)),
            pl.BlockSpec((1, 8, 2 * SSM_STATE), lambda i: (i, 0, 0)),
            pl.BlockSpec((1, 1, width), lambda i: (i, 0, 0)),
        ],
        out_specs=pl.BlockSpec((1, hc, rows, lc), lambda i: (i, 0, 0, 0)),
        out_shape=jax.ShapeDtypeStruct((g, hc, rows, lc), F32),
        scratch_shapes=[pltpu.VMEM((rows, 2 * SSM_STATE), F32)],
        compiler_params=pltpu.CompilerParams(
            dimension_semantics=("parallel",), vmem_limit_bytes=VMEM_LIMIT),
        cost_estimate=pl.CostEstimate(flops=2 * g * rows * width * (width + 4 * SSM_STATE), transcendentals=0,
                                      bytes_accessed=g * (2 * width * (width + 4 * SSM_STATE) + 6 * rows * width)),
        name="ssm",
    )(u4, m, w, r, a, d)


def _ssm_tables(lam_re, lam_im, log_dt, b_re, b_im, c_re, c_im, d_skip):
    hp = lax.Precision.HIGHEST
    lc = SSM_CHUNK
    g, p = lam_re.shape
    hc = SSM_GROUP_CH
    dt = jnp.exp(log_dt)[:, None]
    a_re = jnp.exp(lam_re * dt) * jnp.cos(lam_im * dt)
    a_im = jnp.exp(lam_re * dt) * jnp.sin(lam_im * dt)
    den = lam_re * lam_re + lam_im * lam_im
    nr = a_re - 1.0
    z_re = (nr * lam_re + a_im * lam_im) / den
    z_im = (a_im * lam_re - nr * lam_im) / den
    bb_re = z_re[..., None] * b_re - z_im[..., None] * b_im
    bb_im = z_re[..., None] * b_im + z_im[..., None] * b_re
    tau = jnp.arange(lc + 1, dtype=F32)[:, None, None]
    mag = jnp.exp(tau * (lam_re * dt)[None])
    ang = tau * (lam_im * dt)[None]
    p_re = mag * jnp.cos(ang)
    p_im = mag * jnp.sin(ang)
    ab_re = p_re[:lc, :, :, None] * bb_re[None] - p_im[:lc, :, :, None] * bb_im[None]
    ab_im = p_re[:lc, :, :, None] * bb_im[None] + p_im[:lc, :, :, None] * bb_re[None]
    kq = (jnp.einsum('ghp,tgpk->gkht', c_re, ab_re, precision=hp)
          - jnp.einsum('ghp,tgpk->gkht', c_im, ab_im, precision=hp)).reshape(g, hc * hc, lc)
    m = _toeplitz(kq)
    rp_re = p_re[:lc][::-1].transpose(1, 0, 2)[:, None]
    rp_im = p_im[:lc][::-1].transpose(1, 0, 2)[:, None]
    tb_re = bb_re.transpose(0, 2, 1)[:, :, None, :]
    tb_im = bb_im.transpose(0, 2, 1)[:, :, None, :]
    w = jnp.concatenate([(rp_re * tb_re - rp_im * tb_im).reshape(g, hc * lc, p),
                         (rp_re * tb_im + rp_im * tb_re).reshape(g, hc * lc, p)], axis=-1)
    np_re = p_re[1:].transpose(1, 2, 0)[:, :, None, :]
    np_im = p_im[1:].transpose(1, 2, 0)[:, :, None, :]
    tc_re = c_re.transpose(0, 2, 1)[:, :, :, None]
    tc_im = c_im.transpose(0, 2, 1)[:, :, :, None]
    r = jnp.concatenate([(np_re * tc_re - np_im * tc_im).reshape(g, p, hc * lc),
                         -(np_re * tc_im + np_im * tc_re).reshape(g, p, hc * lc)], axis=1)
    al_re, al_im = p_re[lc], p_im[lc]
    a = jnp.zeros((g, 8, 2 * p), F32)
    a = a.at[:, 0, :].set(jnp.concatenate([al_re, al_re], axis=-1))
    a = a.at[:, 1, :].set(jnp.concatenate([-al_im, al_im], axis=-1))
    d = jnp.repeat(d_skip, lc, axis=-1).reshape(g, 1, hc * lc)
    return m, w.astype(BF16), r.astype(BF16), a, d


def _attn_kernel(q_ref, k_ref, v_ref, o_ref, m_ref, l_ref, acc_ref, *, blk):
    i = pl.program_id(1)
    q = q_ref[0]
    m_ref[...] = jnp.full_like(m_ref, -jnp.inf)
    l_ref[...] = jnp.zeros_like(l_ref)
    acc_ref[...] = jnp.zeros_like(acc_ref)

    def step(j, masked):
        off = pl.multiple_of(j * blk, blk)
        k = k_ref[0, pl.ds(off, blk), :]
        v = v_ref[0, pl.ds(off, blk), :]
        s = lax.dot_general(q, k, (((1,), (1,)), ((), ())), preferred_element_type=F32)
        if masked:
            row = lax.broadcasted_iota(I32, s.shape, 0)
            col = lax.broadcasted_iota(I32, s.shape, 1)
            s = jnp.where(row >= col, s, -jnp.inf)
        m_prev = m_ref[...]
        m_new = jnp.maximum(m_prev, jnp.max(s, axis=1, keepdims=True))
        p = jnp.exp(s - m_new)
        alpha = jnp.exp(m_prev - m_new)
        l_ref[...] = alpha * l_ref[...] + jnp.sum(p, axis=1, keepdims=True)
        acc_ref[...] = alpha * acc_ref[...] + jnp.dot(p.astype(BF16), v, preferred_element_type=F32)
        m_ref[...] = m_new

    def body(j, c):
        step(j, False)
        return c

    lax.fori_loop(0, i, body, 0)
    step(i, True)
    o_ref[0] = acc_ref[...] / l_ref[...]


def _attention(qa, ka, v, blk):
    bh, s, _ = qa.shape
    kern = functools.partial(_attn_kernel, blk=blk)
    return pl.pallas_call(
        kern,
        grid=(bh, s // blk),
        in_specs=[
            pl.BlockSpec((1, blk, AUG), lambda b, i: (b, i, 0)),
            pl.BlockSpec((1, s, AUG), lambda b, i: (b, 0, 0)),
            pl.BlockSpec((1, s, HEAD_DIM), lambda b, i: (b, 0, 0)),
        ],
        out_specs=pl.BlockSpec((1, blk, HEAD_DIM), lambda b, i: (b, i, 0)),
        out_shape=jax.ShapeDtypeStruct((bh, s, HEAD_DIM), F32),
        scratch_shapes=[pltpu.VMEM((blk, 1), F32), pltpu.VMEM((blk, 1), F32), pltpu.VMEM((blk, HEAD_DIM), F32)],
        compiler_params=pltpu.CompilerParams(
            dimension_semantics=("parallel", "arbitrary"), vmem_limit_bytes=VMEM_LIMIT),
        cost_estimate=pl.CostEstimate(flops=bh * s * s * (AUG + HEAD_DIM), transcendentals=bh * s * s // 2,
                                      bytes_accessed=bh * s * (2 * 2 * AUG + 2 * HEAD_DIM + 4 * HEAD_DIM)),
        name="fox_attn",
    )(qa, ka, v)


def _out_proj_kernel(y_ref, att_ref, h_ref, wg_ref, bg_ref, gs_ref, ga_ref, wos_ref, woa_ref, o_ref):
    g = _gelu(y_ref[...])
    z = jnp.dot(wg_ref[...], g.astype(BF16), preferred_element_type=F32) + bg_ref[...]
    o = g * _sigmoid(z)
    a = o * lax.rsqrt(jnp.mean(o * o, axis=0, keepdims=True) + RMS_EPS) * gs_ref[...]
    acc = h_ref[0] + lax.dot_general(a.astype(BF16), wos_ref[...], (((0,), (0,)), ((), ())),
                                     preferred_element_type=F32)
    ssq = jnp.zeros((acc.shape[0], 1), F32)
    for hh in range(N_HEADS):
        t = att_ref[0, hh]
        ssq = ssq + jnp.sum(t * t, axis=1, keepdims=True)
    inv = lax.rsqrt(ssq / (N_HEADS * HEAD_DIM) + RMS_EPS)
    for hh in range(N_HEADS):
        bh = att_ref[0, hh] * inv * ga_ref[hh]
        acc = acc + jnp.dot(bh.astype(BF16), woa_ref[hh], preferred_element_type=F32)
    o_ref[0] = acc


def _out_proj(y_t, y_att, h, wg_t, bg, gs, ga, wos, woa, tm):
    b, s, d = h.shape
    w = y_t.shape[0]
    ns = s // tm
    return pl.pallas_call(
        _out_proj_kernel,
        grid=(b, ns),
        in_specs=[
            pl.BlockSpec((w, tm), lambda i, j: (0, i * ns + j)),
            pl.BlockSpec((1, N_HEADS, tm, HEAD_DIM), lambda i, j: (i, 0, j, 0)),
            pl.BlockSpec((1, tm, d), lambda i, j: (i, j, 0)),
            pl.BlockSpec((w, w), lambda i, j: (0, 0)),
            pl.BlockSpec((w, 1), lambda i, j: (0, 0)),
            pl.BlockSpec((w, 1), lambda i, j: (0, 0)),
            pl.BlockSpec((N_HEADS, 1, HEAD_DIM), lambda i, j: (0, 0, 0)),
            pl.BlockSpec((w, d), lambda i, j: (0, 0)),
            pl.BlockSpec((N_HEADS, HEAD_DIM, d), lambda i, j: (0, 0, 0)),
        ],
        out_specs=pl.BlockSpec((1, tm, d), lambda i, j: (i, j, 0)),
        out_shape=jax.ShapeDtypeStruct((b, s, d), F32),
        compiler_params=pltpu.CompilerParams(
            dimension_semantics=("parallel", "parallel"), vmem_limit_bytes=VMEM_LIMIT),
        cost_estimate=pl.CostEstimate(flops=2 * b * s * (w * w + 2 * w * d), transcendentals=2 * b * s * w,
                                      bytes_accessed=b * s * (8 * w + 8 * d) + 2 * (w * w + 2 * w * d)),
        name="out_proj",
    )(y_t, y_att, h, wg_t, bg, gs, ga, wos, woa)


def _take_top(vals, payload, k):
    n_rows = vals.shape[0]
    rows = lax.broadcasted_iota(I32, vals.shape, 0)
    tops, picks = [], []
    for _ in range(k):
        m = jnp.max(vals, axis=0, keepdims=True)
        arg = jnp.min(jnp.where(vals == m, rows, n_rows), axis=0, keepdims=True)
        hit = rows == arg
        tops.append(m)
        picks.append(arg if payload is None else jnp.max(jnp.where(hit, payload, -1), axis=0, keepdims=True))
        vals = jnp.where(hit, -jnp.inf, vals)
    return jnp.concatenate(tops, axis=0), jnp.concatenate(picks, axis=0)


def _route_kernel(h_ref, g_ref, wq_ref, keys_ref, hn_ref, idx_ref, gate_ref):
    x = h_ref[...]
    hn = _rms(x, g_ref[...])
    hn_ref[...] = _pack_bf16_pairs(hn)
    q = jnp.dot(hn.astype(BF16), wq_ref[...], preferred_element_type=F32)
    k = PEER_TOPK
    idx_rows, gate_rows = [], []
    for hh in range(PEER_HEADS):
        qh = q[:, 128 * hh: 128 * (hh + 1)].astype(BF16)
        sc = lax.dot_general(keys_ref[hh], qh, (((1,), (1,)), ((), ())), preferred_element_type=F32)
        t1, i1 = _take_top(sc[:PEER_KEYS], None, k)
        t2, i2 = _take_top(sc[PEER_KEYS:], None, k)
        widths = [k // (a + 1) for a in range(k)]
        pad = -sum(widths) % 8
        cand = jnp.concatenate([t1[a:a + 1] + t2[:widths[a]] for a in range(k)]
                               + [jnp.full((pad, t1.shape[1]), -jnp.inf, F32)], axis=0)
        cidx = jnp.concatenate([i1[a:a + 1] * PEER_KEYS + i2[:widths[a]] for a in range(k)]
                               + [jnp.full((pad, t1.shape[1]), -1, I32)], axis=0)
        best, idx = _take_top(cand, cidx, k)
        e = jnp.exp(best - best[0:1])
        gate = e / jnp.sum(e, axis=0, keepdims=True)
        idx_rows.append(idx)
        gate_rows.append(gate)
    idx_ref[...] = jnp.concatenate(idx_rows, axis=0).T
    gate_ref[...] = jnp.concatenate(gate_rows, axis=0).T


def _route(h2, g, wq, keys_cat, tm):
    t, d = h2.shape
    nk = PEER_HEADS * PEER_TOPK
    return pl.pallas_call(
        _route_kernel,
        grid=(t // tm,),
        in_specs=[
            pl.BlockSpec((tm, d), lambda i: (i, 0)),
            pl.BlockSpec((1, d), lambda i: (0, 0)),
            pl.BlockSpec((d, PEER_HEADS * 128), lambda i: (0, 0)),
            pl.BlockSpec((PEER_HEADS, 2 * PEER_KEYS, 128), lambda i: (0, 0, 0)),
        ],
        out_specs=[
            pl.BlockSpec((tm, d // 2), lambda i: (i, 0)),
            pl.BlockSpec((tm, nk), lambda i: (i, 0)),
            pl.BlockSpec((tm, nk), lambda i: (i, 0)),
        ],
        out_shape=[
            jax.ShapeDtypeStruct((t, d // 2), I32),
            jax.ShapeDtypeStruct((t, nk), I32),
            jax.ShapeDtypeStruct((t, nk), F32),
        ],
        compiler_params=pltpu.CompilerParams(
            dimension_semantics=("parallel",), vmem_limit_bytes=VMEM_LIMIT),
        cost_estimate=pl.CostEstimate(flops=2 * t * d * PEER_HEADS * 128 + 2 * t * PEER_HEADS * 256 * 128,
                                      transcendentals=t * nk, bytes_accessed=t * (8 * d + 8 * nk) + 2 * d * PEER_HEADS * 128),
        name="peer_route",
    )(h2, g, wq, keys_cat)


def _final_norm_kernel(x_ref, g_ref, o_ref):
    o_ref[...] = _rms(x_ref[...], g_ref[...])


def _final_norm(x2, g, tm):
    t, d = x2.shape
    return pl.pallas_call(
        _final_norm_kernel,
        grid=(t // tm,),
        in_specs=[pl.BlockSpec((tm, d), lambda i: (i, 0)), pl.BlockSpec((1, d), lambda i: (0, 0))],
        out_specs=pl.BlockSpec((tm, d), lambda i: (i, 0)),
        out_shape=jax.ShapeDtypeStruct((t, d), F32),
        compiler_params=pltpu.CompilerParams(dimension_semantics=("parallel",)),
        cost_estimate=pl.CostEstimate(flops=4 * t * d, transcendentals=t, bytes_accessed=8 * t * d),
        name="final_norm",
    )(x2, g)


SC_WORKERS = 32
SC_CORES = 2
SC_LANES = 16
SC_TOK = 16
SC_ROWS = 16
SC_BUFS = 4
SC_SLAB = 8
HI16 = -65536


def _sc_params():
    cp = pltpu.CompilerParams()
    if "needs_layout_passes" in pltpu.CompilerParams.__dataclass_fields__:
        cp = pltpu.CompilerParams(needs_layout_passes=False)
    return cp


def _sc_worker_id():
    return lax.axis_index("s") * SC_CORES + lax.axis_index("c")


def _tree_sum(xs):
    xs = list(xs)
    while len(xs) > 1:
        xs = [xs[i] + xs[i + 1] for i in range(0, len(xs) - 1, 2)] + ([xs[-1]] if len(xs) % 2 else [])
    return xs[0]


def _gelu_via_exp(x):
    c = math.sqrt(2.0 / math.pi)
    z = c * (x + 0.044715 * (x * x * x))
    return 0.5 * x * (2.0 - 2.0 / (1.0 + jnp.exp(2.0 * z)))


def _round_to_bf16_bits(bits):
    return bits + 0x7FFF + (lax.shift_right_logical(bits, jnp.int32(16)) & 1)


def _pack_bf16_pairs(x):
    w = x.shape[-1] // 2
    r = _round_to_bf16_bits(lax.bitcast_convert_type(x, I32))
    return lax.shift_right_logical(r[..., :w], jnp.int32(16)) | (r[..., w:] & HI16)


def _lo_f32(word):
    return lax.bitcast_convert_type(lax.shift_left(word, jnp.int32(16)), F32)


def _hi_f32(word):
    return lax.bitcast_convert_type(word & HI16, F32)


def _peer_experts_body(hn_hbm, h_hbm, idx_hbm, gate_hbm, u_hbm, v_hbm, out_hbm,
                       h_v, o_v, idx_v, gate_v, coef_v, cw_v, acc_v, *bufs, tok_per_worker):
    rows, sems = bufs[:SC_BUFS], bufs[SC_BUFS:]
    half = hn_hbm.shape[1]
    nk = PEER_HEADS * PEER_TOPK
    per_tok = nk // SC_ROWS
    n = SC_TOK * per_tok
    n_slab = half // (SC_SLAB * SC_LANES)
    lane = lax.iota(I32, SC_LANES)
    wid = _sc_worker_id()

    def packed_mul(word, other_bf16):
        return plsc.bitcast(plsc.bitcast(word, BF16) * other_bf16, I32)

    def gather_u(k, slot):
        return pltpu.make_async_copy(u_hbm.at[idx_v.at[k]], rows[slot], sems[slot])

    def gather_v(k, slot):
        return pltpu.make_async_copy(v_hbm.at[idx_v.at[k]], rows[slot], sems[slot])

    def start_ahead(ahead, slot):
        @pl.when(ahead < n)
        def _():
            gather_u(ahead, slot).start()

        @pl.when(jnp.logical_and(ahead >= n, ahead < 2 * n))
        def _():
            gather_v(ahead - n, slot).start()

    def score(rw, k):
        t = k // per_tok
        for sl in range(n_slab):
            base = sl * SC_SLAB * SC_LANES
            hw = [plsc.bitcast(h_v[t, pl.ds(base + SC_LANES * j, SC_LANES)], BF16) for j in range(SC_SLAB)]

            @plsc.parallel_loop(0, SC_ROWS, 1)
            def _(r):
                terms = []
                for j in range(SC_SLAB):
                    p = packed_mul(rw[r, pl.ds(base + SC_LANES * j, SC_LANES)], hw[j])
                    terms += [_lo_f32(p), _hi_f32(p)]
                p = _tree_sum(terms)
                off = pl.multiple_of(r * SC_LANES, SC_LANES)
                if sl == 0:
                    acc_v[pl.ds(off, SC_LANES)] = p
                else:
                    acc_v[pl.ds(off, SC_LANES)] = acc_v[pl.ds(off, SC_LANES)] + p

        for rg in range(SC_ROWS // SC_LANES):
            base = rg * SC_LANES * SC_LANES
            tot = _tree_sum([plsc.load_gather(acc_v, [lane * SC_LANES + (base + j)]) for j in range(SC_LANES)])
            coef_v[pl.ds(pl.multiple_of(k * SC_ROWS + rg * SC_LANES, SC_LANES), SC_LANES)] = tot

    def combine(rw, k):
        t = k // per_tok
        for sl in range(n_slab):
            base = sl * SC_SLAB * SC_LANES
            acc0 = (tuple(o_v[t, pl.ds(base + SC_LANES * j, SC_LANES)] for j in range(SC_SLAB))
                    + tuple(o_v[t, pl.ds(half + base + SC_LANES * j, SC_LANES)] for j in range(SC_SLAB)))

            def rbody(r, acc):
                w = plsc.bitcast(plsc.load_gather(cw_v, [jnp.full((SC_LANES,), k * SC_ROWS + r, I32)]), BF16)
                new = list(acc)
                for j in range(SC_SLAB):
                    p = packed_mul(rw[r, pl.ds(base + SC_LANES * j, SC_LANES)], w)
                    new[j] = acc[j] + _lo_f32(p)
                    new[SC_SLAB + j] = acc[SC_SLAB + j] + _hi_f32(p)
                return tuple(new)

            acc = lax.fori_loop(0, SC_ROWS, rbody, acc0)
            for j in range(SC_SLAB):
                o_v[t, pl.ds(base + SC_LANES * j, SC_LANES)] = acc[j]
                o_v[t, pl.ds(half + base + SC_LANES * j, SC_LANES)] = acc[SC_SLAB + j]

    def batch(bi, carry):
        t0 = pl.multiple_of(wid * tok_per_worker + bi * SC_TOK, SC_TOK)
        pltpu.sync_copy(idx_hbm.at[pl.ds(t0 * per_tok, n)], idx_v)
        for s in range(SC_BUFS - 1):
            gather_u(s, s).start()
        pltpu.sync_copy(hn_hbm.at[pl.ds(t0, SC_TOK)], h_v)
        pltpu.sync_copy(h_hbm.at[pl.ds(t0, SC_TOK)], o_v)
        pltpu.sync_copy(gate_hbm.at[pl.ds(t0 * nk, SC_TOK * nk)], gate_v)

        def score_group(i, c):
            for s in range(SC_BUFS):
                k = i * SC_BUFS + s
                start_ahead(k + SC_BUFS - 1, (s + SC_BUFS - 1) % SC_BUFS)
                gather_u(k, s).wait()
                score(rows[s], k)
            return c

        lax.fori_loop(0, n // SC_BUFS, score_group, 0)

        @plsc.parallel_loop(0, SC_TOK * nk // SC_LANES, 1)
        def _(j):
            off = pl.multiple_of(j * SC_LANES, SC_LANES)
            c = _gelu_via_exp(coef_v[pl.ds(off, SC_LANES)]) * gate_v[pl.ds(off, SC_LANES)]
            hi = _round_to_bf16_bits(lax.bitcast_convert_type(c, I32)) & HI16
            cw_v[pl.ds(off, SC_LANES)] = hi | lax.shift_right_logical(hi, jnp.int32(16))

        def combine_group(i, c):
            for s in range(SC_BUFS):
                k = i * SC_BUFS + s
                start_ahead(n + k + SC_BUFS - 1, (s + SC_BUFS - 1) % SC_BUFS)
                gather_v(k, s).wait()
                combine(rows[s], k)
            return c

        lax.fori_loop(0, n // SC_BUFS, combine_group, 0)
        pltpu.sync_copy(o_v, out_hbm.at[pl.ds(t0, SC_TOK)])
        return carry

    lax.fori_loop(0, tok_per_worker // SC_TOK, batch, 0)


def _peer_experts(hn_words, h2, idx4, gate_flat, u_words, v_words):
    t, d = h2.shape
    nk = PEER_HEADS * PEER_TOPK
    mesh = plsc.VectorSubcoreMesh(core_axis_name="c", subcore_axis_name="s")
    body = functools.partial(_peer_experts_body, tok_per_worker=t // SC_WORKERS)
    return pl.kernel(
        body,
        out_type=jax.ShapeDtypeStruct((t, d), F32),
        mesh=mesh,
        scratch_types=[
            pltpu.VMEM((SC_TOK, d // 2), I32),
            pltpu.VMEM((SC_TOK, d), F32),
            pltpu.VMEM((SC_TOK * nk // SC_ROWS, SC_ROWS), I32),
            pltpu.VMEM((SC_TOK * nk,), F32),
            pltpu.VMEM((SC_TOK * nk,), F32),
            pltpu.VMEM((SC_TOK * nk,), I32),
            pltpu.VMEM((SC_ROWS * SC_LANES,), F32),
        ] + [pltpu.VMEM((SC_ROWS, d // 2), I32)] * SC_BUFS + [pltpu.SemaphoreType.DMA] * SC_BUFS,
        compiler_params=_sc_params(),
        cost_estimate=pl.CostEstimate(flops=4 * t * nk * d, transcendentals=t * nk,
                                      bytes_accessed=4 * t * nk * d + 10 * t * d),
        name="peer_experts_sc",
    )(hn_words, h2, idx4, gate_flat, u_words, v_words)


def _mixers(h, l, norm1_g, w_in, ssm, fox_b_f, g_ssm_out, g_attn_out, w_o, ssm_w_glu, ssm_b_glu, tm, blk):
    b, s, d = h.shape
    w = N_HEADS * HEAD_DIM
    wl = w_in[l]
    wu_t = wl[:, :w].T.astype(BF16)
    wm = wl[:, w:4 * w].astype(BF16)
    wf = jnp.pad(wl[:, 4 * w:], ((0, 0), (0, 128 - N_HEADS)))
    bf = jnp.pad(fox_b_f[l], (0, 128 - N_HEADS)).reshape(1, 128)
    u_t, qa, ka, v = _in_proj(h, norm1_g[l].reshape(1, d), wu_t, wm, wf, bf, tm)

    lc = SSM_CHUNK
    nc = s // lc
    g = w // SSM_GROUP_CH
    y4 = _ssm(u_t.reshape(g, SSM_GROUP_CH, b * nc, lc), *ssm, n_chunks=nc, batch=b)
    y_t = y4.reshape(w, b * s)

    y_att = _attention(qa.reshape(b * N_HEADS, s, AUG), ka.reshape(b * N_HEADS, s, AUG),
                       v.reshape(b * N_HEADS, s, HEAD_DIM), blk).reshape(b, N_HEADS, s, HEAD_DIM)

    wo = w_o[l].astype(BF16)
    return _out_proj(y_t, y_att, h, ssm_w_glu[l].T.astype(BF16), ssm_b_glu[l].reshape(w, 1),
                     g_ssm_out[l].reshape(w, 1), g_attn_out[l].reshape(N_HEADS, 1, HEAD_DIM),
                     wo[:w], wo[w:].reshape(N_HEADS, HEAD_DIM, d), tm)


def _keys_cat(keys_l):
    z = jnp.zeros_like(keys_l[:, 0])
    top = jnp.concatenate([keys_l[:, 0], z], axis=-1)
    bot = jnp.concatenate([z, keys_l[:, 1]], axis=-1)
    return jnp.concatenate([top, bot], axis=1).astype(BF16)


def kernel(x, norm1_g, w_in, ssm_lambda_re, ssm_lambda_im, ssm_log_dt, ssm_b_re, ssm_b_im, ssm_c_re, ssm_c_im, ssm_d, ssm_w_glu, ssm_b_glu, fox_b_f, g_ssm_out, g_attn_out, w_o, norm2_g, peer_w_q, peer_keys, peer_u, peer_v, norm_f):
    b, s, d = x.shape
    depth = w_in.shape[0]
    nk = PEER_HEADS * PEER_TOPK
    tm = min(512, s)
    blk = min(512, s)
    assert b % (2 * N_PARTS) == 0
    bh = b // N_PARTS
    ssm_tabs = [_ssm_tables(ssm_lambda_re[l], ssm_lambda_im[l], ssm_log_dt[l], ssm_b_re[l], ssm_b_im[l],
                            ssm_c_re[l], ssm_c_im[l], ssm_d[l]) for l in range(depth)]

    def dense_stage(h, l, early=None):
        t = h.shape[0] * s
        h = _mixers(h, l, norm1_g, w_in, ssm_tabs[l], fox_b_f, g_ssm_out, g_attn_out, w_o, ssm_w_glu, ssm_b_glu,
                    tm, blk)
        if early is not None:
            h, _ = lax.optimization_barrier((h, early))
        h2 = h.reshape(t, d)
        hn, idx, gate = _route(h2, norm2_g[l].reshape(1, d), peer_w_q[l].astype(BF16), _keys_cat(peer_keys[l]),
                               min(256, t))
        return h2, hn, idx.reshape(t * nk // SC_ROWS, SC_ROWS), gate.reshape(t * nk)

    tabs = [(_pack_bf16_pairs(peer_u[l]), _pack_bf16_pairs(peer_v[l])) for l in range(depth)]

    def expert_stage(st, l):
        h2, hn, idx4, gate = st
        return _peer_experts(hn, h2, idx4, gate, tabs[l][0], tabs[l][1]).reshape(-1, s, d)

    def after(value, st):
        value, _ = lax.optimization_barrier((value, st[3]))
        return value

    hs = [x[p * bh:(p + 1) * bh] for p in range(N_PARTS)]
    prev = None
    chain = []
    for l in range(depth):
        for p in range(N_PARTS):
            pieces = [hs[p][:bh // 2], hs[p][bh // 2:]] if prev is None else [hs[p]]
            done = []
            for piece in pieces:
                early = tabs[0] if prev is None else (tabs[l + 1] if p == N_PARTS - 1 and l + 1 < depth else None)
                if prev is not None:
                    piece = after(piece, prev)
                if len(chain) >= 2 and l == 0:
                    piece, _ = lax.optimization_barrier((piece, chain[-2]))
                st = dense_stage(piece, l, early)
                done.append(expert_stage(st, l))
                chain.append(done[-1])
                prev = st
            hs[p] = done[0] if len(done) == 1 else jnp.concatenate(done, axis=0)
    t = bh * s
    outs = [_final_norm(h.reshape(t, d), norm_f.reshape(1, d), min(512, t)).reshape(bh, s, d) for h in hs]
    return jnp.concatenate(outs, axis=0)
```

```python
import functools
import math

import jax
import jax.numpy as jnp
from jax import lax
from jax.experimental import pallas as pl
from jax.experimental.pallas import tpu as pltpu
from jax.experimental.pallas import tpu_sc as plsc

F32 = jnp.float32
BF16 = jnp.bfloat16
I32 = jnp.int32

RMS_EPS = 1e-6
SSM_GROUP_CH = 16
SSM_STATE = 64
SSM_CHUNK = 128
HEAD_DIM = 64
N_HEADS = 8
AUG = 128
PEER_HEADS = 8
PEER_KEYS = 128
PEER_TOPK = 16
VMEM_LIMIT = 56 * 1024 * 1024
N_PARTS = 2


def _rms(x, g):
    return x * lax.rsqrt(jnp.mean(x * x, axis=-1, keepdims=True) + RMS_EPS) * g


def _gelu(x):
    c = math.sqrt(2.0 / math.pi)
    return 0.5 * x * (1.0 + jnp.tanh(c * (x + 0.044715 * (x * x * x))))


def _sigmoid(x):
    return 1.0 / (1.0 + jnp.exp(-x))


def _in_proj_kernel(h_ref, g_ref, wu_ref, wm_ref, wf_ref, bf_ref, u_ref, qa_ref, ka_ref, v_ref, cum_ref):
    j = pl.program_id(1)

    @pl.when(j == 0)
    def _():
        cum_ref[...] = jnp.zeros_like(cum_ref)

    x = h_ref[0]
    tm = x.shape[0]
    xn = _rms(x, g_ref[...])
    xb = xn.astype(BF16)
    proj = jnp.dot(xb, wm_ref[...], preferred_element_type=F32)
    u_ref[...] = lax.dot_general(wu_ref[...], xb, (((1,), (1,)), ((), ())),
                                 preferred_element_type=F32).astype(BF16)
    f = jnp.dot(xn, wf_ref[...], precision=lax.Precision.HIGHEST, preferred_element_type=F32) + bf_ref[...]
    logf = jnp.minimum(f, 0.0) - jnp.log(1.0 + jnp.exp(-jnp.abs(f)))
    row = lax.broadcasted_iota(I32, (tm, tm), 0)
    col = lax.broadcasted_iota(I32, (tm, tm), 1)
    tri = (row >= col).astype(F32)
    cum = jnp.dot(tri, logf, precision=lax.Precision.HIGHEST, preferred_element_type=F32) + cum_ref[0:1, :]
    cum_ref[0:1, :] = cum[tm - 1:tm, :]

    w = HEAD_DIM * N_HEADS
    lane = lax.broadcasted_iota(I32, (tm, AUG), 1)
    scale = HEAD_DIM ** -0.5
    for hh in range(N_HEADS):
        pair = hh // 2
        q2 = proj[:, 128 * pair: 128 * pair + 128]
        k2 = proj[:, w + 128 * pair: w + 128 * pair + 128]
        v2 = proj[:, 2 * w + 128 * pair: 2 * w + 128 * pair + 128]
        if hh % 2 == 1:
            q2 = pltpu.roll(q2, 64, axis=1)
            k2 = pltpu.roll(k2, 64, axis=1)
            vh = v2[:, 64:]
        else:
            vh = v2[:, :64]
        c = jnp.broadcast_to(cum[:, hh:hh + 1], (tm, AUG))
        c1 = c.astype(BF16).astype(F32)
        r1 = c - c1
        c2 = r1.astype(BF16).astype(F32)
        c3 = r1 - c2
        one = jnp.ones((tm, AUG), F32)
        zero = jnp.zeros((tm, AUG), F32)
        qa = jnp.where(lane < 64, q2 * scale,
             jnp.where(lane == 64, c1, jnp.where(lane == 65, c2, jnp.where(lane == 66, c3,
             jnp.where(lane < 70, one, zero)))))
        ka = jnp.where(lane < 64, k2,
             jnp.where(lane < 67, one, jnp.where(lane == 67, -c1, jnp.where(lane == 68, -c2,
             jnp.where(lane == 69, -c3, zero)))))
        qa_ref[0, hh] = qa.astype(BF16)
        ka_ref[0, hh] = ka.astype(BF16)
        v_ref[0, hh] = vh.astype(BF16)


def _in_proj(h, g, wu_t, wm, wf, bf, tm):
    b, s, d = h.shape
    w = HEAD_DIM * N_HEADS
    ns = s // tm
    return pl.pallas_call(
        _in_proj_kernel,
        grid=(b, ns),
        in_specs=[
            pl.BlockSpec((1, tm, d), lambda i, j: (i, j, 0)),
            pl.BlockSpec((1, d), lambda i, j: (0, 0)),
            pl.BlockSpec((w, d), lambda i, j: (0, 0)),
            pl.BlockSpec((d, 3 * w), lambda i, j: (0, 0)),
            pl.BlockSpec((d, 128), lambda i, j: (0, 0)),
            pl.BlockSpec((1, 128), lambda i, j: (0, 0)),
        ],
        out_specs=[
            pl.BlockSpec((w, tm), lambda i, j: (0, i * ns + j)),
            pl.BlockSpec((1, N_HEADS, tm, AUG), lambda i, j: (i, 0, j, 0)),
            pl.BlockSpec((1, N_HEADS, tm, AUG), lambda i, j: (i, 0, j, 0)),
            pl.BlockSpec((1, N_HEADS, tm, HEAD_DIM), lambda i, j: (i, 0, j, 0)),
        ],
        out_shape=[
            jax.ShapeDtypeStruct((w, b * s), BF16),
            jax.ShapeDtypeStruct((b, N_HEADS, s, AUG), BF16),
            jax.ShapeDtypeStruct((b, N_HEADS, s, AUG), BF16),
            jax.ShapeDtypeStruct((b, N_HEADS, s, HEAD_DIM), BF16),
        ],
        scratch_shapes=[pltpu.VMEM((8, 128), F32)],
        compiler_params=pltpu.CompilerParams(
            dimension_semantics=("parallel", "arbitrary"), vmem_limit_bytes=VMEM_LIMIT),
        cost_estimate=pl.CostEstimate(flops=2 * b * s * d * (4 * w + 128), transcendentals=2 * b * s * 128,
                                      bytes_accessed=4 * b * s * d + 2 * d * (4 * w) + 2 * b * s * (w + 2 * N_HEADS * AUG + w)),
        name="in_proj",
    )(h, g, wu_t, wm, wf, bf)


def _toeplitz_kernel(k_ref, m_ref):
    lc = SSM_CHUNK
    hc = SSM_GROUP_CH
    row = lax.broadcasted_iota(I32, (lc, lc), 0)
    col = lax.broadcasted_iota(I32, (lc, lc), 1)
    causal = col >= row

    def body(hi, carry):
        r0 = pl.multiple_of(hi * lc, lc)
        for ho in range(hc):
            k = jnp.broadcast_to(k_ref[0, pl.ds(hi * hc + ho, 1), :], (lc, lc))
            t = pltpu.roll(k, 0, 1, stride=1, stride_axis=0)
            m_ref[0, pl.ds(r0, lc), ho * lc:(ho + 1) * lc] = jnp.where(causal, t, 0.0).astype(BF16)
        return carry

    lax.fori_loop(0, hc, body, 0)


def _toeplitz(kq):
    g, hh, lc = kq.shape
    width = SSM_GROUP_CH * lc
    return pl.pallas_call(
        _toeplitz_kernel,
        grid=(g,),
        in_specs=[pl.BlockSpec((1, hh, lc), lambda i: (i, 0, 0))],
        out_specs=pl.BlockSpec((1, width, width), lambda i: (i, 0, 0)),
        out_shape=jax.ShapeDtypeStruct((g, width, width), BF16),
        compiler_params=pltpu.CompilerParams(dimension_semantics=("parallel",), vmem_limit_bytes=VMEM_LIMIT),
        cost_estimate=pl.CostEstimate(flops=g * width * width, transcendentals=0,
                                      bytes_accessed=2 * g * width * width + 4 * g * hh * lc),
        name="ssm_toeplitz",
    )(kq)


def _ssm_kernel(u_ref, m_ref, w_ref, r_ref, a_ref, d_ref, y_ref, e_scr, *, n_chunks, batch):
    hc = SSM_GROUP_CH
    lc = SSM_CHUNK
    rows_n = u_ref.shape[1] // lc
    u = jnp.concatenate([u_ref[r:r + 1, :].reshape(rows_n, lc) for r in range(hc)], axis=1)
    y = jnp.dot(u, m_ref[0], preferred_element_type=F32)
    e_scr[...] = jnp.dot(u, w_ref[0], preferred_element_type=F32)
    ar = a_ref[0, 0:1, :]
    ai = a_ref[0, 1:2, :]

    def body(c, s):
        rows = pl.ds(c, batch, stride=n_chunks)
        e_c = e_scr[rows, :]
        e_scr[rows, :] = s
        return ar * s + ai * pltpu.roll(s, SSM_STATE, axis=1) + e_c

    lax.fori_loop(0, n_chunks, body, jnp.zeros((batch, 2 * SSM_STATE), F32))
    y = y + jnp.dot(e_scr[...].astype(BF16), r_ref[0], preferred_element_type=F32)
    y = y + u.astype(F32) * d_ref[0]
    for r in range(hc):
        y_ref[0, r] = y[:, r * lc:(r + 1) * lc]


def _ssm(u_t, m, w, r, a, d, n_chunks, batch):
    hc, lc = SSM_GROUP_CH, SSM_CHUNK
    g = u_t.shape[0] // hc
    rows = u_t.shape[1] // lc
    width = hc * lc
    kern = functools.partial(_ssm_kernel, n_chunks=n_chunks, batch=batch)
    return pl.pallas_call(
        kern,
        grid=(g,),
        in_specs=[
            pl.BlockSpec((hc, rows * lc), lambda i: (i, 0)),
            pl.BlockSpec((1, width, width), lambda i: (i, 0, 0)),
            pl.BlockSpec((1, width, 2 * SSM_STATE), lambda i: (i, 0, 0)),
            pl.BlockSpec((1, 2 * SSM_STATE, width), lambda i: (i, 0, 0)),
            pl.BlockSpec((1, 8, 2 * SSM_STATE), lambda i: (i, 0, 0)),
            pl.BlockSpec((1, 1, width), lambda i: (i, 0, 0)),
        ],
        out_specs=pl.BlockSpec((1, hc, rows, lc), lambda i: (i, 0, 0, 0)),
        out_shape=jax.ShapeDtypeStruct((g, hc, rows, lc), F32),
        scratch_shapes=[pltpu.VMEM((rows, 2 * SSM_STATE), F32)],
        compiler_params=pltpu.CompilerParams(
            dimension_semantics=("parallel",), vmem_limit_bytes=VMEM_LIMIT),
        cost_estimate=pl.CostEstimate(flops=2 * g * rows * width * (width + 4 * SSM_STATE), transcendentals=0,
                                      bytes_accessed=g * (2 * width * (width + 4 * SSM_STATE) + 6 * rows * width)),
        name="ssm",
    )(u_t, m, w, r, a, d)


def _ssm_tables(lam_re, lam_im, log_dt, b_re, b_im, c_re, c_im, d_skip):
    hp = lax.Precision.HIGHEST
    lc = SSM_CHUNK
    g, p = lam_re.shape
    hc = SSM_GROUP_CH
    dt = jnp.exp(log_dt)[:, None]
    a_re = jnp.exp(lam_re * dt) * jnp.cos(lam_im * dt)
    a_im = jnp.exp(lam_re * dt) * jnp.sin(lam_im * dt)
    den = lam_re * lam_re + lam_im * lam_im
    nr = a_re - 1.0
    z_re = (nr * lam_re + a_im * lam_im) / den
    z_im = (a_im * lam_re - nr * lam_im) / den
    bb_re = z_re[..., None] * b_re - z_im[..., None] * b_im
    bb_im = z_re[..., None] * b_im + z_im[..., None] * b_re
    tau = jnp.arange(lc + 1, dtype=F32)[:, None, None]
    mag = jnp.exp(tau * (lam_re * dt)[None])
    ang = tau * (lam_im * dt)[None]
    p_re = mag * jnp.cos(ang)
    p_im = mag * jnp.sin(ang)
    ab_re = p_re[:lc, :, :, None] * bb_re[None] - p_im[:lc, :, :, None] * bb_im[None]
    ab_im = p_re[:lc, :, :, None] * bb_im[None] + p_im[:lc, :, :, None] * bb_re[None]
    kq = (jnp.einsum('ghp,tgpk->gkht', c_re, ab_re, precision=hp)
          - jnp.einsum('ghp,tgpk->gkht', c_im, ab_im, precision=hp)).reshape(g, hc * hc, lc)
    m = _toeplitz(kq)
    rp_re = p_re[:lc][::-1].transpose(1, 0, 2)[:, None]
    rp_im = p_im[:lc][::-1].transpose(1, 0, 2)[:, None]
    tb_re = bb_re.transpose(0, 2, 1)[:, :, None, :]
    tb_im = bb_im.transpose(0, 2, 1)[:, :, None, :]
    w = jnp.concatenate([(rp_re * tb_re - rp_im * tb_im).reshape(g, hc * lc, p),
                         (rp_re * tb_im + rp_im * tb_re).reshape(g, hc * lc, p)], axis=-1)
    np_re = p_re[1:].transpose(1, 2, 0)[:, :, None, :]
    np_im = p_im[1:].transpose(1, 2, 0)[:, :, None, :]
    tc_re = c_re.transpose(0, 2, 1)[:, :, :, None]
    tc_im = c_im.transpose(0, 2, 1)[:, :, :, None]
    r = jnp.concatenate([(np_re * tc_re - np_im * tc_im).reshape(g, p, hc * lc),
                         -(np_re * tc_im + np_im * tc_re).reshape(g, p, hc * lc)], axis=1)
    al_re, al_im = p_re[lc], p_im[lc]
    a = jnp.zeros((g, 8, 2 * p), F32)
    a = a.at[:, 0, :].set(jnp.concatenate([al_re, al_re], axis=-1))
    a = a.at[:, 1, :].set(jnp.concatenate([-al_im, al_im], axis=-1))
    d = jnp.repeat(d_skip, lc, axis=-1).reshape(g, 1, hc * lc)
    return m, w.astype(BF16), r.astype(BF16), a, d


def _attn_kernel(q_ref, k_ref, v_ref, o_ref, m_ref, l_ref, acc_ref, *, blk):
    i = pl.program_id(1)
    q = q_ref[0]
    m_ref[...] = jnp.full_like(m_ref, -jnp.inf)
    l_ref[...] = jnp.zeros_like(l_ref)
    acc_ref[...] = jnp.zeros_like(acc_ref)

    def step(j, masked):
        off = pl.multiple_of(j * blk, blk)
        k = k_ref[0, pl.ds(off, blk), :]
        v = v_ref[0, pl.ds(off, blk), :]
        s = lax.dot_general(q, k, (((1,), (1,)), ((), ())), preferred_element_type=F32)
        if masked:
            row = lax.broadcasted_iota(I32, s.shape, 0)
            col = lax.broadcasted_iota(I32, s.shape, 1)
            s = jnp.where(row >= col, s, -jnp.inf)
        m_prev = m_ref[...]
        m_new = jnp.maximum(m_prev, jnp.max(s, axis=1, keepdims=True))
        p = jnp.exp(s - m_new)
        alpha = jnp.exp(m_prev - m_new)
        l_ref[...] = alpha * l_ref[...] + jnp.sum(p, axis=1, keepdims=True)
        acc_ref[...] = alpha * acc_ref[...] + jnp.dot(p.astype(BF16), v, preferred_element_type=F32)
        m_ref[...] = m_new

    def body(j, c):
        step(j, False)
        return c

    lax.fori_loop(0, i, body, 0)
    step(i, True)
    o_ref[0] = acc_ref[...] / l_ref[...]


def _attention(qa, ka, v, blk):
    bh, s, _ = qa.shape
    kern = functools.partial(_attn_kernel, blk=blk)
    return pl.pallas_call(
        kern,
        grid=(bh, s // blk),
        in_specs=[
            pl.BlockSpec((1, blk, AUG), lambda b, i: (b, i, 0)),
            pl.BlockSpec((1, s, AUG), lambda b, i: (b, 0, 0)),
            pl.BlockSpec((1, s, HEAD_DIM), lambda b, i: (b, 0, 0)),
        ],
        out_specs=pl.BlockSpec((1, blk, HEAD_DIM), lambda b, i: (b, i, 0)),
        out_shape=jax.ShapeDtypeStruct((bh, s, HEAD_DIM), F32),
        scratch_shapes=[pltpu.VMEM((blk, 1), F32), pltpu.VMEM((blk, 1), F32), pltpu.VMEM((blk, HEAD_DIM), F32)],
        compiler_params=pltpu.CompilerParams(
            dimension_semantics=("parallel", "arbitrary"), vmem_limit_bytes=VMEM_LIMIT),
        cost_estimate=pl.CostEstimate(flops=bh * s * s * (AUG + HEAD_DIM), transcendentals=bh * s * s // 2,
                                      bytes_accessed=bh * s * (2 * 2 * AUG + 2 * HEAD_DIM + 4 * HEAD_DIM)),
        name="fox_attn",
    )(qa, ka, v)


def _out_proj_kernel(y_ref, att_ref, h_ref, wg_ref, bg_ref, gs_ref, ga_ref, wos_ref, woa_ref, o_ref):
    g = _gelu(y_ref[...])
    z = jnp.dot(wg_ref[...], g.astype(BF16), preferred_element_type=F32) + bg_ref[...]
    o = g * _sigmoid(z)
    a = o * lax.rsqrt(jnp.mean(o * o, axis=0, keepdims=True) + RMS_EPS) * gs_ref[...]
    acc = h_ref[0] + lax.dot_general(a.astype(BF16), wos_ref[...], (((0,), (0,)), ((), ())),
                                     preferred_element_type=F32)
    ssq = jnp.zeros((acc.shape[0], 1), F32)
    for hh in range(N_HEADS):
        t = att_ref[0, hh]
        ssq = ssq + jnp.sum(t * t, axis=1, keepdims=True)
    inv = lax.rsqrt(ssq / (N_HEADS * HEAD_DIM) + RMS_EPS)
    for hh in range(N_HEADS):
        bh = att_ref[0, hh] * inv * ga_ref[hh]
        acc = acc + jnp.dot(bh.astype(BF16), woa_ref[hh], preferred_element_type=F32)
    o_ref[0] = acc


def _out_proj(y_t, y_att, h, wg_t, bg, gs, ga, wos, woa, tm):
    b, s, d = h.shape
    w = y_t.shape[0]
    ns = s // tm
    return pl.pallas_call(
        _out_proj_kernel,
        grid=(b, ns),
        in_specs=[
            pl.BlockSpec((w, tm), lambda i, j: (0, i * ns + j)),
            pl.BlockSpec((1, N_HEADS, tm, HEAD_DIM), lambda i, j: (i, 0, j, 0)),
            pl.BlockSpec((1, tm, d), lambda i, j: (i, j, 0)),
            pl.BlockSpec((w, w), lambda i, j: (0, 0)),
            pl.BlockSpec((w, 1), lambda i, j: (0, 0)),
            pl.BlockSpec((w, 1), lambda i, j: (0, 0)),
            pl.BlockSpec((N_HEADS, 1, HEAD_DIM), lambda i, j: (0, 0, 0)),
            pl.BlockSpec((w, d), lambda i, j: (0, 0)),
            pl.BlockSpec((N_HEADS, HEAD_DIM, d), lambda i, j: (0, 0, 0)),
        ],
        out_specs=pl.BlockSpec((1, tm, d), lambda i, j: (i, j, 0)),
        out_shape=jax.ShapeDtypeStruct((b, s, d), F32),
        compiler_params=pltpu.CompilerParams(
            dimension_semantics=("parallel", "parallel"), vmem_limit_bytes=VMEM_LIMIT),
        cost_estimate=pl.CostEstimate(flops=2 * b * s * (w * w + 2 * w * d), transcendentals=2 * b * s * w,
                                      bytes_accessed=b * s * (8 * w + 8 * d) + 2 * (w * w + 2 * w * d)),
        name="out_proj",
    )(y_t, y_att, h, wg_t, bg, gs, ga, wos, woa)


def _take_top(vals, payload, k):
    n_rows = vals.shape[0]
    rows = lax.broadcasted_iota(I32, vals.shape, 0)
    tops, picks = [], []
    for _ in range(k):
        m = jnp.max(vals, axis=0, keepdims=True)
        arg = jnp.min(jnp.where(vals == m, rows, n_rows), axis=0, keepdims=True)
        hit = rows == arg
        tops.append(m)
        picks.append(arg if payload is None else jnp.max(jnp.where(hit, payload, -1), axis=0, keepdims=True))
        vals = jnp.where(hit, -jnp.inf, vals)
    return jnp.concatenate(tops, axis=0), jnp.concatenate(picks, axis=0)


def _route_kernel(h_ref, g_ref, wq_ref, keys_ref, hn_ref, idx_ref, gate_ref):
    x = h_ref[...]
    hn = _rms(x, g_ref[...])
    hn_ref[...] = _pack_bf16_pairs(hn)
    q = jnp.dot(hn.astype(BF16), wq_ref[...], preferred_element_type=F32)
    k = PEER_TOPK
    idx_rows, gate_rows = [], []
    for hh in range(PEER_HEADS):
        qh = q[:, 128 * hh: 128 * (hh + 1)].astype(BF16)
        sc = lax.dot_general(keys_ref[hh], qh, (((1,), (1,)), ((), ())), preferred_element_type=F32)
        t1, i1 = _take_top(sc[:PEER_KEYS], None, k)
        t2, i2 = _take_top(sc[PEER_KEYS:], None, k)
        widths = [k // (a + 1) for a in range(k)]
        pad = -sum(widths) % 8
        cand = jnp.concatenate([t1[a:a + 1] + t2[:widths[a]] for a in range(k)]
                               + [jnp.full((pad, t1.shape[1]), -jnp.inf, F32)], axis=0)
        cidx = jnp.concatenate([i1[a:a + 1] * PEER_KEYS + i2[:widths[a]] for a in range(k)]
                               + [jnp.full((pad, t1.shape[1]), -1, I32)], axis=0)
        best, idx = _take_top(cand, cidx, k)
        e = jnp.exp(best - best[0:1])
        gate = e / jnp.sum(e, axis=0, keepdims=True)
        idx_rows.append(idx)
        gate_rows.append(gate)
    idx_ref[...] = jnp.concatenate(idx_rows, axis=0).T
    gate_ref[...] = jnp.concatenate(gate_rows, axis=0).T


def _route(h2, g, wq, keys_cat, tm):
    t, d = h2.shape
    nk = PEER_HEADS * PEER_TOPK
    return pl.pallas_call(
        _route_kernel,
        grid=(t // tm,),
        in_specs=[
            pl.BlockSpec((tm, d), lambda i: (i, 0)),
            pl.BlockSpec((1, d), lambda i: (0, 0)),
            pl.BlockSpec((d, PEER_HEADS * 128), lambda i: (0, 0)),
            pl.BlockSpec((PEER_HEADS, 2 * PEER_KEYS, 128), lambda i: (0, 0, 0)),
        ],
        out_specs=[
            pl.BlockSpec((tm, d // 2), lambda i: (i, 0)),
            pl.BlockSpec((tm, nk), lambda i: (i, 0)),
            pl.BlockSpec((tm, nk), lambda i: (i, 0)),
        ],
        out_shape=[
            jax.ShapeDtypeStruct((t, d // 2), I32),
            jax.ShapeDtypeStruct((t, nk), I32),
            jax.ShapeDtypeStruct((t, nk), F32),
        ],
        compiler_params=pltpu.CompilerParams(
            dimension_semantics=("parallel",), vmem_limit_bytes=VMEM_LIMIT),
        cost_estimate=pl.CostEstimate(flops=2 * t * d * PEER_HEADS * 128 + 2 * t * PEER_HEADS * 256 * 128,
                                      transcendentals=t * nk, bytes_accessed=t * (8 * d + 8 * nk) + 2 * d * PEER_HEADS * 128),
        name="peer_route",
    )(h2, g, wq, keys_cat)


def _final_norm_kernel(x_ref, g_ref, o_ref):
    o_ref[...] = _rms(x_ref[...], g_ref[...])


def _final_norm(x2, g, tm):
    t, d = x2.shape
    return pl.pallas_call(
        _final_norm_kernel,
        grid=(t // tm,),
        in_specs=[pl.BlockSpec((tm, d), lambda i: (i, 0)), pl.BlockSpec((1, d), lambda i: (0, 0))],
        out_specs=pl.BlockSpec((tm, d), lambda i: (i, 0)),
        out_shape=jax.ShapeDtypeStruct((t, d), F32),
        compiler_params=pltpu.CompilerParams(dimension_semantics=("parallel",)),
        cost_estimate=pl.CostEstimate(flops=4 * t * d, transcendentals=t, bytes_accessed=8 * t * d),
        name="final_norm",
    )(x2, g)


SC_WORKERS = 32
SC_CORES = 2
SC_LANES = 16
SC_TOK = 16
SC_ROWS = 16
SC_BUFS = 4
SC_SLAB = 8
HI16 = -65536


def _sc_params():
    cp = pltpu.CompilerParams()
    if "needs_layout_passes" in pltpu.CompilerParams.__dataclass_fields__:
        cp = pltpu.CompilerParams(needs_layout_passes=False)
    return cp


def _sc_worker_id():
    return lax.axis_index("s") * SC_CORES + lax.axis_index("c")


def _tree_sum(xs):
    xs = list(xs)
    while len(xs) > 1:
        xs = [xs[i] + xs[i + 1] for i in range(0, len(xs) - 1, 2)] + ([xs[-1]] if len(xs) % 2 else [])
    return xs[0]


def _gelu_via_exp(x):
    c = math.sqrt(2.0 / math.pi)
    z = c * (x + 0.044715 * (x * x * x))
    return 0.5 * x * (2.0 - 2.0 / (1.0 + jnp.exp(2.0 * z)))


def _round_to_bf16_bits(bits):
    return bits + 0x7FFF + (lax.shift_right_logical(bits, jnp.int32(16)) & 1)


def _pack_bf16_pairs(x):
    w = x.shape[-1] // 2
    r = _round_to_bf16_bits(lax.bitcast_convert_type(x, I32))
    return lax.shift_right_logical(r[..., :w], jnp.int32(16)) | (r[..., w:] & HI16)


def _lo_f32(word):
    return lax.bitcast_convert_type(lax.shift_left(word, jnp.int32(16)), F32)


def _hi_f32(word):
    return lax.bitcast_convert_type(word & HI16, F32)


def _peer_experts_body(hn_hbm, h_hbm, idx_hbm, gate_hbm, u_hbm, v_hbm, out_hbm,
                       h_v, o_v, idx_v, gate_v, coef_v, cw_v, acc_v, *bufs, tok_per_worker):
    rows, sems = bufs[:SC_BUFS], bufs[SC_BUFS:]
    half = hn_hbm.shape[1]
    nk = PEER_HEADS * PEER_TOPK
    per_tok = nk // SC_ROWS
    n = SC_TOK * per_tok
    n_slab = half // (SC_SLAB * SC_LANES)
    lane = lax.iota(I32, SC_LANES)
    wid = _sc_worker_id()

    def packed_mul(word, other_bf16):
        return plsc.bitcast(plsc.bitcast(word, BF16) * other_bf16, I32)

    def gather_u(k, slot):
        return pltpu.make_async_copy(u_hbm.at[idx_v.at[k]], rows[slot], sems[slot])

    def gather_v(k, slot):
        return pltpu.make_async_copy(v_hbm.at[idx_v.at[k]], rows[slot], sems[slot])

    def start_ahead(ahead, slot):
        @pl.when(ahead < n)
        def _():
            gather_u(ahead, slot).start()

        @pl.when(jnp.logical_and(ahead >= n, ahead < 2 * n))
        def _():
            gather_v(ahead - n, slot).start()

    def score(rw, k):
        t = k // per_tok
        for sl in range(n_slab):
            base = sl * SC_SLAB * SC_LANES
            hw = [plsc.bitcast(h_v[t, pl.ds(base + SC_LANES * j, SC_LANES)], BF16) for j in range(SC_SLAB)]

            @plsc.parallel_loop(0, SC_ROWS, 1)
            def _(r):
                terms = []
                for j in range(SC_SLAB):
                    p = packed_mul(rw[r, pl.ds(base + SC_LANES * j, SC_LANES)], hw[j])
                    terms += [_lo_f32(p), _hi_f32(p)]
                p = _tree_sum(terms)
                off = pl.multiple_of(r * SC_LANES, SC_LANES)
                if sl == 0:
                    acc_v[pl.ds(off, SC_LANES)] = p
                else:
                    acc_v[pl.ds(off, SC_LANES)] = acc_v[pl.ds(off, SC_LANES)] + p

        for rg in range(SC_ROWS // SC_LANES):
            base = rg * SC_LANES * SC_LANES
            tot = _tree_sum([plsc.load_gather(acc_v, [lane * SC_LANES + (base + j)]) for j in range(SC_LANES)])
            coef_v[pl.ds(pl.multiple_of(k * SC_ROWS + rg * SC_LANES, SC_LANES), SC_LANES)] = tot

    def combine(rw, k):
        t = k // per_tok
        for sl in range(n_slab):
            base = sl * SC_SLAB * SC_LANES
            acc0 = (tuple(o_v[t, pl.ds(base + SC_LANES * j, SC_LANES)] for j in range(SC_SLAB))
                    + tuple(o_v[t, pl.ds(half + base + SC_LANES * j, SC_LANES)] for j in range(SC_SLAB)))

            def rbody(r, acc):
                w = plsc.bitcast(plsc.load_gather(cw_v, [jnp.full((SC_LANES,), k * SC_ROWS + r, I32)]), BF16)
                new = list(acc)
                for j in range(SC_SLAB):
                    p = packed_mul(rw[r, pl.ds(base + SC_LANES * j, SC_LANES)], w)
                    new[j] = acc[j] + _lo_f32(p)
                    new[SC_SLAB + j] = acc[SC_SLAB + j] + _hi_f32(p)
                return tuple(new)

            acc = lax.fori_loop(0, SC_ROWS, rbody, acc0)
            for j in range(SC_SLAB):
                o_v[t, pl.ds(base + SC_LANES * j, SC_LANES)] = acc[j]
                o_v[t, pl.ds(half + base + SC_LANES * j, SC_LANES)] = acc[SC_SLAB + j]

    def batch(bi, carry):
        t0 = pl.multiple_of(wid * tok_per_worker + bi * SC_TOK, SC_TOK)
        pltpu.sync_copy(idx_hbm.at[pl.ds(t0 * per_tok, n)], idx_v)
        for s in range(SC_BUFS - 1):
            gather_u(s, s).start()
        pltpu.sync_copy(hn_hbm.at[pl.ds(t0, SC_TOK)], h_v)
        pltpu.sync_copy(h_hbm.at[pl.ds(t0, SC_TOK)], o_v)
        pltpu.sync_copy(gate_hbm.at[pl.ds(t0 * nk, SC_TOK * nk)], gate_v)

        def score_group(i, c):
            for s in range(SC_BUFS):
                k = i * SC_BUFS + s
                start_ahead(k + SC_BUFS - 1, (s + SC_BUFS - 1) % SC_BUFS)
                gather_u(k, s).wait()
                score(rows[s], k)
            return c

        lax.fori_loop(0, n // SC_BUFS, score_group, 0)

        @plsc.parallel_loop(0, SC_TOK * nk // SC_LANES, 1)
        def _(j):
            off = pl.multiple_of(j * SC_LANES, SC_LANES)
            c = _gelu_via_exp(coef_v[pl.ds(off, SC_LANES)]) * gate_v[pl.ds(off, SC_LANES)]
            hi = _round_to_bf16_bits(lax.bitcast_convert_type(c, I32)) & HI16
            cw_v[pl.ds(off, SC_LANES)] = hi | lax.shift_right_logical(hi, jnp.int32(16))

        def combine_group(i, c):
            for s in range(SC_BUFS):
                k = i * SC_BUFS + s
                start_ahead(n + k + SC_BUFS - 1, (s + SC_BUFS - 1) % SC_BUFS)
                gather_v(k, s).wait()
                combine(rows[s], k)
            return c

        lax.fori_loop(0, n // SC_BUFS, combine_group, 0)
        pltpu.sync_copy(o_v, out_hbm.at[pl.ds(t0, SC_TOK)])
        return carry

    lax.fori_loop(0, tok_per_worker // SC_TOK, batch, 0)


def _peer_experts(hn_words, h2, idx4, gate_flat, u_words, v_words):
    t, d = h2.shape
    nk = PEER_HEADS * PEER_TOPK
    mesh = plsc.VectorSubcoreMesh(core_axis_name="c", subcore_axis_name="s")
    body = functools.partial(_peer_experts_body, tok_per_worker=t // SC_WORKERS)
    return pl.kernel(
        body,
        out_type=jax.ShapeDtypeStruct((t, d), F32),
        mesh=mesh,
        scratch_types=[
            pltpu.VMEM((SC_TOK, d // 2), I32),
            pltpu.VMEM((SC_TOK, d), F32),
            pltpu.VMEM((SC_TOK * nk // SC_ROWS, SC_ROWS), I32),
            pltpu.VMEM((SC_TOK * nk,), F32),
            pltpu.VMEM((SC_TOK * nk,), F32),
            pltpu.VMEM((SC_TOK * nk,), I32),
            pltpu.VMEM((SC_ROWS * SC_LANES,), F32),
        ] + [pltpu.VMEM((SC_ROWS, d // 2), I32)] * SC_BUFS + [pltpu.SemaphoreType.DMA] * SC_BUFS,
        compiler_params=_sc_params(),
        cost_estimate=pl.CostEstimate(flops=4 * t * nk * d, transcendentals=t * nk,
                                      bytes_accessed=4 * t * nk * d + 10 * t * d),
        name="peer_experts_sc",
    )(hn_words, h2, idx4, gate_flat, u_words, v_words)


def _mixers(h, l, norm1_g, w_in, ssm, fox_b_f, g_ssm_out, g_attn_out, w_o, ssm_w_glu, ssm_b_glu, tm, blk):
    b, s, d = h.shape
    w = N_HEADS * HEAD_DIM
    wl = w_in[l]
    wu_t = wl[:, :w].T.astype(BF16)
    wm = wl[:, w:4 * w].astype(BF16)
    wf = jnp.pad(wl[:, 4 * w:], ((0, 0), (0, 128 - N_HEADS)))
    bf = jnp.pad(fox_b_f[l], (0, 128 - N_HEADS)).reshape(1, 128)
    u_t, qa, ka, v = _in_proj(h, norm1_g[l].reshape(1, d), wu_t, wm, wf, bf, tm)

    lc = SSM_CHUNK
    nc = s // lc
    g = w // SSM_GROUP_CH
    y4 = _ssm(u_t, *ssm, n_chunks=nc, batch=b)
    y_t = y4.reshape(w, b * s)

    y_att = _attention(qa.reshape(b * N_HEADS, s, AUG), ka.reshape(b * N_HEADS, s, AUG),
                       v.reshape(b * N_HEADS, s, HEAD_DIM), blk).reshape(b, N_HEADS, s, HEAD_DIM)

    wo = w_o[l].astype(BF16)
    return _out_proj(y_t, y_att, h, ssm_w_glu[l].T.astype(BF16), ssm_b_glu[l].reshape(w, 1),
                     g_ssm_out[l].reshape(w, 1), g_attn_out[l].reshape(N_HEADS, 1, HEAD_DIM),
                     wo[:w], wo[w:].reshape(N_HEADS, HEAD_DIM, d), tm)


def _keys_cat(keys_l):
    z = jnp.zeros_like(keys_l[:, 0])
    top = jnp.concatenate([keys_l[:, 0], z], axis=-1)
    bot = jnp.concatenate([z, keys_l[:, 1]], axis=-1)
    return jnp.concatenate([top, bot], axis=1).astype(BF16)


def kernel(x, norm1_g, w_in, ssm_lambda_re, ssm_lambda_im, ssm_log_dt, ssm_b_re, ssm_b_im, ssm_c_re, ssm_c_im, ssm_d, ssm_w_glu, ssm_b_glu, fox_b_f, g_ssm_out, g_attn_out, w_o, norm2_g, peer_w_q, peer_keys, peer_u, peer_v, norm_f):
    b, s, d = x.shape
    depth = w_in.shape[0]
    nk = PEER_HEADS * PEER_TOPK
    tm = min(512, s)
    blk = min(512, s)
    assert b % (2 * N_PARTS) == 0
    bh = b // N_PARTS
    ssm_tabs = [_ssm_tables(ssm_lambda_re[l], ssm_lambda_im[l], ssm_log_dt[l], ssm_b_re[l], ssm_b_im[l],
                            ssm_c_re[l], ssm_c_im[l], ssm_d[l]) for l in range(depth)]

    def dense_stage(h, l, early=None):
        t = h.shape[0] * s
        h = _mixers(h, l, norm1_g, w_in, ssm_tabs[l], fox_b_f, g_ssm_out, g_attn_out, w_o, ssm_w_glu, ssm_b_glu,
                    tm, blk)
        if early is not None:
            h, _ = lax.optimization_barrier((h, early))
        h2 = h.reshape(t, d)
        hn, idx, gate = _route(h2, norm2_g[l].reshape(1, d), peer_w_q[l].astype(BF16), _keys_cat(peer_keys[l]),
                               min(256, t))
        return h2, hn, idx.reshape(t * nk // SC_ROWS, SC_ROWS), gate.reshape(t * nk)

    tabs = [(_pack_bf16_pairs(peer_u[l]), _pack_bf16_pairs(peer_v[l])) for l in range(depth)]

    def expert_stage(st, l):
        h2, hn, idx4, gate = st
        return _peer_experts(hn, h2, idx4, gate, tabs[l][0], tabs[l][1]).reshape(-1, s, d)

    def after(value, st):
        value, _ = lax.optimization_barrier((value, st[3]))
        return value

    hs = [x[p * bh:(p + 1) * bh] for p in range(N_PARTS)]
    prev = None
    chain = []
    for l in range(depth):
        for p in range(N_PARTS):
            pieces = [hs[p][:bh // 2], hs[p][bh // 2:]] if prev is None else [hs[p]]
            done = []
            for piece in pieces:
                early = tabs[0] if prev is None else (tabs[l + 1] if p == N_PARTS - 1 and l + 1 < depth else None)
                if prev is not None:
                    piece = after(piece, prev)
                if len(chain) >= 2 and l == 0:
                    piece, _ = lax.optimization_barrier((piece, chain[-2]))
                st = dense_stage(piece, l, early)
                done.append(expert_stage(st, l))
                chain.append(done[-1])
                prev = st
            hs[p] = done[0] if len(done) == 1 else jnp.concatenate(done, axis=0)
    t = bh * s
    outs = [_final_norm(h.reshape(t, d), norm_f.reshape(1, d), min(512, t)).reshape(bh, s, d) for h in hs]
    return jnp.concatenate(outs, axis=0)
```

```python
import functools
import math

import jax
import jax.numpy as jnp
from jax import lax
from jax.experimental import pallas as pl
from jax.experimental.pallas import tpu as pltpu
from jax.experimental.pallas import tpu_sc as plsc

F32 = jnp.float32
BF16 = jnp.bfloat16
I32 = jnp.int32

RMS_EPS = 1e-6
SSM_GROUP_CH = 16
SSM_STATE = 64
SSM_CHUNK = 128
HEAD_DIM = 64
N_HEADS = 8
AUG = 128
PEER_HEADS = 8
PEER_KEYS = 128
PEER_TOPK = 16
VMEM_LIMIT = 56 * 1024 * 1024
N_PARTS = 2


def _rms(x, g):
    return x * lax.rsqrt(jnp.mean(x * x, axis=-1, keepdims=True) + RMS_EPS) * g


def _gelu(x):
    c = math.sqrt(2.0 / math.pi)
    return 0.5 * x * (1.0 + jnp.tanh(c * (x + 0.044715 * (x * x * x))))


def _sigmoid(x):
    return 1.0 / (1.0 + jnp.exp(-x))


def _in_proj_kernel(h_ref, g_ref, wu_ref, wm_ref, wf_ref, bf_ref, u_ref, qa_ref, ka_ref, v_ref, cum_ref):
    j = pl.program_id(1)

    @pl.when(j == 0)
    def _():
        cum_ref[...] = jnp.zeros_like(cum_ref)

    x = h_ref[0]
    tm = x.shape[0]
    xn = _rms(x, g_ref[...])
    xb = xn.astype(BF16)
    proj = jnp.dot(xb, wm_ref[...], preferred_element_type=F32)
    u_ref[...] = lax.dot_general(wu_ref[...], xb, (((1,), (1,)), ((), ())),
                                 preferred_element_type=F32).astype(BF16)
    f = jnp.dot(xn, wf_ref[...], precision=lax.Precision.HIGHEST, preferred_element_type=F32) + bf_ref[...]
    logf = jnp.minimum(f, 0.0) - jnp.log(1.0 + jnp.exp(-jnp.abs(f)))
    row = lax.broadcasted_iota(I32, (tm, tm), 0)
    col = lax.broadcasted_iota(I32, (tm, tm), 1)
    tri = (row >= col).astype(F32)
    cum = jnp.dot(tri, logf, precision=lax.Precision.HIGHEST, preferred_element_type=F32) + cum_ref[0:1, :]
    cum_ref[0:1, :] = cum[tm - 1:tm, :]

    w = HEAD_DIM * N_HEADS
    lane = lax.broadcasted_iota(I32, (tm, AUG), 1)
    scale = HEAD_DIM ** -0.5
    for hh in range(N_HEADS):
        pair = hh // 2
        q2 = proj[:, 128 * pair: 128 * pair + 128]
        k2 = proj[:, w + 128 * pair: w + 128 * pair + 128]
        v2 = proj[:, 2 * w + 128 * pair: 2 * w + 128 * pair + 128]
        if hh % 2 == 1:
            q2 = pltpu.roll(q2, 64, axis=1)
            k2 = pltpu.roll(k2, 64, axis=1)
            vh = v2[:, 64:]
        else:
            vh = v2[:, :64]
        c = jnp.broadcast_to(cum[:, hh:hh + 1], (tm, AUG))
        c1 = c.astype(BF16).astype(F32)
        r1 = c - c1
        c2 = r1.astype(BF16).astype(F32)
        c3 = r1 - c2
        one = jnp.ones((tm, AUG), F32)
        zero = jnp.zeros((tm, AUG), F32)
        qa = jnp.where(lane < 64, q2 * scale,
             jnp.where(lane == 64, c1, jnp.where(lane == 65, c2, jnp.where(lane == 66, c3,
             jnp.where(lane < 70, one, zero)))))
        ka = jnp.where(lane < 64, k2,
             jnp.where(lane < 67, one, jnp.where(lane == 67, -c1, jnp.where(lane == 68, -c2,
             jnp.where(lane == 69, -c3, zero)))))
        qa_ref[0, hh] = qa.astype(BF16)
        ka_ref[0, hh] = ka.astype(BF16)
        v_ref[0, hh] = vh.astype(BF16)


def _in_proj(h, g, wu_t, wm, wf, bf, tm):
    b, s, d = h.shape
    w = HEAD_DIM * N_HEADS
    ns = s // tm
    return pl.pallas_call(
        _in_proj_kernel,
        grid=(b, ns),
        in_specs=[
            pl.BlockSpec((1, tm, d), lambda i, j: (i, j, 0)),
            pl.BlockSpec((1, d), lambda i, j: (0, 0)),
            pl.BlockSpec((w, d), lambda i, j: (0, 0)),
            pl.BlockSpec((d, 3 * w), lambda i, j: (0, 0)),
            pl.BlockSpec((d, 128), lambda i, j: (0, 0)),
            pl.BlockSpec((1, 128), lambda i, j: (0, 0)),
        ],
        out_specs=[
            pl.BlockSpec((w, tm), lambda i, j: (0, i * ns + j)),
            pl.BlockSpec((1, N_HEADS, tm, AUG), lambda i, j: (i, 0, j, 0)),
            pl.BlockSpec((1, N_HEADS, tm, AUG), lambda i, j: (i, 0, j, 0)),
            pl.BlockSpec((1, N_HEADS, tm, HEAD_DIM), lambda i, j: (i, 0, j, 0)),
        ],
        out_shape=[
            jax.ShapeDtypeStruct((w, b * s), BF16),
            jax.ShapeDtypeStruct((b, N_HEADS, s, AUG), BF16),
            jax.ShapeDtypeStruct((b, N_HEADS, s, AUG), BF16),
            jax.ShapeDtypeStruct((b, N_HEADS, s, HEAD_DIM), BF16),
        ],
        scratch_shapes=[pltpu.VMEM((8, 128), F32)],
        compiler_params=pltpu.CompilerParams(
            dimension_semantics=("parallel", "arbitrary"), vmem_limit_bytes=VMEM_LIMIT),
        cost_estimate=pl.CostEstimate(flops=2 * b * s * d * (4 * w + 128), transcendentals=2 * b * s * 128,
                                      bytes_accessed=4 * b * s * d + 2 * d * (4 * w) + 2 * b * s * (w + 2 * N_HEADS * AUG + w)),
        name="in_proj",
    )(h, g, wu_t, wm, wf, bf)


def _toeplitz_kernel(k_ref, m_ref):
    lc = SSM_CHUNK
    hc = SSM_GROUP_CH
    row = lax.broadcasted_iota(I32, (lc, lc), 0)
    col = lax.broadcasted_iota(I32, (lc, lc), 1)
    causal = col >= row

    def body(hi, carry):
        r0 = pl.multiple_of(hi * lc, lc)
        for ho in range(hc):
            k = jnp.broadcast_to(k_ref[0, pl.ds(hi * hc + ho, 1), :], (lc, lc))
            t = pltpu.roll(k, 0, 1, stride=1, stride_axis=0)
            m_ref[0, pl.ds(r0, lc), ho * lc:(ho + 1) * lc] = jnp.where(causal, t, 0.0).astype(BF16)
        return carry

    lax.fori_loop(0, hc, body, 0)


def _toeplitz(kq):
    g, hh, lc = kq.shape
    width = SSM_GROUP_CH * lc
    return pl.pallas_call(
        _toeplitz_kernel,
        grid=(g,),
        in_specs=[pl.BlockSpec((1, hh, lc), lambda i: (i, 0, 0))],
        out_specs=pl.BlockSpec((1, width, width), lambda i: (i, 0, 0)),
        out_shape=jax.ShapeDtypeStruct((g, width, width), BF16),
        compiler_params=pltpu.CompilerParams(dimension_semantics=("parallel",), vmem_limit_bytes=VMEM_LIMIT),
        cost_estimate=pl.CostEstimate(flops=g * width * width, transcendentals=0,
                                      bytes_accessed=2 * g * width * width + 4 * g * hh * lc),
        name="ssm_toeplitz",
    )(kq)


def _ssm_kernel(u_ref, m_ref, w_ref, r_ref, a_ref, d_ref, y_ref, e_scr, *, n_chunks, batch):
    hc = SSM_GROUP_CH
    lc = SSM_CHUNK
    rows_n = u_ref.shape[1] // lc
    u = jnp.concatenate([u_ref[r:r + 1, :].reshape(rows_n, lc) for r in range(hc)], axis=1)
    y = jnp.dot(u, m_ref[0], preferred_element_type=F32)
    e_scr[...] = jnp.dot(u, w_ref[0], preferred_element_type=F32)
    ar = a_ref[0, 0:1, :]
    ai = a_ref[0, 1:2, :]

    def body(c, s):
        rows = pl.ds(c, batch, stride=n_chunks)
        e_c = e_scr[rows, :]
        e_scr[rows, :] = s
        return ar * s + ai * pltpu.roll(s, SSM_STATE, axis=1) + e_c

    lax.fori_loop(0, n_chunks, body, jnp.zeros((batch, 2 * SSM_STATE), F32))
    y = y + jnp.dot(e_scr[...].astype(BF16), r_ref[0], preferred_element_type=F32)
    y = y + u.astype(F32) * d_ref[0]
    for r in range(hc):
        y_ref[0, r] = y[:, r * lc:(r + 1) * lc]


def _ssm(u_t, m, w, r, a, d, n_chunks, batch):
    hc, lc = SSM_GROUP_CH, SSM_CHUNK
    g = u_t.shape[0] // hc
    rows = u_t.shape[1] // lc
    width = hc * lc
    kern = functools.partial(_ssm_kernel, n_chunks=n_chunks, batch=batch)
    return pl.pallas_call(
        kern,
        grid=(g,),
        in_specs=[
            pl.BlockSpec((hc, rows * lc), lambda i: (i, 0)),
            pl.BlockSpec((1, width, width), lambda i: (i, 0, 0)),
            pl.BlockSpec((1, width, 2 * SSM_STATE), lambda i: (i, 0, 0)),
            pl.BlockSpec((1, 2 * SSM_STATE, width), lambda i: (i, 0, 0)),
            pl.BlockSpec((1, 8, 2 * SSM_STATE), lambda i: (i, 0, 0)),
            pl.BlockSpec((1, 1, width), lambda i: (i, 0, 0)),
        ],
        out_specs=pl.BlockSpec((1, hc, rows, lc), lambda i: (i, 0, 0, 0)),
        out_shape=jax.ShapeDtypeStruct((g, hc, rows, lc), F32),
        scratch_shapes=[pltpu.VMEM((rows, 2 * SSM_STATE), F32)],
        compiler_params=pltpu.CompilerParams(
            dimension_semantics=("parallel",), vmem_limit_bytes=VMEM_LIMIT),
        cost_estimate=pl.CostEstimate(flops=2 * g * rows * width * (width + 4 * SSM_STATE), transcendentals=0,
                                      bytes_accessed=g * (2 * width * (width + 4 * SSM_STATE) + 6 * rows * width)),
        name="ssm",
    )(u_t, m, w, r, a, d)


def _ssm_tables(lam_re, lam_im, log_dt, b_re, b_im, c_re, c_im, d_skip):
    hp = lax.Precision.HIGHEST
    lc = SSM_CHUNK
    g, p = lam_re.shape
    hc = SSM_GROUP_CH
    dt = jnp.exp(log_dt)[:, None]
    a_re = jnp.exp(lam_re * dt) * jnp.cos(lam_im * dt)
    a_im = jnp.exp(lam_re * dt) * jnp.sin(lam_im * dt)
    den = lam_re * lam_re + lam_im * lam_im
    nr = a_re - 1.0
    z_re = (nr * lam_re + a_im * lam_im) / den
    z_im = (a_im * lam_re - nr * lam_im) / den
    bb_re = z_re[..., None] * b_re - z_im[..., None] * b_im
    bb_im = z_re[..., None] * b_im + z_im[..., None] * b_re
    tau = jnp.arange(lc + 1, dtype=F32)[:, None, None]
    mag = jnp.exp(tau * (lam_re * dt)[None])
    ang = tau * (lam_im * dt)[None]
    p_re = mag * jnp.cos(ang)
    p_im = mag * jnp.sin(ang)
    ab_re = p_re[:lc, :, :, None] * bb_re[None] - p_im[:lc, :, :, None] * bb_im[None]
    ab_im = p_re[:lc, :, :, None] * bb_im[None] + p_im[:lc, :, :, None] * bb_re[None]
    kq = (jnp.einsum('ghp,tgpk->gkht', c_re, ab_re, precision=hp)
          - jnp.einsum('ghp,tgpk->gkht', c_im, ab_im, precision=hp)).reshape(g, hc * hc, lc)
    m = _toeplitz(kq)
    rp_re = p_re[:lc][::-1].transpose(1, 0, 2)[:, None]
    rp_im = p_im[:lc][::-1].transpose(1, 0, 2)[:, None]
    tb_re = bb_re.transpose(0, 2, 1)[:, :, None, :]
    tb_im = bb_im.transpose(0, 2, 1)[:, :, None, :]
    w = jnp.concatenate([(rp_re * tb_re - rp_im * tb_im).reshape(g, hc * lc, p),
                         (rp_re * tb_im + rp_im * tb_re).reshape(g, hc * lc, p)], axis=-1)
    np_re = p_re[1:].transpose(1, 2, 0)[:, :, None, :]
    np_im = p_im[1:].transpose(1, 2, 0)[:, :, None, :]
    tc_re = c_re.transpose(0, 2, 1)[:, :, :, None]
    tc_im = c_im.transpose(0, 2, 1)[:, :, :, None]
    r = jnp.concatenate([(np_re * tc_re - np_im * tc_im).reshape(g, p, hc * lc),
                         -(np_re * tc_im + np_im * tc_re).reshape(g, p, hc * lc)], axis=1)
    al_re, al_im = p_re[lc], p_im[lc]
    a = jnp.zeros((g, 8, 2 * p), F32)
    a = a.at[:, 0, :].set(jnp.concatenate([al_re, al_re], axis=-1))
    a = a.at[:, 1, :].set(jnp.concatenate([-al_im, al_im], axis=-1))
    d = jnp.repeat(d_skip, lc, axis=-1).reshape(g, 1, hc * lc)
    return m, w.astype(BF16), r.astype(BF16), a, d


def _attn_kernel(q_ref, k_ref, v_ref, o_ref, m_ref, l_ref, acc_ref, *, blk):
    i = pl.program_id(1)
    q = q_ref[0]
    m_ref[...] = jnp.full_like(m_ref, -jnp.inf)
    l_ref[...] = jnp.zeros_like(l_ref)
    acc_ref[...] = jnp.zeros_like(acc_ref)

    def step(j, masked):
        off = pl.multiple_of(j * blk, blk)
        k = k_ref[0, pl.ds(off, blk), :]
        v = v_ref[0, pl.ds(off, blk), :]
        s = lax.dot_general(q, k, (((1,), (1,)), ((), ())), preferred_element_type=F32)
        if masked:
            row = lax.broadcasted_iota(I32, s.shape, 0)
            col = lax.broadcasted_iota(I32, s.shape, 1)
            s = jnp.where(row >= col, s, -jnp.inf)
        m_prev = m_ref[...]
        m_new = jnp.maximum(m_prev, jnp.max(s, axis=1, keepdims=True))
        p = jnp.exp(s - m_new)
        alpha = jnp.exp(m_prev - m_new)
        l_ref[...] = alpha * l_ref[...] + jnp.sum(p, axis=1, keepdims=True)
        acc_ref[...] = alpha * acc_ref[...] + jnp.dot(p.astype(BF16), v, preferred_element_type=F32)
        m_ref[...] = m_new

    def body(j, c):
        step(j, False)
        return c

    lax.fori_loop(0, i, body, 0)
    step(i, True)
    o_ref[0] = acc_ref[...] / l_ref[...]


def _attention(qa, ka, v, blk):
    bh, s, _ = qa.shape
    kern = functools.partial(_attn_kernel, blk=blk)
    return pl.pallas_call(
        kern,
        grid=(bh, s // blk),
        in_specs=[
            pl.BlockSpec((1, blk, AUG), lambda b, i: (b, i, 0)),
            pl.BlockSpec((1, s, AUG), lambda b, i: (b, 0, 0)),
            pl.BlockSpec((1, s, HEAD_DIM), lambda b, i: (b, 0, 0)),
        ],
        out_specs=pl.BlockSpec((1, blk, HEAD_DIM), lambda b, i: (b, i, 0)),
        out_shape=jax.ShapeDtypeStruct((bh, s, HEAD_DIM), F32),
        scratch_shapes=[pltpu.VMEM((blk, 1), F32), pltpu.VMEM((blk, 1), F32), pltpu.VMEM((blk, HEAD_DIM), F32)],
        compiler_params=pltpu.CompilerParams(
            dimension_semantics=("parallel", "arbitrary"), vmem_limit_bytes=VMEM_LIMIT),
        cost_estimate=pl.CostEstimate(flops=bh * s * s * (AUG + HEAD_DIM), transcendentals=bh * s * s // 2,
                                      bytes_accessed=bh * s * (2 * 2 * AUG + 2 * HEAD_DIM + 4 * HEAD_DIM)),
        name="fox_attn",
    )(qa, ka, v)


def _out_proj_kernel(y_ref, att_ref, h_ref, wg_ref, bg_ref, gs_ref, ga_ref, wos_ref, woa_ref, o_ref):
    g = _gelu(y_ref[...])
    z = jnp.dot(wg_ref[...], g.astype(BF16), preferred_element_type=F32) + bg_ref[...]
    o = g * _sigmoid(z)
    a = o * lax.rsqrt(jnp.mean(o * o, axis=0, keepdims=True) + RMS_EPS) * gs_ref[...]
    acc = h_ref[0] + lax.dot_general(a.astype(BF16), wos_ref[...], (((0,), (0,)), ((), ())),
                                     preferred_element_type=F32)
    ssq = jnp.zeros((acc.shape[0], 1), F32)
    for hh in range(N_HEADS):
        t = att_ref[0, hh]
        ssq = ssq + jnp.sum(t * t, axis=1, keepdims=True)
    inv = lax.rsqrt(ssq / (N_HEADS * HEAD_DIM) + RMS_EPS)
    for hh in range(N_HEADS):
        bh = att_ref[0, hh] * inv * ga_ref[hh]
        acc = acc + jnp.dot(bh.astype(BF16), woa_ref[hh], preferred_element_type=F32)
    o_ref[0] = acc


def _out_proj(y_t, y_att, h, wg_t, bg, gs, ga, wos, woa, tm):
    b, s, d = h.shape
    w = y_t.shape[0]
    ns = s // tm
    return pl.pallas_call(
        _out_proj_kernel,
        grid=(b, ns),
        in_specs=[
            pl.BlockSpec((w, tm), lambda i, j: (0, i * ns + j)),
            pl.BlockSpec((1, N_HEADS, tm, HEAD_DIM), lambda i, j: (i, 0, j, 0)),
            pl.BlockSpec((1, tm, d), lambda i, j: (i, j, 0)),
            pl.BlockSpec((w, w), lambda i, j: (0, 0)),
            pl.BlockSpec((w, 1), lambda i, j: (0, 0)),
            pl.BlockSpec((w, 1), lambda i, j: (0, 0)),
            pl.BlockSpec((N_HEADS, 1, HEAD_DIM), lambda i, j: (0, 0, 0)),
            pl.BlockSpec((w, d), lambda i, j: (0, 0)),
            pl.BlockSpec((N_HEADS, HEAD_DIM, d), lambda i, j: (0, 0, 0)),
        ],
        out_specs=pl.BlockSpec((1, tm, d), lambda i, j: (i, j, 0)),
        out_shape=jax.ShapeDtypeStruct((b, s, d), F32),
        compiler_params=pltpu.CompilerParams(
            dimension_semantics=("parallel", "parallel"), vmem_limit_bytes=VMEM_LIMIT),
        cost_estimate=pl.CostEstimate(flops=2 * b * s * (w * w + 2 * w * d), transcendentals=2 * b * s * w,
                                      bytes_accessed=b * s * (8 * w + 8 * d) + 2 * (w * w + 2 * w * d)),
        name="out_proj",
    )(y_t, y_att, h, wg_t, bg, gs, ga, wos, woa)


def _take_top(vals, payload, k):
    n_rows = vals.shape[0]
    rows = lax.broadcasted_iota(I32, vals.shape, 0)
    tops, picks = [], []
    for _ in range(k):
        m = jnp.max(vals, axis=0, keepdims=True)
        arg = jnp.min(jnp.where(vals == m, rows, n_rows), axis=0, keepdims=True)
        hit = rows == arg
        tops.append(m)
        picks.append(arg if payload is None else jnp.max(jnp.where(hit, payload, -1), axis=0, keepdims=True))
        vals = jnp.where(hit, -jnp.inf, vals)
    return jnp.concatenate(tops, axis=0), jnp.concatenate(picks, axis=0)


def _route_kernel(h_ref, g_ref, wq_ref, keys_ref, hn_ref, idx_ref, gate_ref):
    x = h_ref[...]
    hn = _rms(x, g_ref[...])
    hn_ref[...] = _pack_bf16_pairs(hn)
    q = jnp.dot(hn.astype(BF16), wq_ref[...], preferred_element_type=F32)
    k = PEER_TOPK
    idx_rows, gate_rows = [], []
    for hh in range(PEER_HEADS):
        qh = q[:, 128 * hh: 128 * (hh + 1)].astype(BF16)
        sc = lax.dot_general(keys_ref[hh], qh, (((1,), (1,)), ((), ())), preferred_element_type=F32)
        t1, i1 = _take_top(sc[:PEER_KEYS], None, k)
        t2, i2 = _take_top(sc[PEER_KEYS:], None, k)
        widths = [k // (a + 1) for a in range(k)]
        pad = -sum(widths) % 8
        cand = jnp.concatenate([t1[a:a + 1] + t2[:widths[a]] for a in range(k)]
                               + [jnp.full((pad, t1.shape[1]), -jnp.inf, F32)], axis=0)
        cidx = jnp.concatenate([i1[a:a + 1] * PEER_KEYS + i2[:widths[a]] for a in range(k)]
                               + [jnp.full((pad, t1.shape[1]), -1, I32)], axis=0)
        best, idx = _take_top(cand, cidx, k)
        e = jnp.exp(best - best[0:1])
        gate = e / jnp.sum(e, axis=0, keepdims=True)
        idx_rows.append(idx)
        gate_rows.append(gate)
    idx_ref[...] = jnp.concatenate(idx_rows, axis=0).T
    gate_ref[...] = jnp.concatenate(gate_rows, axis=0).T


def _route(h2, g, wq, keys_cat, tm):
    t, d = h2.shape
    nk = PEER_HEADS * PEER_TOPK
    return pl.pallas_call(
        _route_kernel,
        grid=(t // tm,),
        in_specs=[
            pl.BlockSpec((tm, d), lambda i: (i, 0)),
            pl.BlockSpec((1, d), lambda i: (0, 0)),
            pl.BlockSpec((d, PEER_HEADS * 128), lambda i: (0, 0)),
            pl.BlockSpec((PEER_HEADS, 2 * PEER_KEYS, 128), lambda i: (0, 0, 0)),
        ],
        out_specs=[
            pl.BlockSpec((tm, d // 2), lambda i: (i, 0)),
            pl.BlockSpec((tm, nk), lambda i: (i, 0)),
            pl.BlockSpec((tm, nk), lambda i: (i, 0)),
        ],
        out_shape=[
            jax.ShapeDtypeStruct((t, d // 2), I32),
            jax.ShapeDtypeStruct((t, nk), I32),
            jax.ShapeDtypeStruct((t, nk), F32),
        ],
        compiler_params=pltpu.CompilerParams(
            dimension_semantics=("parallel",), vmem_limit_bytes=VMEM_LIMIT),
        cost_estimate=pl.CostEstimate(flops=2 * t * d * PEER_HEADS * 128 + 2 * t * PEER_HEADS * 256 * 128,
                                      transcendentals=t * nk, bytes_accessed=t * (8 * d + 8 * nk) + 2 * d * PEER_HEADS * 128),
        name="peer_route",
    )(h2, g, wq, keys_cat)


def _final_norm_kernel(x_ref, g_ref, o_ref):
    o_ref[...] = _rms(x_ref[...], g_ref[...])


def _final_norm(x2, g, tm):
    t, d = x2.shape
    return pl.pallas_call(
        _final_norm_kernel,
        grid=(t // tm,),
        in_specs=[pl.BlockSpec((tm, d), lambda i: (i, 0)), pl.BlockSpec((1, d), lambda i: (0, 0))],
        out_specs=pl.BlockSpec((tm, d), lambda i: (i, 0)),
        out_shape=jax.ShapeDtypeStruct((t, d), F32),
        compiler_params=pltpu.CompilerParams(dimension_semantics=("parallel",)),
        cost_estimate=pl.CostEstimate(flops=4 * t * d, transcendentals=t, bytes_accessed=8 * t * d),
        name="final_norm",
    )(x2, g)


SC_WORKERS = 32
SC_CORES = 2
SC_LANES = 16
SC_TOK = 16
SC_ROWS = 32
SC_BUFS = 4
SC_SLAB = 8
HI16 = -65536


def _sc_params():
    cp = pltpu.CompilerParams()
    if "needs_layout_passes" in pltpu.CompilerParams.__dataclass_fields__:
        cp = pltpu.CompilerParams(needs_layout_passes=False)
    return cp


def _sc_worker_id():
    return lax.axis_index("s") * SC_CORES + lax.axis_index("c")


def _tree_sum(xs):
    xs = list(xs)
    while len(xs) > 1:
        xs = [xs[i] + xs[i + 1] for i in range(0, len(xs) - 1, 2)] + ([xs[-1]] if len(xs) % 2 else [])
    return xs[0]


def _gelu_via_exp(x):
    c = math.sqrt(2.0 / math.pi)
    z = c * (x + 0.044715 * (x * x * x))
    return 0.5 * x * (2.0 - 2.0 / (1.0 + jnp.exp(2.0 * z)))


def _round_to_bf16_bits(bits):
    return bits + 0x7FFF + (lax.shift_right_logical(bits, jnp.int32(16)) & 1)


def _pack_bf16_pairs(x):
    w = x.shape[-1] // 2
    r = _round_to_bf16_bits(lax.bitcast_convert_type(x, I32))
    return lax.shift_right_logical(r[..., :w], jnp.int32(16)) | (r[..., w:] & HI16)


def _lo_f32(word):
    return lax.bitcast_convert_type(lax.shift_left(word, jnp.int32(16)), F32)


def _hi_f32(word):
    return lax.bitcast_convert_type(word & HI16, F32)


def _peer_experts_body(hn_hbm, h_hbm, idx_hbm, gate_hbm, u_hbm, v_hbm, out_hbm,
                       h_v, o_v, idx_v, gate_v, coef_v, cw_v, acc_v, *bufs, tok_per_worker):
    rows, sems = bufs[:SC_BUFS], bufs[SC_BUFS:]
    half = hn_hbm.shape[1]
    nk = PEER_HEADS * PEER_TOPK
    per_tok = nk // SC_ROWS
    n = SC_TOK * per_tok
    n_slab = half // (SC_SLAB * SC_LANES)
    lane = lax.iota(I32, SC_LANES)
    wid = _sc_worker_id()

    def packed_mul(word, other_bf16):
        return plsc.bitcast(plsc.bitcast(word, BF16) * other_bf16, I32)

    def gather_u(k, slot):
        return pltpu.make_async_copy(u_hbm.at[idx_v.at[k]], rows[slot], sems[slot])

    def gather_v(k, slot):
        return pltpu.make_async_copy(v_hbm.at[idx_v.at[k]], rows[slot], sems[slot])

    def start_ahead(ahead, slot):
        @pl.when(ahead < n)
        def _():
            gather_u(ahead, slot).start()

        @pl.when(jnp.logical_and(ahead >= n, ahead < 2 * n))
        def _():
            gather_v(ahead - n, slot).start()

    def score(rw, k):
        t = k // per_tok
        for sl in range(n_slab):
            base = sl * SC_SLAB * SC_LANES
            hw = [plsc.bitcast(h_v[t, pl.ds(base + SC_LANES * j, SC_LANES)], BF16) for j in range(SC_SLAB)]

            @plsc.parallel_loop(0, SC_ROWS, 1)
            def _(r):
                terms = []
                for j in range(SC_SLAB):
                    p = packed_mul(rw[r, pl.ds(base + SC_LANES * j, SC_LANES)], hw[j])
                    terms += [_lo_f32(p), _hi_f32(p)]
                p = _tree_sum(terms)
                off = pl.multiple_of(r * SC_LANES, SC_LANES)
                if sl == 0:
                    acc_v[pl.ds(off, SC_LANES)] = p
                else:
                    acc_v[pl.ds(off, SC_LANES)] = acc_v[pl.ds(off, SC_LANES)] + p

        for rg in range(SC_ROWS // SC_LANES):
            base = rg * SC_LANES * SC_LANES
            tot = _tree_sum([plsc.load_gather(acc_v, [lane * SC_LANES + (base + j)]) for j in range(SC_LANES)])
            coef_v[pl.ds(pl.multiple_of(k * SC_ROWS + rg * SC_LANES, SC_LANES), SC_LANES)] = tot

    def combine(rw, k):
        t = k // per_tok
        for sl in range(n_slab):
            base = sl * SC_SLAB * SC_LANES
            acc0 = (tuple(o_v[t, pl.ds(base + SC_LANES * j, SC_LANES)] for j in range(SC_SLAB))
                    + tuple(o_v[t, pl.ds(half + base + SC_LANES * j, SC_LANES)] for j in range(SC_SLAB)))

            def rbody(r, acc):
                w = plsc.bitcast(plsc.load_gather(cw_v, [jnp.full((SC_LANES,), k * SC_ROWS + r, I32)]), BF16)
                new = list(acc)
                for j in range(SC_SLAB):
                    p = packed_mul(rw[r, pl.ds(base + SC_LANES * j, SC_LANES)], w)
                    new[j] = acc[j] + _lo_f32(p)
                    new[SC_SLAB + j] = acc[SC_SLAB + j] + _hi_f32(p)
                return tuple(new)

            acc = lax.fori_loop(0, SC_ROWS, rbody, acc0)
            for j in range(SC_SLAB):
                o_v[t, pl.ds(base + SC_LANES * j, SC_LANES)] = acc[j]
                o_v[t, pl.ds(half + base + SC_LANES * j, SC_LANES)] = acc[SC_SLAB + j]

    def batch(bi, carry):
        t0 = pl.multiple_of(wid * tok_per_worker + bi * SC_TOK, SC_TOK)
        pltpu.sync_copy(idx_hbm.at[pl.ds(t0 * per_tok, n)], idx_v)
        for s in range(SC_BUFS - 1):
            gather_u(s, s).start()
        pltpu.sync_copy(hn_hbm.at[pl.ds(t0, SC_TOK)], h_v)
        pltpu.sync_copy(h_hbm.at[pl.ds(t0, SC_TOK)], o_v)
        pltpu.sync_copy(gate_hbm.at[pl.ds(t0 * nk, SC_TOK * nk)], gate_v)

        def score_group(i, c):
            for s in range(SC_BUFS):
                k = i * SC_BUFS + s
                start_ahead(k + SC_BUFS - 1, (s + SC_BUFS - 1) % SC_BUFS)
                gather_u(k, s).wait()
                score(rows[s], k)
            return c

        lax.fori_loop(0, n // SC_BUFS, score_group, 0)

        @plsc.parallel_loop(0, SC_TOK * nk // SC_LANES, 1)
        def _(j):
            off = pl.multiple_of(j * SC_LANES, SC_LANES)
            c = _gelu_via_exp(coef_v[pl.ds(off, SC_LANES)]) * gate_v[pl.ds(off, SC_LANES)]
            hi = _round_to_bf16_bits(lax.bitcast_convert_type(c, I32)) & HI16
            cw_v[pl.ds(off, SC_LANES)] = hi | lax.shift_right_logical(hi, jnp.int32(16))

        def combine_group(i, c):
            for s in range(SC_BUFS):
                k = i * SC_BUFS + s
                start_ahead(n + k + SC_BUFS - 1, (s + SC_BUFS - 1) % SC_BUFS)
                gather_v(k, s).wait()
                combine(rows[s], k)
            return c

        lax.fori_loop(0, n // SC_BUFS, combine_group, 0)
        pltpu.sync_copy(o_v, out_hbm.at[pl.ds(t0, SC_TOK)])
        return carry

    lax.fori_loop(0, tok_per_worker // SC_TOK, batch, 0)


def _peer_experts(hn_words, h2, idx4, gate_flat, u_words, v_words):
    t, d = h2.shape
    nk = PEER_HEADS * PEER_TOPK
    mesh = plsc.VectorSubcoreMesh(core_axis_name="c", subcore_axis_name="s")
    body = functools.partial(_peer_experts_body, tok_per_worker=t // SC_WORKERS)
    return pl.kernel(
        body,
        out_type=jax.ShapeDtypeStruct((t, d), F32),
        mesh=mesh,
        scratch_types=[
            pltpu.VMEM((SC_TOK, d // 2), I32),
            pltpu.VMEM((SC_TOK, d), F32),
            pltpu.VMEM((SC_TOK * nk // SC_ROWS, SC_ROWS), I32),
            pltpu.VMEM((SC_TOK * nk,), F32),
            pltpu.VMEM((SC_TOK * nk,), F32),
            pltpu.VMEM((SC_TOK * nk,), I32),
            pltpu.VMEM((SC_ROWS * SC_LANES,), F32),
        ] + [pltpu.VMEM((SC_ROWS, d // 2), I32)] * SC_BUFS + [pltpu.SemaphoreType.DMA] * SC_BUFS,
        compiler_params=_sc_params(),
        cost_estimate=pl.CostEstimate(flops=4 * t * nk * d, transcendentals=t * nk,
                                      bytes_accessed=4 * t * nk * d + 10 * t * d),
        name="peer_experts_sc",
    )(hn_words, h2, idx4, gate_flat, u_words, v_words)


def _mixers(h, l, norm1_g, w_in, ssm, fox_b_f, g_ssm_out, g_attn_out, w_o, ssm_w_glu, ssm_b_glu, tm, blk):
    b, s, d = h.shape
    w = N_HEADS * HEAD_DIM
    wl = w_in[l]
    wu_t = wl[:, :w].T.astype(BF16)
    wm = wl[:, w:4 * w].astype(BF16)
    wf = jnp.pad(wl[:, 4 * w:], ((0, 0), (0, 128 - N_HEADS)))
    bf = jnp.pad(fox_b_f[l], (0, 128 - N_HEADS)).reshape(1, 128)
    u_t, qa, ka, v = _in_proj(h, norm1_g[l].reshape(1, d), wu_t, wm, wf, bf, tm)

    lc = SSM_CHUNK
    nc = s // lc
    g = w // SSM_GROUP_CH
    y4 = _ssm(u_t, *ssm, n_chunks=nc, batch=b)
    y_t = y4.reshape(w, b * s)

    y_att = _attention(qa.reshape(b * N_HEADS, s, AUG), ka.reshape(b * N_HEADS, s, AUG),
                       v.reshape(b * N_HEADS, s, HEAD_DIM), blk).reshape(b, N_HEADS, s, HEAD_DIM)

    wo = w_o[l].astype(BF16)
    return _out_proj(y_t, y_att, h, ssm_w_glu[l].T.astype(BF16), ssm_b_glu[l].reshape(w, 1),
                     g_ssm_out[l].reshape(w, 1), g_attn_out[l].reshape(N_HEADS, 1, HEAD_DIM),
                     wo[:w], wo[w:].reshape(N_HEADS, HEAD_DIM, d), tm)


def _keys_cat(keys_l):
    z = jnp.zeros_like(keys_l[:, 0])
    top = jnp.concatenate([keys_l[:, 0], z], axis=-1)
    bot = jnp.concatenate([z, keys_l[:, 1]], axis=-1)
    return jnp.concatenate([top, bot], axis=1).astype(BF16)


def kernel(x, norm1_g, w_in, ssm_lambda_re, ssm_lambda_im, ssm_log_dt, ssm_b_re, ssm_b_im, ssm_c_re, ssm_c_im, ssm_d, ssm_w_glu, ssm_b_glu, fox_b_f, g_ssm_out, g_attn_out, w_o, norm2_g, peer_w_q, peer_keys, peer_u, peer_v, norm_f):
    b, s, d = x.shape
    depth = w_in.shape[0]
    nk = PEER_HEADS * PEER_TOPK
    tm = min(512, s)
    blk = min(512, s)
    assert b % (2 * N_PARTS) == 0
    bh = b // N_PARTS
    ssm_tabs = [_ssm_tables(ssm_lambda_re[l], ssm_lambda_im[l], ssm_log_dt[l], ssm_b_re[l], ssm_b_im[l],
                            ssm_c_re[l], ssm_c_im[l], ssm_d[l]) for l in range(depth)]

    def dense_stage(h, l, early=None):
        t = h.shape[0] * s
        h = _mixers(h, l, norm1_g, w_in, ssm_tabs[l], fox_b_f, g_ssm_out, g_attn_out, w_o, ssm_w_glu, ssm_b_glu,
                    tm, blk)
        if early is not None:
            h, _ = lax.optimization_barrier((h, early))
        h2 = h.reshape(t, d)
        hn, idx, gate = _route(h2, norm2_g[l].reshape(1, d), peer_w_q[l].astype(BF16), _keys_cat(peer_keys[l]),
                               min(256, t))
        return h2, hn, idx.reshape(t * nk // SC_ROWS, SC_ROWS), gate.reshape(t * nk)

    tabs = [(_pack_bf16_pairs(peer_u[l]), _pack_bf16_pairs(peer_v[l])) for l in range(depth)]

    def expert_stage(st, l):
        h2, hn, idx4, gate = st
        return _peer_experts(hn, h2, idx4, gate, tabs[l][0], tabs[l][1]).reshape(-1, s, d)

    def after(value, st):
        value, _ = lax.optimization_barrier((value, st[3]))
        return value

    hs = [x[p * bh:(p + 1) * bh] for p in range(N_PARTS)]
    prev = None
    chain = []
    for l in range(depth):
        for p in range(N_PARTS):
            pieces = [hs[p][:bh // 2], hs[p][bh // 2:]] if prev is None else [hs[p]]
            done = []
            for piece in pieces:
                early = tabs[0] if prev is None else (tabs[l + 1] if p == N_PARTS - 1 and l + 1 < depth else None)
                if prev is not None:
                    piece = after(piece, prev)
                if len(chain) >= 2 and l == 0:
                    piece, _ = lax.optimization_barrier((piece, chain[-2]))
                st = dense_stage(piece, l, early)
                done.append(expert_stage(st, l))
                chain.append(done[-1])
                prev = st
            hs[p] = done[0] if len(done) == 1 else jnp.concatenate(done, axis=0)
    t = bh * s
    outs = [_final_norm(h.reshape(t, d), norm_f.reshape(1, d), min(512, t)).reshape(bh, s, d) for h in hs]
    return jnp.concatenate(outs, axis=0)
```

```python
import functools
import math

import jax
import jax.numpy as jnp
from jax import lax
from jax.experimental import pallas as pl
from jax.experimental.pallas import tpu as pltpu
from jax.experimental.pallas import tpu_sc as plsc

F32 = jnp.float32
BF16 = jnp.bfloat16
I32 = jnp.int32

RMS_EPS = 1e-6
SSM_GROUP_CH = 16
SSM_STATE = 64
SSM_CHUNK = 128
HEAD_DIM = 64
N_HEADS = 8
AUG = 128
PEER_HEADS = 8
PEER_KEYS = 128
PEER_TOPK = 16
VMEM_LIMIT = 56 * 1024 * 1024
N_PARTS = 2


def _rms(x, g):
    return x * lax.rsqrt(jnp.mean(x * x, axis=-1, keepdims=True) + RMS_EPS) * g


def _gelu(x):
    c = math.sqrt(2.0 / math.pi)
    return 0.5 * x * (1.0 + jnp.tanh(c * (x + 0.044715 * (x * x * x))))


def _sigmoid(x):
    return 1.0 / (1.0 + jnp.exp(-x))


def _in_proj_kernel(h_ref, g_ref, wu_ref, wm_ref, wf_ref, bf_ref, u_ref, qa_ref, ka_ref, v_ref, cum_ref):
    j = pl.program_id(1)

    @pl.when(j == 0)
    def _():
        cum_ref[...] = jnp.zeros_like(cum_ref)

    x = h_ref[0]
    tm = x.shape[0]
    xn = _rms(x, g_ref[...])
    xb = xn.astype(BF16)
    proj = jnp.dot(xb, wm_ref[...], preferred_element_type=F32)
    u_ref[...] = lax.dot_general(wu_ref[...], xb, (((1,), (1,)), ((), ())),
                                 preferred_element_type=F32).astype(BF16)
    f = jnp.dot(xn, wf_ref[...], precision=lax.Precision.HIGHEST, preferred_element_type=F32) + bf_ref[...]
    logf = jnp.minimum(f, 0.0) - jnp.log(1.0 + jnp.exp(-jnp.abs(f)))
    row = lax.broadcasted_iota(I32, (tm, tm), 0)
    col = lax.broadcasted_iota(I32, (tm, tm), 1)
    tri = (row >= col).astype(F32)
    cum = jnp.dot(tri, logf, precision=lax.Precision.HIGHEST, preferred_element_type=F32) + cum_ref[0:1, :]
    cum_ref[0:1, :] = cum[tm - 1:tm, :]

    w = HEAD_DIM * N_HEADS
    lane = lax.broadcasted_iota(I32, (tm, AUG), 1)
    scale = HEAD_DIM ** -0.5
    for hh in range(N_HEADS):
        pair = hh // 2
        q2 = proj[:, 128 * pair: 128 * pair + 128]
        k2 = proj[:, w + 128 * pair: w + 128 * pair + 128]
        v2 = proj[:, 2 * w + 128 * pair: 2 * w + 128 * pair + 128]
        if hh % 2 == 1:
            q2 = pltpu.roll(q2, 64, axis=1)
            k2 = pltpu.roll(k2, 64, axis=1)
            vh = v2[:, 64:]
        else:
            vh = v2[:, :64]
        c = jnp.broadcast_to(cum[:, hh:hh + 1], (tm, AUG))
        c1 = c.astype(BF16).astype(F32)
        r1 = c - c1
        c2 = r1.astype(BF16).astype(F32)
        c3 = r1 - c2
        one = jnp.ones((tm, AUG), F32)
        zero = jnp.zeros((tm, AUG), F32)
        qa = jnp.where(lane < 64, q2 * scale,
             jnp.where(lane == 64, c1, jnp.where(lane == 65, c2, jnp.where(lane == 66, c3,
             jnp.where(lane < 70, one, zero)))))
        ka = jnp.where(lane < 64, k2,
             jnp.where(lane < 67, one, jnp.where(lane == 67, -c1, jnp.where(lane == 68, -c2,
             jnp.where(lane == 69, -c3, zero)))))
        qa_ref[0, hh] = qa.astype(BF16)
        ka_ref[0, hh] = ka.astype(BF16)
        v_ref[0, hh] = vh.astype(BF16)


def _in_proj(h, g, wu_t, wm, wf, bf, tm):
    b, s, d = h.shape
    w = HEAD_DIM * N_HEADS
    ns = s // tm
    return pl.pallas_call(
        _in_proj_kernel,
        grid=(b, ns),
        in_specs=[
            pl.BlockSpec((1, tm, d), lambda i, j: (i, j, 0)),
            pl.BlockSpec((1, d), lambda i, j: (0, 0)),
            pl.BlockSpec((w, d), lambda i, j: (0, 0)),
            pl.BlockSpec((d, 3 * w), lambda i, j: (0, 0)),
            pl.BlockSpec((d, 128), lambda i, j: (0, 0)),
            pl.BlockSpec((1, 128), lambda i, j: (0, 0)),
        ],
        out_specs=[
            pl.BlockSpec((w, tm), lambda i, j: (0, i * ns + j)),
            pl.BlockSpec((1, N_HEADS, tm, AUG), lambda i, j: (i, 0, j, 0)),
            pl.BlockSpec((1, N_HEADS, tm, AUG), lambda i, j: (i, 0, j, 0)),
            pl.BlockSpec((1, N_HEADS, tm, HEAD_DIM), lambda i, j: (i, 0, j, 0)),
        ],
        out_shape=[
            jax.ShapeDtypeStruct((w, b * s), BF16),
            jax.ShapeDtypeStruct((b, N_HEADS, s, AUG), BF16),
            jax.ShapeDtypeStruct((b, N_HEADS, s, AUG), BF16),
            jax.ShapeDtypeStruct((b, N_HEADS, s, HEAD_DIM), BF16),
        ],
        scratch_shapes=[pltpu.VMEM((8, 128), F32)],
        compiler_params=pltpu.CompilerParams(
            dimension_semantics=("parallel", "arbitrary"), vmem_limit_bytes=VMEM_LIMIT),
        cost_estimate=pl.CostEstimate(flops=2 * b * s * d * (4 * w + 128), transcendentals=2 * b * s * 128,
                                      bytes_accessed=4 * b * s * d + 2 * d * (4 * w) + 2 * b * s * (w + 2 * N_HEADS * AUG + w)),
        name="in_proj",
    )(h, g, wu_t, wm, wf, bf)


def _toeplitz_kernel(k_ref, m_ref):
    lc = SSM_CHUNK
    hc = SSM_GROUP_CH
    row = lax.broadcasted_iota(I32, (lc, lc), 0)
    col = lax.broadcasted_iota(I32, (lc, lc), 1)
    causal = col >= row

    def body(hi, carry):
        r0 = pl.multiple_of(hi * lc, lc)
        for ho in range(hc):
            k = jnp.broadcast_to(k_ref[0, pl.ds(hi * hc + ho, 1), :], (lc, lc))
            t = pltpu.roll(k, 0, 1, stride=1, stride_axis=0)
            m_ref[0, pl.ds(r0, lc), ho * lc:(ho + 1) * lc] = jnp.where(causal, t, 0.0).astype(BF16)
        return carry

    lax.fori_loop(0, hc, body, 0)


def _toeplitz(kq):
    g, hh, lc = kq.shape
    width = SSM_GROUP_CH * lc
    return pl.pallas_call(
        _toeplitz_kernel,
        grid=(g,),
        in_specs=[pl.BlockSpec((1, hh, lc), lambda i: (i, 0, 0))],
        out_specs=pl.BlockSpec((1, width, width), lambda i: (i, 0, 0)),
        out_shape=jax.ShapeDtypeStruct((g, width, width), BF16),
        compiler_params=pltpu.CompilerParams(dimension_semantics=("parallel",), vmem_limit_bytes=VMEM_LIMIT),
        cost_estimate=pl.CostEstimate(flops=g * width * width, transcendentals=0,
                                      bytes_accessed=2 * g * width * width + 4 * g * hh * lc),
        name="ssm_toeplitz",
    )(kq)


def _ssm_kernel(u_ref, m_ref, w_ref, r_ref, a_ref, d_ref, y_ref, e_scr, *, n_chunks, batch):
    hc = SSM_GROUP_CH
    lc = SSM_CHUNK
    rows_n = u_ref.shape[1] // lc
    u = jnp.concatenate([u_ref[r:r + 1, :].reshape(rows_n, lc) for r in range(hc)], axis=1)
    y = jnp.dot(u, m_ref[0], preferred_element_type=F32)
    e_scr[...] = jnp.dot(u, w_ref[0], preferred_element_type=F32)
    ar = a_ref[0, 0:1, :]
    ai = a_ref[0, 1:2, :]

    def body(c, s):
        rows = pl.ds(c, batch, stride=n_chunks)
        e_c = e_scr[rows, :]
        e_scr[rows, :] = s
        return ar * s + ai * pltpu.roll(s, SSM_STATE, axis=1) + e_c

    lax.fori_loop(0, n_chunks, body, jnp.zeros((batch, 2 * SSM_STATE), F32))
    y = y + jnp.dot(e_scr[...].astype(BF16), r_ref[0], preferred_element_type=F32)
    y = y + u.astype(F32) * d_ref[0]
    for r in range(hc):
        y_ref[0, r] = y[:, r * lc:(r + 1) * lc]


def _ssm(u_t, m, w, r, a, d, n_chunks, batch):
    hc, lc = SSM_GROUP_CH, SSM_CHUNK
    g = u_t.shape[0] // hc
    rows = u_t.shape[1] // lc
    width = hc * lc
    kern = functools.partial(_ssm_kernel, n_chunks=n_chunks, batch=batch)
    return pl.pallas_call(
        kern,
        grid=(g,),
        in_specs=[
            pl.BlockSpec((hc, rows * lc), lambda i: (i, 0)),
            pl.BlockSpec((1, width, width), lambda i: (i, 0, 0)),
            pl.BlockSpec((1, width, 2 * SSM_STATE), lambda i: (i, 0, 0)),
            pl.BlockSpec((1, 2 * SSM_STATE, width), lambda i: (i, 0, 0)),
            pl.BlockSpec((1, 8, 2 * SSM_STATE), lambda i: (i, 0, 0)),
            pl.BlockSpec((1, 1, width), lambda i: (i, 0, 0)),
        ],
        out_specs=pl.BlockSpec((1, hc, rows, lc), lambda i: (i, 0, 0, 0)),
        out_shape=jax.ShapeDtypeStruct((g, hc, rows, lc), F32),
        scratch_shapes=[pltpu.VMEM((rows, 2 * SSM_STATE), F32)],
        compiler_params=pltpu.CompilerParams(
            dimension_semantics=("parallel",), vmem_limit_bytes=VMEM_LIMIT),
        cost_estimate=pl.CostEstimate(flops=2 * g * rows * width * (width + 4 * SSM_STATE), transcendentals=0,
                                      bytes_accessed=g * (2 * width * (width + 4 * SSM_STATE) + 6 * rows * width)),
        name="ssm",
    )(u_t, m, w, r, a, d)


def _ssm_tables(lam_re, lam_im, log_dt, b_re, b_im, c_re, c_im, d_skip):
    hp = lax.Precision.HIGHEST
    lc = SSM_CHUNK
    g, p = lam_re.shape
    hc = SSM_GROUP_CH
    dt = jnp.exp(log_dt)[:, None]
    a_re = jnp.exp(lam_re * dt) * jnp.cos(lam_im * dt)
    a_im = jnp.exp(lam_re * dt) * jnp.sin(lam_im * dt)
    den = lam_re * lam_re + lam_im * lam_im
    nr = a_re - 1.0
    z_re = (nr * lam_re + a_im * lam_im) / den
    z_im = (a_im * lam_re - nr * lam_im) / den
    bb_re = z_re[..., None] * b_re - z_im[..., None] * b_im
    bb_im = z_re[..., None] * b_im + z_im[..., None] * b_re
    tau = jnp.arange(lc + 1, dtype=F32)[:, None, None]
    mag = jnp.exp(tau * (lam_re * dt)[None])
    ang = tau * (lam_im * dt)[None]
    p_re = mag * jnp.cos(ang)
    p_im = mag * jnp.sin(ang)
    ab_re = p_re[:lc, :, :, None] * bb_re[None] - p_im[:lc, :, :, None] * bb_im[None]
    ab_im = p_re[:lc, :, :, None] * bb_im[None] + p_im[:lc, :, :, None] * bb_re[None]
    kq = (jnp.einsum('ghp,tgpk->gkht', c_re, ab_re, precision=hp)
          - jnp.einsum('ghp,tgpk->gkht', c_im, ab_im, precision=hp)).reshape(g, hc * hc, lc)
    m = _toeplitz(kq)
    rp_re = p_re[:lc][::-1].transpose(1, 0, 2)[:, None]
    rp_im = p_im[:lc][::-1].transpose(1, 0, 2)[:, None]
    tb_re = bb_re.transpose(0, 2, 1)[:, :, None, :]
    tb_im = bb_im.transpose(0, 2, 1)[:, :, None, :]
    w = jnp.concatenate([(rp_re * tb_re - rp_im * tb_im).reshape(g, hc * lc, p),
                         (rp_re * tb_im + rp_im * tb_re).reshape(g, hc * lc, p)], axis=-1)
    np_re = p_re[1:].transpose(1, 2, 0)[:, :, None, :]
    np_im = p_im[1:].transpose(1, 2, 0)[:, :, None, :]
    tc_re = c_re.transpose(0, 2, 1)[:, :, :, None]
    tc_im = c_im.transpose(0, 2, 1)[:, :, :, None]
    r = jnp.concatenate([(np_re * tc_re - np_im * tc_im).reshape(g, p, hc * lc),
                         -(np_re * tc_im + np_im * tc_re).reshape(g, p, hc * lc)], axis=1)
    al_re, al_im = p_re[lc], p_im[lc]
    a = jnp.zeros((g, 8, 2 * p), F32)
    a = a.at[:, 0, :].set(jnp.concatenate([al_re, al_re], axis=-1))
    a = a.at[:, 1, :].set(jnp.concatenate([-al_im, al_im], axis=-1))
    d = jnp.repeat(d_skip, lc, axis=-1).reshape(g, 1, hc * lc)
    return m, w.astype(BF16), r.astype(BF16), a, d


def _attn_kernel(q_ref, k_ref, v_ref, o_ref, m_ref, l_ref, acc_ref, *, blk):
    i = pl.program_id(1)
    q = q_ref[0]
    m_ref[...] = jnp.full_like(m_ref, -jnp.inf)
    l_ref[...] = jnp.zeros_like(l_ref)
    acc_ref[...] = jnp.zeros_like(acc_ref)

    def step(j, masked):
        off = pl.multiple_of(j * blk, blk)
        k = k_ref[0, pl.ds(off, blk), :]
        v = v_ref[0, pl.ds(off, blk), :]
        s = lax.dot_general(q, k, (((1,), (1,)), ((), ())), preferred_element_type=F32)
        if masked:
            row = lax.broadcasted_iota(I32, s.shape, 0)
            col = lax.broadcasted_iota(I32, s.shape, 1)
            s = jnp.where(row >= col, s, -jnp.inf)
        m_prev = m_ref[...]
        m_new = jnp.maximum(m_prev, jnp.max(s, axis=1, keepdims=True))
        p = jnp.exp(s - m_new)
        alpha = jnp.exp(m_prev - m_new)
        l_ref[...] = alpha * l_ref[...] + jnp.sum(p, axis=1, keepdims=True)
        acc_ref[...] = alpha * acc_ref[...] + jnp.dot(p.astype(BF16), v, preferred_element_type=F32)
        m_ref[...] = m_new

    def body(j, c):
        step(j, False)
        return c

    lax.fori_loop(0, i, body, 0)
    step(i, True)
    o_ref[0] = acc_ref[...] / l_ref[...]


def _attention(qa, ka, v, blk):
    bh, s, _ = qa.shape
    kern = functools.partial(_attn_kernel, blk=blk)
    return pl.pallas_call(
        kern,
        grid=(bh, s // blk),
        in_specs=[
            pl.BlockSpec((1, blk, AUG), lambda b, i: (b, i, 0)),
            pl.BlockSpec((1, s, AUG), lambda b, i: (b, 0, 0)),
            pl.BlockSpec((1, s, HEAD_DIM), lambda b, i: (b, 0, 0)),
        ],
        out_specs=pl.BlockSpec((1, blk, HEAD_DIM), lambda b, i: (b, i, 0)),
        out_shape=jax.ShapeDtypeStruct((bh, s, HEAD_DIM), F32),
        scratch_shapes=[pltpu.VMEM((blk, 1), F32), pltpu.VMEM((blk, 1), F32), pltpu.VMEM((blk, HEAD_DIM), F32)],
        compiler_params=pltpu.CompilerParams(
            dimension_semantics=("parallel", "arbitrary"), vmem_limit_bytes=VMEM_LIMIT),
        cost_estimate=pl.CostEstimate(flops=bh * s * s * (AUG + HEAD_DIM), transcendentals=bh * s * s // 2,
                                      bytes_accessed=bh * s * (2 * 2 * AUG + 2 * HEAD_DIM + 4 * HEAD_DIM)),
        name="fox_attn",
    )(qa, ka, v)


def _out_proj_kernel(y_ref, att_ref, h_ref, wg_ref, bg_ref, gs_ref, ga_ref, wos_ref, woa_ref, o_ref):
    g = _gelu(y_ref[...])
    z = jnp.dot(wg_ref[...], g.astype(BF16), preferred_element_type=F32) + bg_ref[...]
    o = g * _sigmoid(z)
    a = o * lax.rsqrt(jnp.mean(o * o, axis=0, keepdims=True) + RMS_EPS) * gs_ref[...]
    acc = h_ref[0] + lax.dot_general(a.astype(BF16), wos_ref[...], (((0,), (0,)), ((), ())),
                                     preferred_element_type=F32)
    ssq = jnp.zeros((acc.shape[0], 1), F32)
    for hh in range(N_HEADS):
        t = att_ref[0, hh]
        ssq = ssq + jnp.sum(t * t, axis=1, keepdims=True)
    inv = lax.rsqrt(ssq / (N_HEADS * HEAD_DIM) + RMS_EPS)
    for hh in range(N_HEADS):
        bh = att_ref[0, hh] * inv * ga_ref[hh]
        acc = acc + jnp.dot(bh.astype(BF16), woa_ref[hh], preferred_element_type=F32)
    o_ref[0] = acc


def _out_proj(y_t, y_att, h, wg_t, bg, gs, ga, wos, woa, tm):
    b, s, d = h.shape
    w = y_t.shape[0]
    ns = s // tm
    return pl.pallas_call(
        _out_proj_kernel,
        grid=(b, ns),
        in_specs=[
            pl.BlockSpec((w, tm), lambda i, j: (0, i * ns + j)),
            pl.BlockSpec((1, N_HEADS, tm, HEAD_DIM), lambda i, j: (i, 0, j, 0)),
            pl.BlockSpec((1, tm, d), lambda i, j: (i, j, 0)),
            pl.BlockSpec((w, w), lambda i, j: (0, 0)),
            pl.BlockSpec((w, 1), lambda i, j: (0, 0)),
            pl.BlockSpec((w, 1), lambda i, j: (0, 0)),
            pl.BlockSpec((N_HEADS, 1, HEAD_DIM), lambda i, j: (0, 0, 0)),
            pl.BlockSpec((w, d), lambda i, j: (0, 0)),
            pl.BlockSpec((N_HEADS, HEAD_DIM, d), lambda i, j: (0, 0, 0)),
        ],
        out_specs=pl.BlockSpec((1, tm, d), lambda i, j: (i, j, 0)),
        out_shape=jax.ShapeDtypeStruct((b, s, d), F32),
        compiler_params=pltpu.CompilerParams(
            dimension_semantics=("parallel", "parallel"), vmem_limit_bytes=VMEM_LIMIT),
        cost_estimate=pl.CostEstimate(flops=2 * b * s * (w * w + 2 * w * d), transcendentals=2 * b * s * w,
                                      bytes_accessed=b * s * (8 * w + 8 * d) + 2 * (w * w + 2 * w * d)),
        name="out_proj",
    )(y_t, y_att, h, wg_t, bg, gs, ga, wos, woa)


def _take_top(vals, payload, k):
    n_rows = vals.shape[0]
    rows = lax.broadcasted_iota(I32, vals.shape, 0)
    tops, picks = [], []
    for _ in range(k):
        m = jnp.max(vals, axis=0, keepdims=True)
        arg = jnp.min(jnp.where(vals == m, rows, n_rows), axis=0, keepdims=True)
        hit = rows == arg
        tops.append(m)
        picks.append(arg if payload is None else jnp.max(jnp.where(hit, payload, -1), axis=0, keepdims=True))
        vals = jnp.where(hit, -jnp.inf, vals)
    return jnp.concatenate(tops, axis=0), jnp.concatenate(picks, axis=0)


def _route_kernel(h_ref, g_ref, wq_ref, keys_ref, hn_ref, idx_ref, gate_ref):
    x = h_ref[...]
    hn = _rms(x, g_ref[...])
    hn_ref[...] = _pack_bf16_pairs(hn)
    q = jnp.dot(hn.astype(BF16), wq_ref[...], preferred_element_type=F32)
    k = PEER_TOPK
    idx_rows, gate_rows = [], []
    for hh in range(PEER_HEADS):
        qh = q[:, 128 * hh: 128 * (hh + 1)].astype(BF16)
        sc = lax.dot_general(keys_ref[hh], qh, (((1,), (1,)), ((), ())), preferred_element_type=F32)
        t1, i1 = _take_top(sc[:PEER_KEYS], None, k)
        t2, i2 = _take_top(sc[PEER_KEYS:], None, k)
        widths = [k // (a + 1) for a in range(k)]
        pad = -sum(widths) % 8
        cand = jnp.concatenate([t1[a:a + 1] + t2[:widths[a]] for a in range(k)]
                               + [jnp.full((pad, t1.shape[1]), -jnp.inf, F32)], axis=0)
        cidx = jnp.concatenate([i1[a:a + 1] * PEER_KEYS + i2[:widths[a]] for a in range(k)]
                               + [jnp.full((pad, t1.shape[1]), -1, I32)], axis=0)
        best, idx = _take_top(cand, cidx, k)
        e = jnp.exp(best - best[0:1])
        gate = e / jnp.sum(e, axis=0, keepdims=True)
        idx_rows.append(idx)
        gate_rows.append(gate)
    idx_ref[...] = jnp.concatenate(idx_rows, axis=0).T
    gate_ref[...] = jnp.concatenate(gate_rows, axis=0).T


def _route(h2, g, wq, keys_cat, tm):
    t, d = h2.shape
    nk = PEER_HEADS * PEER_TOPK
    return pl.pallas_call(
        _route_kernel,
        grid=(t // tm,),
        in_specs=[
            pl.BlockSpec((tm, d), lambda i: (i, 0)),
            pl.BlockSpec((1, d), lambda i: (0, 0)),
            pl.BlockSpec((d, PEER_HEADS * 128), lambda i: (0, 0)),
            pl.BlockSpec((PEER_HEADS, 2 * PEER_KEYS, 128), lambda i: (0, 0, 0)),
        ],
        out_specs=[
            pl.BlockSpec((tm, d // 2), lambda i: (i, 0)),
            pl.BlockSpec((tm, nk), lambda i: (i, 0)),
            pl.BlockSpec((tm, nk), lambda i: (i, 0)),
        ],
        out_shape=[
            jax.ShapeDtypeStruct((t, d // 2), I32),
            jax.ShapeDtypeStruct((t, nk), I32),
            jax.ShapeDtypeStruct((t, nk), F32),
        ],
        compiler_params=pltpu.CompilerParams(
            dimension_semantics=("parallel",), vmem_limit_bytes=VMEM_LIMIT),
        cost_estimate=pl.CostEstimate(flops=2 * t * d * PEER_HEADS * 128 + 2 * t * PEER_HEADS * 256 * 128,
                                      transcendentals=t * nk, bytes_accessed=t * (8 * d + 8 * nk) + 2 * d * PEER_HEADS * 128),
        name="peer_route",
    )(h2, g, wq, keys_cat)


def _final_norm_kernel(x_ref, g_ref, o_ref):
    o_ref[...] = _rms(x_ref[...], g_ref[...])


def _final_norm(x2, g, tm):
    t, d = x2.shape
    return pl.pallas_call(
        _final_norm_kernel,
        grid=(t // tm,),
        in_specs=[pl.BlockSpec((tm, d), lambda i: (i, 0)), pl.BlockSpec((1, d), lambda i: (0, 0))],
        out_specs=pl.BlockSpec((tm, d), lambda i: (i, 0)),
        out_shape=jax.ShapeDtypeStruct((t, d), F32),
        compiler_params=pltpu.CompilerParams(dimension_semantics=("parallel",)),
        cost_estimate=pl.CostEstimate(flops=4 * t * d, transcendentals=t, bytes_accessed=8 * t * d),
        name="final_norm",
    )(x2, g)


SC_WORKERS = 32
SC_CORES = 2
SC_LANES = 16
SC_TOK = 16
SC_ROWS = 32
SC_BUFS = 4
SC_SLAB = 8
HI16 = -65536


def _sc_params():
    cp = pltpu.CompilerParams()
    if "needs_layout_passes" in pltpu.CompilerParams.__dataclass_fields__:
        cp = pltpu.CompilerParams(needs_layout_passes=False)
    return cp


def _sc_worker_id():
    return lax.axis_index("s") * SC_CORES + lax.axis_index("c")


def _tree_sum(xs):
    xs = list(xs)
    while len(xs) > 1:
        xs = [xs[i] + xs[i + 1] for i in range(0, len(xs) - 1, 2)] + ([xs[-1]] if len(xs) % 2 else [])
    return xs[0]


def _gelu_via_exp(x):
    c = math.sqrt(2.0 / math.pi)
    z = c * (x + 0.044715 * (x * x * x))
    return 0.5 * x * (2.0 - 2.0 / (1.0 + jnp.exp(2.0 * z)))


def _round_to_bf16_bits(bits):
    return bits + 0x7FFF + (lax.shift_right_logical(bits, jnp.int32(16)) & 1)


def _pack_bf16_pairs(x):
    w = x.shape[-1] // 2
    r = _round_to_bf16_bits(lax.bitcast_convert_type(x, I32))
    return lax.shift_right_logical(r[..., :w], jnp.int32(16)) | (r[..., w:] & HI16)


def _lo_f32(word):
    return lax.bitcast_convert_type(lax.shift_left(word, jnp.int32(16)), F32)


def _hi_f32(word):
    return lax.bitcast_convert_type(word & HI16, F32)


def _peer_experts_body(hn_hbm, h_hbm, idx_hbm, gate_hbm, u_hbm, v_hbm, out_hbm,
                       h_v, o_v, idx_v, gate_v, coef_v, cw_v, acc_v, *bufs, tok_per_worker):
    rows, sems = bufs[:SC_BUFS], bufs[SC_BUFS:]
    half = hn_hbm.shape[1]
    nk = PEER_HEADS * PEER_TOPK
    per_tok = nk // SC_ROWS
    n = SC_TOK * per_tok
    n_slab = half // (SC_SLAB * SC_LANES)
    lane = lax.iota(I32, SC_LANES)
    wid = _sc_worker_id()

    def packed_mul(word, other_bf16):
        return plsc.bitcast(plsc.bitcast(word, BF16) * other_bf16, I32)

    def gather_u(k, slot):
        return pltpu.make_async_copy(u_hbm.at[idx_v.at[k]], rows[slot], sems[slot])

    def gather_v(k, slot):
        return pltpu.make_async_copy(v_hbm.at[idx_v.at[k]], rows[slot], sems[slot])

    def start_ahead(ahead, slot):
        @pl.when(ahead < n)
        def _():
            gather_u(ahead, slot).start(priority=slot % 2)

        @pl.when(jnp.logical_and(ahead >= n, ahead < 2 * n))
        def _():
            gather_v(ahead - n, slot).start(priority=slot % 2)

    def score(rw, k):
        t = k // per_tok
        for sl in range(n_slab):
            base = sl * SC_SLAB * SC_LANES
            hw = [plsc.bitcast(h_v[t, pl.ds(base + SC_LANES * j, SC_LANES)], BF16) for j in range(SC_SLAB)]

            @plsc.parallel_loop(0, SC_ROWS, 1)
            def _(r):
                terms = []
                for j in range(SC_SLAB):
                    p = packed_mul(rw[r, pl.ds(base + SC_LANES * j, SC_LANES)], hw[j])
                    terms += [_lo_f32(p), _hi_f32(p)]
                p = _tree_sum(terms)
                off = pl.multiple_of(r * SC_LANES, SC_LANES)
                if sl == 0:
                    acc_v[pl.ds(off, SC_LANES)] = p
                else:
                    acc_v[pl.ds(off, SC_LANES)] = acc_v[pl.ds(off, SC_LANES)] + p

        for rg in range(SC_ROWS // SC_LANES):
            base = rg * SC_LANES * SC_LANES
            tot = _tree_sum([plsc.load_gather(acc_v, [lane * SC_LANES + (base + j)]) for j in range(SC_LANES)])
            coef_v[pl.ds(pl.multiple_of(k * SC_ROWS + rg * SC_LANES, SC_LANES), SC_LANES)] = tot

    def combine(rw, k):
        t = k // per_tok
        for sl in range(n_slab):
            base = sl * SC_SLAB * SC_LANES
            acc0 = (tuple(o_v[t, pl.ds(base + SC_LANES * j, SC_LANES)] for j in range(SC_SLAB))
                    + tuple(o_v[t, pl.ds(half + base + SC_LANES * j, SC_LANES)] for j in range(SC_SLAB)))

            def rbody(r, acc):
                w = plsc.bitcast(plsc.load_gather(cw_v, [jnp.full((SC_LANES,), k * SC_ROWS + r, I32)]), BF16)
                new = list(acc)
                for j in range(SC_SLAB):
                    p = packed_mul(rw[r, pl.ds(base + SC_LANES * j, SC_LANES)], w)
                    new[j] = acc[j] + _lo_f32(p)
                    new[SC_SLAB + j] = acc[SC_SLAB + j] + _hi_f32(p)
                return tuple(new)

            acc = lax.fori_loop(0, SC_ROWS, rbody, acc0)
            for j in range(SC_SLAB):
                o_v[t, pl.ds(base + SC_LANES * j, SC_LANES)] = acc[j]
                o_v[t, pl.ds(half + base + SC_LANES * j, SC_LANES)] = acc[SC_SLAB + j]

    def batch(bi, carry):
        t0 = pl.multiple_of(wid * tok_per_worker + bi * SC_TOK, SC_TOK)
        pltpu.sync_copy(idx_hbm.at[pl.ds(t0 * per_tok, n)], idx_v)
        for s in range(SC_BUFS - 1):
            gather_u(s, s).start(priority=s % 2)
        pltpu.sync_copy(hn_hbm.at[pl.ds(t0, SC_TOK)], h_v)
        pltpu.sync_copy(h_hbm.at[pl.ds(t0, SC_TOK)], o_v)
        pltpu.sync_copy(gate_hbm.at[pl.ds(t0 * nk, SC_TOK * nk)], gate_v)

        def score_group(i, c):
            for s in range(SC_BUFS):
                k = i * SC_BUFS + s
                start_ahead(k + SC_BUFS - 1, (s + SC_BUFS - 1) % SC_BUFS)
                gather_u(k, s).wait()
                score(rows[s], k)
            return c

        lax.fori_loop(0, n // SC_BUFS, score_group, 0)

        @plsc.parallel_loop(0, SC_TOK * nk // SC_LANES, 1)
        def _(j):
            off = pl.multiple_of(j * SC_LANES, SC_LANES)
            c = _gelu_via_exp(coef_v[pl.ds(off, SC_LANES)]) * gate_v[pl.ds(off, SC_LANES)]
            hi = _round_to_bf16_bits(lax.bitcast_convert_type(c, I32)) & HI16
            cw_v[pl.ds(off, SC_LANES)] = hi | lax.shift_right_logical(hi, jnp.int32(16))

        def combine_group(i, c):
            for s in range(SC_BUFS):
                k = i * SC_BUFS + s
                start_ahead(n + k + SC_BUFS - 1, (s + SC_BUFS - 1) % SC_BUFS)
                gather_v(k, s).wait()
                combine(rows[s], k)
            return c

        lax.fori_loop(0, n // SC_BUFS, combine_group, 0)
        pltpu.sync_copy(o_v, out_hbm.at[pl.ds(t0, SC_TOK)])
        return carry

    lax.fori_loop(0, tok_per_worker // SC_TOK, batch, 0)


def _peer_experts(hn_words, h2, idx4, gate_flat, u_words, v_words):
    t, d = h2.shape
    nk = PEER_HEADS * PEER_TOPK
    mesh = plsc.VectorSubcoreMesh(core_axis_name="c", subcore_axis_name="s")
    body = functools.partial(_peer_experts_body, tok_per_worker=t // SC_WORKERS)
    return pl.kernel(
        body,
        out_type=jax.ShapeDtypeStruct((t, d), F32),
        mesh=mesh,
        scratch_types=[
            pltpu.VMEM((SC_TOK, d // 2), I32),
            pltpu.VMEM((SC_TOK, d), F32),
            pltpu.VMEM((SC_TOK * nk // SC_ROWS, SC_ROWS), I32),
            pltpu.VMEM((SC_TOK * nk,), F32),
            pltpu.VMEM((SC_TOK * nk,), F32),
            pltpu.VMEM((SC_TOK * nk,), I32),
            pltpu.VMEM((SC_ROWS * SC_LANES,), F32),
        ] + [pltpu.VMEM((SC_ROWS, d // 2), I32)] * SC_BUFS + [pltpu.SemaphoreType.DMA] * SC_BUFS,
        compiler_params=_sc_params(),
        cost_estimate=pl.CostEstimate(flops=4 * t * nk * d, transcendentals=t * nk,
                                      bytes_accessed=4 * t * nk * d + 10 * t * d),
        name="peer_experts_sc",
    )(hn_words, h2, idx4, gate_flat, u_words, v_words)


def _mixers(h, l, norm1_g, w_in, ssm, fox_b_f, g_ssm_out, g_attn_out, w_o, ssm_w_glu, ssm_b_glu, tm, blk):
    b, s, d = h.shape
    w = N_HEADS * HEAD_DIM
    wl = w_in[l]
    wu_t = wl[:, :w].T.astype(BF16)
    wm = wl[:, w:4 * w].astype(BF16)
    wf = jnp.pad(wl[:, 4 * w:], ((0, 0), (0, 128 - N_HEADS)))
    bf = jnp.pad(fox_b_f[l], (0, 128 - N_HEADS)).reshape(1, 128)
    u_t, qa, ka, v = _in_proj(h, norm1_g[l].reshape(1, d), wu_t, wm, wf, bf, tm)

    lc = SSM_CHUNK
    nc = s // lc
    g = w // SSM_GROUP_CH
    y4 = _ssm(u_t, *ssm, n_chunks=nc, batch=b)
    y_t = y4.reshape(w, b * s)

    y_att = _attention(qa.reshape(b * N_HEADS, s, AUG), ka.reshape(b * N_HEADS, s, AUG),
                       v.reshape(b * N_HEADS, s, HEAD_DIM), blk).reshape(b, N_HEADS, s, HEAD_DIM)

    wo = w_o[l].astype(BF16)
    return _out_proj(y_t, y_att, h, ssm_w_glu[l].T.astype(BF16), ssm_b_glu[l].reshape(w, 1),
                     g_ssm_out[l].reshape(w, 1), g_attn_out[l].reshape(N_HEADS, 1, HEAD_DIM),
                     wo[:w], wo[w:].reshape(N_HEADS, HEAD_DIM, d), tm)


def _keys_cat(keys_l):
    z = jnp.zeros_like(keys_l[:, 0])
    top = jnp.concatenate([keys_l[:, 0], z], axis=-1)
    bot = jnp.concatenate([z, keys_l[:, 1]], axis=-1)
    return jnp.concatenate([top, bot], axis=1).astype(BF16)


def kernel(x, norm1_g, w_in, ssm_lambda_re, ssm_lambda_im, ssm_log_dt, ssm_b_re, ssm_b_im, ssm_c_re, ssm_c_im, ssm_d, ssm_w_glu, ssm_b_glu, fox_b_f, g_ssm_out, g_attn_out, w_o, norm2_g, peer_w_q, peer_keys, peer_u, peer_v, norm_f):
    b, s, d = x.shape
    depth = w_in.shape[0]
    nk = PEER_HEADS * PEER_TOPK
    tm = min(512, s)
    blk = min(512, s)
    assert b % (2 * N_PARTS) == 0
    bh = b // N_PARTS
    ssm_tabs = [_ssm_tables(ssm_lambda_re[l], ssm_lambda_im[l], ssm_log_dt[l], ssm_b_re[l], ssm_b_im[l],
                            ssm_c_re[l], ssm_c_im[l], ssm_d[l]) for l in range(depth)]

    def dense_stage(h, l, early=None):
        t = h.shape[0] * s
        h = _mixers(h, l, norm1_g, w_in, ssm_tabs[l], fox_b_f, g_ssm_out, g_attn_out, w_o, ssm_w_glu, ssm_b_glu,
                    tm, blk)
        if early is not None:
            h, _ = lax.optimization_barrier((h, early))
        h2 = h.reshape(t, d)
        hn, idx, gate = _route(h2, norm2_g[l].reshape(1, d), peer_w_q[l].astype(BF16), _keys_cat(peer_keys[l]),
                               min(256, t))
        return h2, hn, idx.reshape(t * nk // SC_ROWS, SC_ROWS), gate.reshape(t * nk)

    tabs = [(_pack_bf16_pairs(peer_u[l]), _pack_bf16_pairs(peer_v[l])) for l in range(depth)]

    def expert_stage(st, l):
        h2, hn, idx4, gate = st
        return _peer_experts(hn, h2, idx4, gate, tabs[l][0], tabs[l][1]).reshape(-1, s, d)

    def after(value, st):
        value, _ = lax.optimization_barrier((value, st[3]))
        return value

    hs = [x[p * bh:(p + 1) * bh] for p in range(N_PARTS)]
    prev = None
    chain = []
    for l in range(depth):
        for p in range(N_PARTS):
            pieces = [hs[p][:bh // 2], hs[p][bh // 2:]] if prev is None else [hs[p]]
            done = []
            for piece in pieces:
                early = tabs[0] if prev is None else (tabs[l + 1] if p == N_PARTS - 1 and l + 1 < depth else None)
                if prev is not None:
                    piece = after(piece, prev)
                if len(chain) >= 2 and l == 0:
                    piece, _ = lax.optimization_barrier((piece, chain[-2]))
                st = dense_stage(piece, l, early)
                done.append(expert_stage(st, l))
                chain.append(done[-1])
                prev = st
            hs[p] = done[0] if len(done) == 1 else jnp.concatenate(done, axis=0)
    t = bh * s
    outs = [_final_norm(h.reshape(t, d), norm_f.reshape(1, d), min(512, t)).reshape(bh, s, d) for h in hs]
    return jnp.concatenate(outs, axis=0)
```
